```python
import math
import jax, jax.numpy as jnp
from jax import lax
import numpy as np

D_MODEL = 2048
BATCH = 8
SEQ = 2048
DEPTH = 2
DEC_BATCH = 8
DEC_SEQ = 64
PAST_LEN = 4096

CHUNK = 64
Q_BLOCK = 128
D_R = D_MODEL // 2
DH_R = 128
H_R = D_R // DH_R
ROPE_BASE = 10000.0
D_DV = D_MODEL // 2
DH_V = 128
H_D = D_DV // DH_V
DH_QK = DH_V // 2
D_DQK = H_D * 2 * DH_QK
D_FF = 5632
N_BUCKETS = 32
MAX_DISTANCE = 128
EPS = 1e-6
N_PROJ = 4 * D_R + 2 * D_DQK + D_DV + 2 * D_MODEL

kernel_name = "hybrid_retention_diffattn_streaming_step"


def _split_points():
    widths = [D_R, D_R, D_R, D_R, D_DQK, D_DQK, D_DV, D_MODEL, D_MODEL]
    pts, acc = [], 0
    for w in widths[:-1]:
        acc += w
        pts.append(acc)
    return pts


def _rms(x):
    xf = x.astype(jnp.float32)
    return xf * lax.rsqrt(jnp.mean(xf * xf, axis=-1, keepdims=True) + EPS)


def rms_norm(x, g):
    return (_rms(x) * g.astype(jnp.float32)).astype(x.dtype)


def swiglu(x, w_gate, w_up, w_down):
    return (jax.nn.silu(x @ w_gate) * (x @ w_up)) @ w_down


def rotary(x, pos):
    half = x.shape[-1] // 2
    inv = ROPE_BASE ** (-jnp.arange(half, dtype=jnp.float32) / half)
    ang = pos.astype(jnp.float32)[:, None] * inv[None, :]
    cos = jnp.cos(ang)[None, :, None, :]
    sin = jnp.sin(ang)[None, :, None, :]
    x1, x2 = x[..., :half], x[..., half:]
    return jnp.concatenate([x1 * cos - x2 * sin, x1 * sin + x2 * cos], axis=-1)


def t5_bucket(rel):
    nb = N_BUCKETS // 2
    max_exact = nb // 2
    base = jnp.where(rel > 0, nb, 0)
    n = jnp.abs(rel)
    large = max_exact + (jnp.log(jnp.maximum(n, 1).astype(jnp.float32) / max_exact)
                         / math.log(MAX_DISTANCE / max_exact) * (nb - max_exact)).astype(jnp.int32)
    large = jnp.minimum(large, nb - 1)
    return base + jnp.where(n < max_exact, n, large)


def retention(q, k, v, s0, length):
    B, S, H, D = q.shape
    nc = S // length
    log_g = jnp.log(1.0 - 2.0 ** (-5.0 - jnp.arange(H, dtype=jnp.float32)))
    qc = q.reshape(B, nc, length, H, D)
    kc = k.reshape(B, nc, length, H, D)
    vc = v.reshape(B, nc, length, H, D)
    idx = jnp.arange(length, dtype=jnp.float32)
    dmat = jnp.exp(log_g[:, None, None] * jnp.abs(idx[:, None] - idx[None, :]))
    s = jnp.einsum('bnihd,bnjhd->bnhij', qc, kc) * dmat
    intra = jnp.einsum('bnhij,bnjhe->bnihe', s, vc)
    w_end = jnp.exp(log_g[None, :] * (length - 1 - idx)[:, None])
    kv = jnp.einsum('bnjhd,bnjhe->nbhde', kc * w_end[:, :, None], vc)
    decay_c = jnp.exp(log_g * length)[None, :, None, None]

    def step(s_prev, kv_n):
        return decay_c * s_prev + kv_n, s_prev

    s_final, s_before = lax.scan(step, s0, kv)
    w_q = jnp.exp(log_g[None, :] * (idx + 1.0)[:, None])
    inter = jnp.einsum('bnihd,nbhde->bnihe', qc * w_q[:, :, None], s_before)
    return (intra + inter).reshape(B, S, H, D), s_final


def diff_attend(q, k, v, qpos, kpos, lam, rel_bias):
    s = jnp.einsum('bqhcd,bkhcd->bchqk', q, k, preferred_element_type=jnp.float32) * (DH_QK ** -0.5)
    bias = jnp.transpose(rel_bias[t5_bucket(kpos[None, :] - qpos[:, None])], (2, 0, 1)).astype(jnp.float32)
    mask = (kpos[None, :] // CHUNK) <= (qpos[:, None] // CHUNK)
    s = jnp.where(mask, s + bias[None, None], -jnp.inf)
    p = jax.nn.softmax(s, axis=-1)
    a = p[:, 0] - lam * p[:, 1]
    return jnp.einsum('bhqk,bkhd->bqhd', a.astype(v.dtype), v)


def diff_prompt(q, k, v, pos, lam, rel_bias):
    B, S = q.shape[0], q.shape[1]
    nqb = S // Q_BLOCK
    qb = jnp.swapaxes(q.reshape(B, nqb, Q_BLOCK, H_D, 2, DH_QK), 0, 1)
    pb = pos.reshape(nqb, Q_BLOCK)
    out = lax.map(lambda a: diff_attend(a[0], k, v, a[1], pos, lam, rel_bias), (qb, pb))
    return jnp.swapaxes(out, 0, 1).reshape(B, S, H_D, DH_V)


def token_mix(h, pos, l, w_in, q_norm, k_norm, lambda_q1, lambda_k1, lambda_q2, lambda_k2,
              subln, w_ret_up, w_dif_up, w_out, rel_bias, s0, cache_k, cache_v):
    B, S, _ = h.shape
    f32 = jnp.float32
    proj = h @ w_in[l]
    rq, rk, rv, rg, dq, dk, dv, gr, gd = jnp.split(proj, _split_points(), axis=-1)
    rq = rotary(rq.reshape(B, S, H_R, DH_R).astype(f32), pos)
    rk = rotary(rk.reshape(B, S, H_R, DH_R).astype(f32), pos) * (DH_R ** -0.5)
    rv = rv.reshape(B, S, H_R, DH_R).astype(f32)
    length = CHUNK if cache_k is None else S
    ro, s_new = retention(rq, rk, rv, s0, length)
    ro = _rms(ro).reshape(B, S, D_R).astype(h.dtype) * jax.nn.silu(rg)
    y_a = ro @ w_ret_up[l]
    dq = rms_norm(dq.reshape(B, S, H_D, 2, DH_QK), q_norm[l])
    dk = rms_norm(dk.reshape(B, S, H_D, 2, DH_QK), k_norm[l])
    dv = dv.reshape(B, S, H_D, DH_V)
    lam_init = 0.8 - 0.6 * math.exp(-0.3 * l)
    lam = (jnp.exp(jnp.sum(lambda_q1[l].astype(f32) * lambda_k1[l].astype(f32)))
           - jnp.exp(jnp.sum(lambda_q2[l].astype(f32) * lambda_k2[l].astype(f32))) + lam_init)
    if cache_k is None:
        do = diff_prompt(dq, dk, dv, pos, lam, rel_bias)
    else:
        past = cache_k.shape[2]
        ck = cache_k[l].reshape(B, past, H_D, 2, DH_QK).astype(dk.dtype)
        k_all = jnp.concatenate([ck, dk], axis=1)
        v_all = jnp.concatenate([cache_v[l].astype(dv.dtype), dv], axis=1)
        kpos = jnp.arange(past + S, dtype=jnp.int32)
        do = diff_attend(dq, k_all, v_all, pos, kpos, lam, rel_bias)
    do = rms_norm(do, subln[l]) * (1.0 - lam_init)
    y_b = do.reshape(B, S, D_DV) @ w_dif_up[l]
    merged = jax.nn.sigmoid(gr) * y_a + jax.nn.sigmoid(gd) * y_b
    return merged @ w_out[l], dk.reshape(B, S, H_D, 2 * DH_QK), dv, s_new


def run_trunk(x, pos, s0_all, cache_k, cache_v, ffn1_norm, ffn1_gate, ffn1_up, ffn1_down, mix_norm,
              w_in, q_norm, k_norm, lambda_q1, lambda_k1, lambda_q2, lambda_k2, subln,
              w_ret_up, w_dif_up, w_out, ffn2_norm, ffn2_gate, ffn2_up, ffn2_down, rel_bias):
    ks, vs, ss = [], [], []
    for l in range(DEPTH):
        h = rms_norm(x, ffn1_norm[l])
        x = x + 0.5 * swiglu(h, ffn1_gate[l], ffn1_up[l], ffn1_down[l])
        h = rms_norm(x, mix_norm[l])
        y, k_new, v_new, s_new = token_mix(h, pos, l, w_in, q_norm, k_norm, lambda_q1, lambda_k1,
                                           lambda_q2, lambda_k2, subln, w_ret_up, w_dif_up, w_out,
                                           rel_bias, s0_all[l], cache_k, cache_v)
        x = x + y
        h = rms_norm(x, ffn2_norm[l])
        x = x + 0.5 * swiglu(h, ffn2_gate[l], ffn2_up[l], ffn2_down[l])
        ks.append(k_new)
        vs.append(v_new)
        ss.append(s_new)
    return x, jnp.stack(ks), jnp.stack(vs), jnp.stack(ss)


def setup_inputs(seed: int = 0) -> dict:
    key = jax.random.key(seed)
    ks = iter(jax.random.split(key, 40))

    def nrm(shape, scale):
        return jax.random.normal(next(ks), shape, jnp.float32) * scale

    def gain(shape):
        return 1.0 + nrm(shape, 0.02)

    return {
        "x_prompt": nrm((BATCH, SEQ, D_MODEL), 1.0),
        "x_sample": nrm((DEC_BATCH, DEC_SEQ, D_MODEL), 1.0),
        "cache_diff_k": nrm((DEPTH, DEC_BATCH, PAST_LEN, H_D, 2 * DH_QK), 1.0),
        "cache_diff_v": nrm((DEPTH, DEC_BATCH, PAST_LEN, H_D, DH_V), 1.0),
        "state_ret": nrm((DEPTH, DEC_BATCH, H_R, DH_R, DH_R), 0.1),
        "ffn1_norm": gain((DEPTH, D_MODEL)),
        "ffn1_gate": nrm((DEPTH, D_MODEL, D_FF), D_MODEL ** -0.5),
        "ffn1_up": nrm((DEPTH, D_MODEL, D_FF), D_MODEL ** -0.5),
        "ffn1_down": nrm((DEPTH, D_FF, D_MODEL), D_FF ** -0.5),
        "mix_norm": gain((DEPTH, D_MODEL)),
        "w_in": nrm((DEPTH, D_MODEL, N_PROJ), D_MODEL ** -0.5),
        "q_norm": gain((DEPTH, DH_QK)),
        "k_norm": gain((DEPTH, DH_QK)),
        "lambda_q1": nrm((DEPTH, DH_QK), 0.1),
        "lambda_k1": nrm((DEPTH, DH_QK), 0.1),
        "lambda_q2": nrm((DEPTH, DH_QK), 0.1),
        "lambda_k2": nrm((DEPTH, DH_QK), 0.1),
        "subln": gain((DEPTH, DH_V)),
        "w_ret_up": nrm((DEPTH, D_R, D_MODEL), D_R ** -0.5),
        "w_dif_up": nrm((DEPTH, D_DV, D_MODEL), D_DV ** -0.5),
        "w_out": nrm((DEPTH, D_MODEL, D_MODEL), D_MODEL ** -0.5),
        "ffn2_norm": gain((DEPTH, D_MODEL)),
        "ffn2_gate": nrm((DEPTH, D_MODEL, D_FF), D_MODEL ** -0.5),
        "ffn2_up": nrm((DEPTH, D_MODEL, D_FF), D_MODEL ** -0.5),
        "ffn2_down": nrm((DEPTH, D_FF, D_MODEL), D_FF ** -0.5),
        "rel_bias": nrm((N_BUCKETS, H_D), 0.1),
    }


def reference(x_prompt, x_sample, cache_diff_k, cache_diff_v, state_ret, ffn1_norm, ffn1_gate, ffn1_up,
              ffn1_down, mix_norm, w_in, q_norm, k_norm, lambda_q1, lambda_k1, lambda_q2, lambda_k2, subln,
              w_ret_up, w_dif_up, w_out, ffn2_norm, ffn2_gate, ffn2_up, ffn2_down, rel_bias):
    weights = (ffn1_norm, ffn1_gate, ffn1_up, ffn1_down, mix_norm, w_in, q_norm, k_norm,
               lambda_q1, lambda_k1, lambda_q2, lambda_k2, subln, w_ret_up, w_dif_up, w_out,
               ffn2_norm, ffn2_gate, ffn2_up, ffn2_down, rel_bias)
    b, s = x_prompt.shape[0], x_prompt.shape[1]
    pos_p = jnp.arange(s, dtype=jnp.int32)
    s0_p = jnp.zeros((DEPTH, b, H_R, DH_R, DH_R), jnp.float32)
    y_prompt, k_p, v_p, st_p = run_trunk(x_prompt, pos_p, s0_p, None, None, *weights)
    past = cache_diff_k.shape[2]
    pos_s = past + jnp.arange(x_sample.shape[1], dtype=jnp.int32)
    y_sample, k_s, v_s, st_s = run_trunk(x_sample, pos_s, state_ret.astype(jnp.float32),
                                         cache_diff_k, cache_diff_v, *weights)
    return (y_prompt, y_sample,
            k_p.astype(cache_diff_k.dtype), v_p.astype(cache_diff_v.dtype), st_p.astype(state_ret.dtype),
            k_s.astype(cache_diff_k.dtype), v_s.astype(cache_diff_v.dtype), st_s.astype(state_ret.dtype))
```

```python
import functools
import math

import numpy as np
import jax
import jax.numpy as jnp
from jax import lax
from jax.experimental import pallas as pl
from jax.experimental.pallas import tpu as pltpu

F32 = jnp.float32
MXU_DTYPE = jnp.bfloat16

CHUNK = 64
HEAD = 128
HALF = HEAD // 2
ROPE_BASE = 10000.0
N_BUCKETS = 32
MAX_DISTANCE = 128
EPS = 1e-6
MASK_VALUE = -1e30

VMEM_LIMIT_CAP = 60 * 1024 * 1024
MIB = 1024 * 1024

ROW_TILES = (512, 256, 128, 64)
FF_TILES = (512, 256, 128)
ATTN_BLOCKS = (256, 128)
RET_BLOCKS = (256, 128, 64)
CACHE_TILES = (512, 256, 128, 64)


def _pick(n, prefs):
    for p in prefs:
        if n % p == 0:
            return p
    raise ValueError(f"no tile in {prefs} divides {n}")


def _params(semantics, est_bytes):
    limit = int(min(max(est_bytes + 8 * MIB, 32 * MIB), VMEM_LIMIT_CAP))
    return pltpu.CompilerParams(dimension_semantics=semantics, vmem_limit_bytes=limit)


def _rms_scale(x):
    return lax.rsqrt(jnp.mean(x * x, axis=-1, keepdims=True) + EPS)


def _ffn_kernel(x_ref, g_ref, wg_ref, wu_ref, wd_ref, o_ref, h_ref, acc_ref, *, nf):
    f = pl.program_id(1)

    @pl.when(f == 0)
    def _():
        x = x_ref[...]
        h_ref[...] = (x * _rms_scale(x) * g_ref[...]).astype(h_ref.dtype)
        acc_ref[...] = jnp.zeros_like(acc_ref)

    h = h_ref[...]
    gate = jnp.dot(h, wg_ref[...], preferred_element_type=F32)
    up = jnp.dot(h, wu_ref[...], preferred_element_type=F32)
    act = (gate * jax.nn.sigmoid(gate) * up).astype(wd_ref.dtype)
    acc_ref[...] += jnp.dot(act, wd_ref[...], preferred_element_type=F32)

    @pl.when(f == nf - 1)
    def _():
        o_ref[...] = x_ref[...] + 0.5 * acc_ref[...]


def _ffn(x, gain, wg, wu, wd, layer):
    m, d = x.shape
    ff = wg.shape[-1]
    tm = _pick(m, ROW_TILES)
    tf = _pick(ff, FF_TILES)
    nf = ff // tf
    wbytes = jnp.dtype(wg.dtype).itemsize
    est = 2 * (2 * tm * d * 4 + 3 * d * tf * wbytes) + tm * d * (4 + wbytes) + 4 * tm * tf * 4
    return pl.pallas_call(
        functools.partial(_ffn_kernel, nf=nf),
        grid=(m // tm, nf),
        in_specs=[
            pl.BlockSpec((tm, d), lambda i, f: (i, 0)),
            pl.BlockSpec((None, 1, d), lambda i, f: (layer, 0, 0)),
            pl.BlockSpec((None, d, tf), lambda i, f: (layer, 0, f)),
            pl.BlockSpec((None, d, tf), lambda i, f: (layer, 0, f)),
            pl.BlockSpec((None, tf, d), lambda i, f: (layer, f, 0)),
        ],
        out_specs=pl.BlockSpec((tm, d), lambda i, f: (i, 0)),
        out_shape=jax.ShapeDtypeStruct((m, d), F32),
        scratch_shapes=[pltpu.VMEM((tm, d), wg.dtype), pltpu.VMEM((tm, d), F32)],
        compiler_params=_params(("parallel", "arbitrary"), est),
        name="swiglu_half_step",
    )(x, gain, wg, wu, wd)


SEG_RQ, SEG_RK, SEG_RV, SEG_RG, SEG_DQ, SEG_DK, SEG_DV, SEG_GR, SEG_GD, N_SEG = 0, 1, 2, 3, 4, 5, 6, 7, 9, 11


def _rotate_half_pairs(a, cos2, sin2):
    return a * cos2 + pltpu.roll(a, HALF, 1) * sin2


def _component_rms_norm(a, gain2):
    lo = lax.broadcasted_iota(jnp.int32, a.shape, 1) < HALF
    sq = a * a
    s_all = jnp.sum(sq, axis=-1, keepdims=True)
    s_lo = jnp.sum(jnp.where(lo, sq, 0.0), axis=-1, keepdims=True)
    ms = jnp.where(lo, s_lo, s_all - s_lo) * (1.0 / HALF)
    return a * lax.rsqrt(ms + EPS) * gain2


def _proj_kernel(x_ref, g_ref, w_ref, cos_ref, sin_ref, qn_ref, kn_ref, p_ref, kf_ref, vf_ref, h_ref, *, nheads):
    j = pl.program_id(1)

    @pl.when(j == 0)
    def _():
        x = x_ref[...]
        h_ref[...] = (x * _rms_scale(x) * g_ref[...]).astype(h_ref.dtype)

    acc = jnp.dot(h_ref[...], w_ref[...], preferred_element_type=F32)
    heads = [slice(h * HEAD, (h + 1) * HEAD) for h in range(nheads)]

    @pl.when(j == SEG_RQ)
    def _():
        for sl in heads:
            p_ref[:, sl] = _rotate_half_pairs(acc[:, sl], cos_ref[...], sin_ref[...]).astype(p_ref.dtype)

    @pl.when(j == SEG_RK)
    def _():
        for sl in heads:
            r = _rotate_half_pairs(acc[:, sl], cos_ref[...], sin_ref[...]) * (HEAD ** -0.5)
            p_ref[:, sl] = r.astype(p_ref.dtype)

    @pl.when(j == SEG_RV)
    def _():
        p_ref[...] = acc.astype(p_ref.dtype)

    @pl.when(j == SEG_RG)
    def _():
        p_ref[...] = (acc * jax.nn.sigmoid(acc)).astype(p_ref.dtype)

    @pl.when(j == SEG_DQ)
    def _():
        for sl in heads:
            r = _component_rms_norm(acc[:, sl], qn_ref[...]) * (HALF ** -0.5)
            p_ref[:, sl] = r.astype(p_ref.dtype)

    @pl.when(j == SEG_DK)
    def _():
        for sl in heads:
            r = _component_rms_norm(acc[:, sl], kn_ref[...])
            kf_ref[:, sl] = r
            p_ref[:, sl] = r.astype(p_ref.dtype)

    @pl.when(j == SEG_DV)
    def _():
        vf_ref[...] = acc
        p_ref[...] = acc.astype(p_ref.dtype)

    @pl.when(j >= SEG_GR)
    def _():
        p_ref[...] = jax.nn.sigmoid(acc).astype(p_ref.dtype)


def _project(x, gain, w_in, cos2, sin2, qn2, kn2, layer):
    m, d = x.shape
    u = d // 2
    assert w_in.shape[-1] == N_SEG * u and u % HEAD == 0
    tm = _pick(m, ROW_TILES)
    wbytes = jnp.dtype(w_in.dtype).itemsize
    est = (2 * (tm * d * 4 + d * u * wbytes + tm * u * wbytes + 2 * tm * u * 4 + 2 * tm * HEAD * 4)
           + tm * d * wbytes + 3 * tm * u * 4)
    return pl.pallas_call(
        functools.partial(_proj_kernel, nheads=u // HEAD),
        grid=(m // tm, N_SEG),
        in_specs=[
            pl.BlockSpec((tm, d), lambda i, j: (i, 0)),
            pl.BlockSpec((None, 1, d), lambda i, j: (layer, 0, 0)),
            pl.BlockSpec((None, d, u), lambda i, j: (layer, 0, j)),
            pl.BlockSpec((tm, HEAD), lambda i, j: (i, 0)),
            pl.BlockSpec((tm, HEAD), lambda i, j: (i, 0)),
            pl.BlockSpec((None, 1, HEAD), lambda i, j: (layer, 0, 0)),
            pl.BlockSpec((None, 1, HEAD), lambda i, j: (layer, 0, 0)),
        ],
        out_specs=[
            pl.BlockSpec((tm, u), lambda i, j: (i, j)),
            pl.BlockSpec((tm, u), lambda i, j: (i, 0)),
            pl.BlockSpec((tm, u), lambda i, j: (i, 0)),
        ],
        out_shape=[
            jax.ShapeDtypeStruct((m, N_SEG * u), w_in.dtype),
            jax.ShapeDtypeStruct((m, u), F32),
            jax.ShapeDtypeStruct((m, u), F32),
        ],
        scratch_shapes=[pltpu.VMEM((tm, d), w_in.dtype)],
        compiler_params=_params(("parallel", "arbitrary"), est),
        name="input_projection",
    )(x, gain, w_in, cos2, sin2, qn2, kn2)


def _retention_kernel(q_ref, k_ref, v_ref, g_ref, s0_ref, d_ref, wq_ref, we_ref, dec_ref, *rest,
                      nheads, nblk, aliased):
    o_ref, sout_ref, st_ref = rest[-3:]
    t = pl.program_id(1)

    @pl.when(t == 0)
    def _():
        st_ref[...] = s0_ref[...]

    for h in range(nheads):
        sl = slice(h * HEAD, (h + 1) * HEAD)
        q = q_ref[:, sl]
        k = k_ref[:, sl]
        v = v_ref[:, sl]
        s = lax.dot_general(q, k, (((1,), (1,)), ((), ())), preferred_element_type=F32) * d_ref[h]
        o = jnp.dot(s.astype(v.dtype), v, preferred_element_type=F32)
        state = st_ref[h]
        o = o + wq_ref[:, sl] * jnp.dot(q, state.astype(q.dtype), preferred_element_type=F32)
        kw = (k.astype(F32) * we_ref[:, sl]).astype(k.dtype)
        kv = lax.dot_general(kw, v, (((0,), (0,)), ((), ())), preferred_element_type=F32)
        st_ref[h] = state * dec_ref[h:h + 1, :] + kv
        r = o * _rms_scale(o)
        o_ref[:, sl] = (r * g_ref[:, sl].astype(F32)).astype(o_ref.dtype)

    @pl.when(t == nblk - 1)
    def _():
        sout_ref[...] = st_ref[...]


def _retention_tables(t, nheads):
    log_g = jnp.log(1.0 - 2.0 ** (-5.0 - jnp.arange(nheads, dtype=F32)))
    idx = jnp.arange(t, dtype=F32)
    dist = jnp.abs(idx[:, None] - idx[None, :])
    ci = np.arange(t) // CHUNK
    visible = jnp.asarray(ci[None, :] <= ci[:, None])
    dmat = jnp.where(visible[None], jnp.exp(log_g[:, None, None] * dist[None]), 0.0)
    wq = jnp.exp(log_g[None, :] * (idx + 1.0)[:, None])
    we = jnp.exp(log_g[None, :] * (t - 1.0 - idx)[:, None])
    dec = jnp.exp(log_g * t)
    expand = lambda a: jnp.repeat(a, HEAD, axis=1)
    return dmat, expand(wq), expand(we), jnp.broadcast_to(dec[:, None], (nheads, HEAD))


def _retention(p_all, s0, t, nbatch, nblk, row_block0, prev_out=None):
    m = p_all.shape[0]
    nheads = s0.shape[1]
    u = nheads * HEAD
    dmat, wq, we, dec = _retention_tables(t, nheads)
    rows = lambda c: pl.BlockSpec((t, u), lambda b, i: (row_block0 + b * nblk + i, c))
    whole = lambda a: pl.BlockSpec(a.shape, lambda b, i: (0,) * a.ndim)
    state_spec = pl.BlockSpec((None, nheads, HEAD, HEAD), lambda b, i: (b, 0, 0, 0))
    in_specs = [rows(SEG_RQ), rows(SEG_RK), rows(SEG_RV), rows(SEG_RG), state_spec,
                whole(dmat), whole(wq), whole(we), whole(dec)]
    args = [p_all, p_all, p_all, p_all, s0, dmat, wq, we, dec]
    aliases = {}
    if prev_out is not None:
        in_specs.append(pl.BlockSpec(memory_space=pl.ANY))
        args.append(prev_out)
        aliases = {len(args) - 1: 0}
    pbytes = jnp.dtype(p_all.dtype).itemsize
    est = (2 * (5 * t * u * pbytes + 2 * nheads * HEAD * HEAD * 4 + nheads * t * t * 4 + 2 * t * u * 4)
           + nheads * HEAD * HEAD * 4 + 6 * t * max(t, HEAD) * 4)
    return pl.pallas_call(
        functools.partial(_retention_kernel, nheads=nheads, nblk=nblk, aliased=prev_out is not None),
        grid=(nbatch, nblk),
        in_specs=in_specs,
        out_specs=[
            pl.BlockSpec((t, u), lambda b, i: (row_block0 + b * nblk + i, 0)),
            state_spec,
        ],
        out_shape=[
            jax.ShapeDtypeStruct((m, u), p_all.dtype),
            jax.ShapeDtypeStruct((nbatch, nheads, HEAD, HEAD), F32),
        ],
        scratch_shapes=[pltpu.VMEM((nheads, HEAD, HEAD), F32)],
        input_output_aliases=aliases,
        compiler_params=_params(("parallel", "arbitrary"), est),
        name="retention",
    )(*args)


def _bucket_thresholds():
    nb = N_BUCKETS // 2
    me = nb // 2
    out = []
    for k in range(1, nb - me):
        n = me
        while n ** (nb - me) * me ** k < me ** (nb - me) * MAX_DISTANCE ** k:
            n += 1
        out.append(n)
    return out


def _t5_bucket_np(rel):
    nb = N_BUCKETS // 2
    me = nb // 2
    n = np.abs(rel)
    large = np.full(rel.shape, me, np.int64)
    for thr in _bucket_thresholds():
        large += (n >= thr)
    large = np.minimum(large, nb - 1)
    return (np.where(rel > 0, nb, 0) + np.where(n < me, n, large)).astype(np.int32)


def _bias_kernel(rb_ref, idx_ref, mask_ref, o_ref):
    h = pl.program_id(0)
    idx = idx_ref[...]
    out = mask_ref[...]
    for b in range(N_BUCKETS):
        out = out + jnp.where(idx == b, rb_ref[b, h], 0.0)
    o_ref[...] = out


def _bias_table(rel_bias, qpos, kpos):
    nheads = rel_bias.shape[1]
    rel = kpos[None, :] - qpos[:, None]
    idx = jnp.asarray(_t5_bucket_np(rel))
    mask = jnp.asarray(np.where((kpos[None, :] // CHUNK) <= (qpos[:, None] // CHUNK), 0.0, MASK_VALUE)
                       .astype(np.float32))
    nq, nk = rel.shape
    return pl.pallas_call(
        _bias_kernel,
        grid=(nheads,),
        in_specs=[
            pl.BlockSpec(memory_space=pltpu.SMEM),
            pl.BlockSpec((nq, nk), lambda h: (0, 0)),
            pl.BlockSpec((nq, nk), lambda h: (0, 0)),
        ],
        out_specs=pl.BlockSpec((None, nq, nk), lambda h: (h, 0, 0)),
        out_shape=jax.ShapeDtypeStruct((nheads, nq, nk), F32),
        compiler_params=_params(("arbitrary",), 6 * nq * nk * 4),
        name="relative_bias_table",
    )(rel_bias, idx, mask)


def _stack_components(q):
    lo = lax.broadcasted_iota(jnp.int32, q.shape, 1) < HALF
    zero = jnp.zeros_like(q)
    return jnp.concatenate([jnp.where(lo, q, zero), jnp.where(lo, zero, q)], axis=0)


def _softmax_step(qs, k, v, bias, m_ref, l_ref, acc_ref):
    s = lax.dot_general(qs, k, (((1,), (1,)), ((), ())), preferred_element_type=F32)
    if bias.ndim == 2:
        t = bias.shape[0]
        s = (s.reshape(2, t, s.shape[-1]) + bias[None]).reshape(s.shape)
    else:
        s = s + bias
    m_prev = m_ref[...]
    m_new = jnp.maximum(m_prev, jnp.max(s, axis=-1, keepdims=True))
    alpha = jnp.exp(m_prev - m_new)
    p = jnp.exp(s - m_new)
    l_ref[...] = alpha * l_ref[...] + jnp.sum(p, axis=-1, keepdims=True)
    acc_ref[...] = alpha * acc_ref[...] + jnp.dot(p.astype(v.dtype), v, preferred_element_type=F32)
    m_ref[...] = m_new


def _softmax_init(m_ref, l_ref, acc_ref):
    m_ref[...] = jnp.full_like(m_ref, MASK_VALUE)
    l_ref[...] = jnp.zeros_like(l_ref)
    acc_ref[...] = jnp.zeros_like(acc_ref)


def _lambda_value(lam_ref, lam_init):
    a = lam_ref[...]
    e1 = jnp.exp(jnp.sum(a[0:1] * a[1:2], axis=-1, keepdims=True))
    e2 = jnp.exp(jnp.sum(a[2:3] * a[3:4], axis=-1, keepdims=True))
    return e1 - e2 + lam_init


def _diff_finish(t, lam, lam_init, subln, l_ref, acc_ref):
    acc = acc_ref[...]
    l = l_ref[...]
    o = acc[:t] / l[:t] - lam * (acc[t:] / l[t:])
    return o * _rms_scale(o) * subln * (1.0 - lam_init)


def _attn_prompt_kernel(rb_ref, q_ref, k_ref, v_ref, bias_ref, lam_ref, sub_ref, o_ref, m_ref, l_ref, acc_ref,
                        *, t, lam_init, far_bucket):
    h = pl.program_id(1)
    qi = pl.program_id(2)
    qs = _stack_components(q_ref[...])
    _softmax_init(m_ref, l_ref, acc_ref)
    far_bias = rb_ref[far_bucket, h]

    def tile(j):
        start = pl.multiple_of(j * t, t)
        return k_ref[pl.ds(start, t), :], v_ref[pl.ds(start, t), :]

    def far_step(j, carry):
        k, v = tile(j)
        _softmax_step(qs, k, v, far_bias, m_ref, l_ref, acc_ref)
        return carry

    lax.fori_loop(0, jnp.maximum(qi - 1, 0), far_step, 0)

    @pl.when(qi > 0)
    def _():
        k, v = tile(qi - 1)
        _softmax_step(qs, k, v, bias_ref[:, :t], m_ref, l_ref, acc_ref)

    k, v = tile(qi)
    _softmax_step(qs, k, v, bias_ref[:, t:], m_ref, l_ref, acc_ref)

    lam = _lambda_value(lam_ref, lam_init)
    o_ref[...] = _diff_finish(t, lam, lam_init, sub_ref[...], l_ref, acc_ref).astype(o_ref.dtype)


def _attn_prompt(p_all, rel_bias, lam_params, subln2, layer, nbatch, seq, nheads, lam_init):
    m = p_all.shape[0]
    u = nheads * HEAD
    t = _pick(seq, ATTN_BLOCKS)
    assert t % CHUNK == 0 and t + 1 >= _bucket_thresholds()[-1]
    nq = seq // t
    hb = u // HEAD
    r = np.arange(t)
    bias = _bias_table(rel_bias, r + t, np.arange(2 * t))
    pbytes = jnp.dtype(p_all.dtype).itemsize
    est = (2 * (2 * t * HEAD * pbytes + 2 * seq * HEAD * pbytes + 2 * t * t * 4)
           + 2 * t * (HEAD + 2 * 128) * 4 + 8 * 2 * t * t * 4)
    return pl.pallas_call(
        functools.partial(_attn_prompt_kernel, t=t, lam_init=lam_init, far_bucket=N_BUCKETS // 2 - 1),
        grid=(nbatch, nheads, nq),
        in_specs=[
            pl.BlockSpec(memory_space=pltpu.SMEM),
            pl.BlockSpec((t, HEAD), lambda b, h, i: (b * nq + i, SEG_DQ * hb + h)),
            pl.BlockSpec((seq, HEAD), lambda b, h, i: (b, SEG_DK * hb + h)),
            pl.BlockSpec((seq, HEAD), lambda b, h, i: (b, SEG_DV * hb + h)),
            pl.BlockSpec((None, t, 2 * t), lambda b, h, i: (h, 0, 0)),
            pl.BlockSpec((None, 4, HALF), lambda b, h, i: (layer, 0, 0)),
            pl.BlockSpec((None, 1, HEAD), lambda b, h, i: (layer, 0, 0)),
        ],
        out_specs=pl.BlockSpec((t, HEAD), lambda b, h, i: (b * nq + i, h)),
        out_shape=jax.ShapeDtypeStruct((m, u), p_all.dtype),
        scratch_shapes=[pltpu.VMEM((2 * t, 1), F32), pltpu.VMEM((2 * t, 1), F32), pltpu.VMEM((2 * t, HEAD), F32)],
        compiler_params=_params(("parallel", "parallel", "arbitrary"), est),
        name="diff_attention_prompt",
    )(rel_bias, p_all, p_all, p_all, bias, lam_params, subln2)


def _attn_sample_kernel(q_ref, kn_ref, vn_ref, kc_ref, vc_ref, bc_ref, bn_ref, lam_ref, sub_ref, prev_ref,
                        o_ref, m_ref, l_ref, acc_ref, *, t, tk, ncache, lam_init):
    del prev_ref
    qs = _stack_components(q_ref[...])
    _softmax_init(m_ref, l_ref, acc_ref)

    def cache_step(j, carry):
        start = pl.multiple_of(j * tk, tk)
        k = kc_ref[pl.ds(start, tk), :].astype(qs.dtype)
        v = vc_ref[pl.ds(start, tk), :].astype(qs.dtype)
        _softmax_step(qs, k, v, bc_ref[j], m_ref, l_ref, acc_ref)
        return carry

    lax.fori_loop(0, ncache, cache_step, 0)
    _softmax_step(qs, kn_ref[...], vn_ref[...], bn_ref[...], m_ref, l_ref, acc_ref)

    lam = _lambda_value(lam_ref, lam_init)
    o_ref[...] = _diff_finish(t, lam, lam_init, sub_ref[...], l_ref, acc_ref).astype(o_ref.dtype)


def _attn_sample(p_all, cache_k, cache_v, rel_bias, lam_params, subln2, prev_out, layer, nbatch, t, past,
                 row0, nheads, lam_init):
    m = p_all.shape[0]
    u = nheads * HEAD
    hb = u // HEAD
    assert row0 % t == 0
    rb0 = row0 // t
    tk = _pick(past, CACHE_TILES)
    qpos = past + np.arange(t)
    ncache = past // tk
    bias_c = _bias_table(rel_bias, qpos, np.arange(past))
    bias_c = bias_c.reshape(nheads, t, ncache, tk).transpose(0, 2, 1, 3)
    bias_n = _bias_table(rel_bias, qpos, qpos)
    pbytes = jnp.dtype(p_all.dtype).itemsize
    est = (2 * (3 * t * HEAD * pbytes + 2 * past * HEAD * 4 + t * past * 4 + t * t * 4 + t * HEAD * pbytes)
           + 2 * t * (HEAD + 2 * 128) * 4 + 8 * 2 * t * tk * 4)
    return pl.pallas_call(
        functools.partial(_attn_sample_kernel, t=t, tk=tk, ncache=ncache, lam_init=lam_init),
        grid=(nbatch, nheads),
        in_specs=[
            pl.BlockSpec((t, HEAD), lambda b, h: (rb0 + b, SEG_DQ * hb + h)),
            pl.BlockSpec((t, HEAD), lambda b, h: (rb0 + b, SEG_DK * hb + h)),
            pl.BlockSpec((t, HEAD), lambda b, h: (rb0 + b, SEG_DV * hb + h)),
            pl.BlockSpec((None, past, HEAD), lambda b, h: (layer, b, h)),
            pl.BlockSpec((None, past, HEAD), lambda b, h: (layer, b, h)),
            pl.BlockSpec((None, ncache, t, tk), lambda b, h: (h, 0, 0, 0)),
            pl.BlockSpec((None, t, t), lambda b, h: (h, 0, 0)),
            pl.BlockSpec((None, 4, HALF), lambda b, h: (layer, 0, 0)),
            pl.BlockSpec((None, 1, HEAD), lambda b, h: (layer, 0, 0)),
            pl.BlockSpec(memory_space=pl.ANY),
        ],
        out_specs=pl.BlockSpec((t, HEAD), lambda b, h: (rb0 + b, h)),
        out_shape=jax.ShapeDtypeStruct((m, u), p_all.dtype),
        scratch_shapes=[pltpu.VMEM((2 * t, 1), F32), pltpu.VMEM((2 * t, 1), F32), pltpu.VMEM((2 * t, HEAD), F32)],
        input_output_aliases={9: 0},
        compiler_params=_params(("parallel", "arbitrary"), est),
        name="diff_attention_sample",
    )(p_all, p_all, p_all, cache_k, cache_v, bias_c, bias_n, lam_params, subln2, prev_out)


def _merge_kernel(a_ref, b_ref, ga0_ref, ga1_ref, gb0_ref, gb1_ref, wa_ref, wb_ref, o_ref, *, u):
    ya = jnp.dot(a_ref[...], wa_ref[...], preferred_element_type=F32)
    yb = jnp.dot(b_ref[...], wb_ref[...], preferred_element_type=F32)
    for c, (ga, gb) in enumerate(((ga0_ref, gb0_ref), (ga1_ref, gb1_ref))):
        sl = slice(c * u, (c + 1) * u)
        o_ref[:, sl] = (ga[...].astype(F32) * ya[:, sl] + gb[...].astype(F32) * yb[:, sl]).astype(o_ref.dtype)


def _merge(ret_out, dif_out, p_all, w_ret_up, w_dif_up, layer):
    m, u = ret_out.shape
    d = 2 * u
    tm = _pick(m, ROW_TILES)
    gate = lambda c: pl.BlockSpec((tm, u), lambda i: (i, c))
    rows = pl.BlockSpec((tm, u), lambda i: (i, 0))
    wspec = pl.BlockSpec((None, u, d), lambda i: (layer, 0, 0))
    pbytes = jnp.dtype(p_all.dtype).itemsize
    est = 2 * (6 * tm * u * pbytes + 2 * u * d * pbytes + tm * d * pbytes) + 3 * tm * d * 4
    return pl.pallas_call(
        functools.partial(_merge_kernel, u=u),
        grid=(m // tm,),
        in_specs=[rows, rows, gate(SEG_GR), gate(SEG_GR + 1), gate(SEG_GD), gate(SEG_GD + 1), wspec, wspec],
        out_specs=pl.BlockSpec((tm, d), lambda i: (i, 0)),
        out_shape=jax.ShapeDtypeStruct((m, d), p_all.dtype),
        compiler_params=_params(("parallel",), est),
        name="gated_merge",
    )(ret_out, dif_out, p_all, p_all, p_all, p_all, w_ret_up, w_dif_up)


def _out_proj_kernel(x_ref, a_ref, w_ref, o_ref):
    o_ref[...] = x_ref[...] + jnp.dot(a_ref[...], w_ref[...], preferred_element_type=F32)


def _out_proj(x, merged, w_out, layer):
    m, d = x.shape
    tm = _pick(m, ROW_TILES)
    wbytes = jnp.dtype(w_out.dtype).itemsize
    est = 2 * (2 * tm * d * 4 + tm * d * wbytes + d * d * wbytes) + tm * d * 4
    return pl.pallas_call(
        _out_proj_kernel,
        grid=(m // tm,),
        in_specs=[
            pl.BlockSpec((tm, d), lambda i: (i, 0)),
            pl.BlockSpec((tm, d), lambda i: (i, 0)),
            pl.BlockSpec((None, d, d), lambda i: (layer, 0, 0)),
        ],
        out_specs=pl.BlockSpec((tm, d), lambda i: (i, 0)),
        out_shape=jax.ShapeDtypeStruct((m, d), F32),
        compiler_params=_params(("parallel",), est),
        name="output_projection",
    )(x, merged, w_out)


def _rotary_tables(pos):
    inv = ROPE_BASE ** (-jnp.arange(HALF, dtype=F32) / HALF)
    ang = pos.astype(F32)[:, None] * inv[None, :]
    cos, sin = jnp.cos(ang), jnp.sin(ang)
    return jnp.concatenate([cos, cos], axis=-1), jnp.concatenate([-sin, sin], axis=-1)


def kernel(x_prompt, x_sample, cache_diff_k, cache_diff_v, state_ret, ffn1_norm, ffn1_gate, ffn1_up, ffn1_down, mix_norm, w_in, q_norm, k_norm, lambda_q1, lambda_k1, lambda_q2, lambda_k2, subln, w_ret_up, w_dif_up, w_out, ffn2_norm, ffn2_gate, ffn2_up, ffn2_down, rel_bias):
    nb, seq, d = x_prompt.shape
    db, dseq, _ = x_sample.shape
    depth, _, past, nh_d, _ = cache_diff_k.shape
    nh_r = state_ret.shape[2]
    u = d // 2
    assert nh_r * HEAD == u and nh_d * HEAD == u and dseq == CHUNK and seq % CHUNK == 0
    mp, ms = nb * seq, db * dseq

    cast = lambda w: w.astype(MXU_DTYPE)
    wg1, wu1, wd1 = cast(ffn1_gate), cast(ffn1_up), cast(ffn1_down)
    wg2, wu2, wd2 = cast(ffn2_gate), cast(ffn2_up), cast(ffn2_down)
    w_in_c, w_ret_c, w_dif_c, w_out_c = cast(w_in), cast(w_ret_up), cast(w_dif_up), cast(w_out)
    row3 = lambda g: g.reshape(depth, 1, g.shape[-1])
    n1, nmix, n2 = row3(ffn1_norm), row3(mix_norm), row3(ffn2_norm)
    qn2 = row3(jnp.concatenate([q_norm, q_norm], axis=-1))
    kn2 = row3(jnp.concatenate([k_norm, k_norm], axis=-1))
    subln2 = row3(subln)
    lam_params = jnp.stack([lambda_q1, lambda_k1, lambda_q2, lambda_k2], axis=1)
    cache_k = cache_diff_k.reshape(depth, db * past, u)
    cache_v = cache_diff_v.reshape(depth, db * past, u)

    pos = jnp.concatenate([jnp.tile(jnp.arange(seq, dtype=jnp.int32), nb),
                           past + jnp.tile(jnp.arange(dseq, dtype=jnp.int32), db)])
    cos2, sin2 = _rotary_tables(pos)

    x = jnp.concatenate([x_prompt.reshape(mp, d), x_sample.reshape(ms, d)], axis=0)
    zero_state = jnp.zeros((nb, nh_r, HEAD, HEAD), F32)
    t_ret = _pick(seq, RET_BLOCKS)

    ks, vs, states_p, states_s = [], [], [], []
    for l in range(depth):
        lam_init = 0.8 - 0.6 * math.exp(-0.3 * l)
        x = _ffn(x, n1, wg1, wu1, wd1, l)
        p_all, k_new, v_new = _project(x, nmix, w_in_c, cos2, sin2, qn2, kn2, l)

        ret_out, st_p = _retention(p_all, zero_state, t_ret, nb, seq // t_ret, 0)
        ret_out, st_s = _retention(p_all, state_ret[l].astype(F32), dseq, db, 1, mp // dseq, prev_out=ret_out)

        dif_out = _attn_prompt(p_all, rel_bias, lam_params, subln2, l, nb, seq, nh_d, lam_init)
        dif_out = _attn_sample(p_all, cache_k, cache_v, rel_bias, lam_params, subln2, dif_out, l, db, dseq, past,
                               mp, nh_d, lam_init)

        merged = _merge(ret_out, dif_out, p_all, w_ret_c, w_dif_c, l)
        x = _out_proj(x, merged, w_out_c, l)
        x = _ffn(x, n2, wg2, wu2, wd2, l)

        ks.append(k_new)
        vs.append(v_new)
        states_p.append(st_p)
        states_s.append(st_s)

    k_all, v_all = jnp.stack(ks), jnp.stack(vs)
    kv_p = lambda a: a[:, :mp].reshape(depth, nb, seq, nh_d, HEAD)
    kv_s = lambda a: a[:, mp:].reshape(depth, db, dseq, nh_d, HEAD)
    return (x[:mp].reshape(nb, seq, d), x[mp:].reshape(db, dseq, d),
            kv_p(k_all).astype(cache_diff_k.dtype), kv_p(v_all).astype(cache_diff_v.dtype),
            jnp.stack(states_p).astype(state_ret.dtype),
            kv_s(k_all).astype(cache_diff_k.dtype), kv_s(v_all).astype(cache_diff_v.dtype),
            jnp.stack(states_s).astype(state_ret.dtype))
```

```python
import functools
import math

import numpy as np
import jax
import jax.numpy as jnp
from jax import lax
from jax.experimental import pallas as pl
from jax.experimental.pallas import tpu as pltpu

F32 = jnp.float32
MXU_DTYPE = jnp.bfloat16

CHUNK = 64
HEAD = 128
HALF = HEAD // 2
ROPE_BASE = 10000.0
N_BUCKETS = 32
MAX_DISTANCE = 128
EPS = 1e-6
MASK_VALUE = -1e30

VMEM_LIMIT_CAP = 60 * 1024 * 1024
MIB = 1024 * 1024

ROW_TILES = (512, 256, 128, 64)
FF_TILES = (512, 256, 128)
ATTN_BLOCKS = (512, 256, 128)
HEADS_PER_STEP = (2, 1)
RET_BLOCKS = (256, 128, 64)
CACHE_TILES = (512, 256, 128, 64)


def _pick(n, prefs):
    for p in prefs:
        if n % p == 0:
            return p
    raise ValueError(f"no tile in {prefs} divides {n}")


def _params(semantics, est_bytes):
    limit = int(min(max(est_bytes + 8 * MIB, 32 * MIB), VMEM_LIMIT_CAP))
    return pltpu.CompilerParams(dimension_semantics=semantics, vmem_limit_bytes=limit)


def _rms_scale(x):
    return lax.rsqrt(jnp.mean(x * x, axis=-1, keepdims=True) + EPS)


def _ffn_kernel(x_ref, g_ref, wg_ref, wu_ref, wd_ref, o_ref, h_ref, acc_ref, *, nf):
    f = pl.program_id(1)

    @pl.when(f == 0)
    def _():
        x = x_ref[...]
        h_ref[...] = (x * _rms_scale(x) * g_ref[...]).astype(h_ref.dtype)
        acc_ref[...] = jnp.zeros_like(acc_ref)

    h = h_ref[...]
    gate = jnp.dot(h, wg_ref[...], preferred_element_type=F32)
    up = jnp.dot(h, wu_ref[...], preferred_element_type=F32)
    act = (gate * jax.nn.sigmoid(gate) * up).astype(wd_ref.dtype)
    acc_ref[...] += jnp.dot(act, wd_ref[...], preferred_element_type=F32)

    @pl.when(f == nf - 1)
    def _():
        o_ref[...] = x_ref[...] + 0.5 * acc_ref[...]


def _ffn(x, gain, wg, wu, wd, layer):
    m, d = x.shape
    ff = wg.shape[-1]
    tm = _pick(m, ROW_TILES)
    tf = _pick(ff, FF_TILES)
    nf = ff // tf
    wbytes = jnp.dtype(wg.dtype).itemsize
    est = 2 * (2 * tm * d * 4 + 3 * d * tf * wbytes) + tm * d * (4 + wbytes) + 4 * tm * tf * 4
    return pl.pallas_call(
        functools.partial(_ffn_kernel, nf=nf),
        grid=(m // tm, nf),
        in_specs=[
            pl.BlockSpec((tm, d), lambda i, f: (i, 0)),
            pl.BlockSpec((None, 1, d), lambda i, f: (layer, 0, 0)),
            pl.BlockSpec((None, d, tf), lambda i, f: (layer, 0, f)),
            pl.BlockSpec((None, d, tf), lambda i, f: (layer, 0, f)),
            pl.BlockSpec((None, tf, d), lambda i, f: (layer, f, 0)),
        ],
        out_specs=pl.BlockSpec((tm, d), lambda i, f: (i, 0)),
        out_shape=jax.ShapeDtypeStruct((m, d), F32),
        scratch_shapes=[pltpu.VMEM((tm, d), wg.dtype), pltpu.VMEM((tm, d), F32)],
        compiler_params=_params(("parallel", "arbitrary"), est),
        name="swiglu_half_step",
    )(x, gain, wg, wu, wd)


SEG_RQ, SEG_RK, SEG_RV, SEG_RG, SEG_DQ, SEG_DK, SEG_DV, SEG_GR, SEG_GD, N_SEG = 0, 1, 2, 3, 4, 5, 6, 7, 9, 11


def _rotate_half_pairs(a, cos2, sin2):
    return a * cos2 + pltpu.roll(a, HALF, 1) * sin2


def _component_rms_norm(a, gain2):
    lo = lax.broadcasted_iota(jnp.int32, a.shape, 1) < HALF
    sq = a * a
    s_all = jnp.sum(sq, axis=-1, keepdims=True)
    s_lo = jnp.sum(jnp.where(lo, sq, 0.0), axis=-1, keepdims=True)
    ms = jnp.where(lo, s_lo, s_all - s_lo) * (1.0 / HALF)
    return a * lax.rsqrt(ms + EPS) * gain2


def _proj_kernel(x_ref, g_ref, w_ref, cos_ref, sin_ref, qn_ref, kn_ref, p_ref, kf_ref, vf_ref, h_ref, *, nheads):
    j = pl.program_id(1)

    @pl.when(j == 0)
    def _():
        x = x_ref[...]
        h_ref[...] = (x * _rms_scale(x) * g_ref[...]).astype(h_ref.dtype)

    acc = jnp.dot(h_ref[...], w_ref[...], preferred_element_type=F32)
    heads = [slice(h * HEAD, (h + 1) * HEAD) for h in range(nheads)]

    @pl.when(j == SEG_RQ)
    def _():
        for sl in heads:
            p_ref[:, sl] = _rotate_half_pairs(acc[:, sl], cos_ref[...], sin_ref[...]).astype(p_ref.dtype)

    @pl.when(j == SEG_RK)
    def _():
        for sl in heads:
            r = _rotate_half_pairs(acc[:, sl], cos_ref[...], sin_ref[...]) * (HEAD ** -0.5)
            p_ref[:, sl] = r.astype(p_ref.dtype)

    @pl.when(j == SEG_RV)
    def _():
        p_ref[...] = acc.astype(p_ref.dtype)

    @pl.when(j == SEG_RG)
    def _():
        p_ref[...] = (acc * jax.nn.sigmoid(acc)).astype(p_ref.dtype)

    @pl.when(j == SEG_DQ)
    def _():
        for sl in heads:
            r = _component_rms_norm(acc[:, sl], qn_ref[...]) * (HALF ** -0.5)
            p_ref[:, sl] = r.astype(p_ref.dtype)

    @pl.when(j == SEG_DK)
    def _():
        for sl in heads:
            r = _component_rms_norm(acc[:, sl], kn_ref[...])
            kf_ref[:, sl] = r
            p_ref[:, sl] = r.astype(p_ref.dtype)

    @pl.when(j == SEG_DV)
    def _():
        vf_ref[...] = acc
        p_ref[...] = acc.astype(p_ref.dtype)

    @pl.when(j >= SEG_GR)
    def _():
        p_ref[...] = jax.nn.sigmoid(acc).astype(p_ref.dtype)


def _project(x, gain, w_in, cos2, sin2, qn2, kn2, layer):
    m, d = x.shape
    u = d // 2
    assert w_in.shape[-1] == N_SEG * u and u % HEAD == 0
    tm = _pick(m, ROW_TILES)
    wbytes = jnp.dtype(w_in.dtype).itemsize
    est = (2 * (tm * d * 4 + d * u * wbytes + tm * u * wbytes + 2 * tm * u * 4 + 2 * tm * HEAD * 4)
           + tm * d * wbytes + 3 * tm * u * 4)
    return pl.pallas_call(
        functools.partial(_proj_kernel, nheads=u // HEAD),
        grid=(m // tm, N_SEG),
        in_specs=[
            pl.BlockSpec((tm, d), lambda i, j: (i, 0)),
            pl.BlockSpec((None, 1, d), lambda i, j: (layer, 0, 0)),
            pl.BlockSpec((None, d, u), lambda i, j: (layer, 0, j)),
            pl.BlockSpec((tm, HEAD), lambda i, j: (i, 0)),
            pl.BlockSpec((tm, HEAD), lambda i, j: (i, 0)),
            pl.BlockSpec((None, 1, HEAD), lambda i, j: (layer, 0, 0)),
            pl.BlockSpec((None, 1, HEAD), lambda i, j: (layer, 0, 0)),
        ],
        out_specs=[
            pl.BlockSpec((tm, u), lambda i, j: (i, j)),
            pl.BlockSpec((tm, u), lambda i, j: (i, 0)),
            pl.BlockSpec((tm, u), lambda i, j: (i, 0)),
        ],
        out_shape=[
            jax.ShapeDtypeStruct((m, N_SEG * u), w_in.dtype),
            jax.ShapeDtypeStruct((m, u), F32),
            jax.ShapeDtypeStruct((m, u), F32),
        ],
        scratch_shapes=[pltpu.VMEM((tm, d), w_in.dtype)],
        compiler_params=_params(("parallel", "arbitrary"), est),
        name="input_projection",
    )(x, gain, w_in, cos2, sin2, qn2, kn2)


def _retention_kernel(q_ref, k_ref, v_ref, g_ref, s0_ref, d_ref, wq_ref, we_ref, dec_ref, *rest,
                      nheads, nblk, aliased):
    o_ref, sout_ref, st_ref = rest[-3:]
    t = pl.program_id(1)

    @pl.when(t == 0)
    def _():
        st_ref[...] = s0_ref[...]

    for h in range(nheads):
        sl = slice(h * HEAD, (h + 1) * HEAD)
        q = q_ref[:, sl]
        k = k_ref[:, sl]
        v = v_ref[:, sl]
        s = lax.dot_general(q, k, (((1,), (1,)), ((), ())), preferred_element_type=F32) * d_ref[h]
        o = jnp.dot(s.astype(v.dtype), v, preferred_element_type=F32)
        state = st_ref[h]
        o = o + wq_ref[:, sl] * jnp.dot(q, state.astype(q.dtype), preferred_element_type=F32)
        kw = (k.astype(F32) * we_ref[:, sl]).astype(k.dtype)
        kv = lax.dot_general(kw, v, (((0,), (0,)), ((), ())), preferred_element_type=F32)
        st_ref[h] = state * dec_ref[h:h + 1, :] + kv
        r = o * _rms_scale(o)
        o_ref[:, sl] = (r * g_ref[:, sl].astype(F32)).astype(o_ref.dtype)

    @pl.when(t == nblk - 1)
    def _():
        sout_ref[...] = st_ref[...]


def _retention_tables(t, nheads):
    log_g = jnp.log(1.0 - 2.0 ** (-5.0 - jnp.arange(nheads, dtype=F32)))
    idx = jnp.arange(t, dtype=F32)
    dist = jnp.abs(idx[:, None] - idx[None, :])
    ci = np.arange(t) // CHUNK
    visible = jnp.asarray(ci[None, :] <= ci[:, None])
    dmat = jnp.where(visible[None], jnp.exp(log_g[:, None, None] * dist[None]), 0.0)
    wq = jnp.exp(log_g[None, :] * (idx + 1.0)[:, None])
    we = jnp.exp(log_g[None, :] * (t - 1.0 - idx)[:, None])
    dec = jnp.exp(log_g * t)
    expand = lambda a: jnp.repeat(a, HEAD, axis=1)
    return dmat, expand(wq), expand(we), jnp.broadcast_to(dec[:, None], (nheads, HEAD))


def _retention(p_all, s0, t, nbatch, nblk, row_block0, prev_out=None):
    m = p_all.shape[0]
    nheads = s0.shape[1]
    u = nheads * HEAD
    dmat, wq, we, dec = _retention_tables(t, nheads)
    rows = lambda c: pl.BlockSpec((t, u), lambda b, i: (row_block0 + b * nblk + i, c))
    whole = lambda a: pl.BlockSpec(a.shape, lambda b, i: (0,) * a.ndim)
    state_spec = pl.BlockSpec((None, nheads, HEAD, HEAD), lambda b, i: (b, 0, 0, 0))
    in_specs = [rows(SEG_RQ), rows(SEG_RK), rows(SEG_RV), rows(SEG_RG), state_spec,
                whole(dmat), whole(wq), whole(we), whole(dec)]
    args = [p_all, p_all, p_all, p_all, s0, dmat, wq, we, dec]
    aliases = {}
    if prev_out is not None:
        in_specs.append(pl.BlockSpec(memory_space=pl.ANY))
        args.append(prev_out)
        aliases = {len(args) - 1: 0}
    pbytes = jnp.dtype(p_all.dtype).itemsize
    est = (2 * (5 * t * u * pbytes + 2 * nheads * HEAD * HEAD * 4 + nheads * t * t * 4 + 2 * t * u * 4)
           + nheads * HEAD * HEAD * 4 + 6 * t * max(t, HEAD) * 4)
    return pl.pallas_call(
        functools.partial(_retention_kernel, nheads=nheads, nblk=nblk, aliased=prev_out is not None),
        grid=(nbatch, nblk),
        in_specs=in_specs,
        out_specs=[
            pl.BlockSpec((t, u), lambda b, i: (row_block0 + b * nblk + i, 0)),
            state_spec,
        ],
        out_shape=[
            jax.ShapeDtypeStruct((m, u), p_all.dtype),
            jax.ShapeDtypeStruct((nbatch, nheads, HEAD, HEAD), F32),
        ],
        scratch_shapes=[pltpu.VMEM((nheads, HEAD, HEAD), F32)],
        input_output_aliases=aliases,
        compiler_params=_params(("parallel", "arbitrary"), est),
        name="retention",
    )(*args)


def _bucket_thresholds():
    nb = N_BUCKETS // 2
    me = nb // 2
    out = []
    for k in range(1, nb - me):
        n = me
        while n ** (nb - me) * me ** k < me ** (nb - me) * MAX_DISTANCE ** k:
            n += 1
        out.append(n)
    return out


def _t5_bucket_np(rel):
    nb = N_BUCKETS // 2
    me = nb // 2
    n = np.abs(rel)
    large = np.full(rel.shape, me, np.int64)
    for thr in _bucket_thresholds():
        large += (n >= thr)
    large = np.minimum(large, nb - 1)
    return (np.where(rel > 0, nb, 0) + np.where(n < me, n, large)).astype(np.int32)


def _bias_kernel(rb_ref, idx_ref, mask_ref, o_ref):
    h = pl.program_id(0)
    idx = idx_ref[...]
    out = mask_ref[...]
    for b in range(N_BUCKETS):
        out = out + jnp.where(idx == b, rb_ref[b, h], 0.0)
    o_ref[...] = out


def _bias_table(rel_bias, qpos, kpos):
    nheads = rel_bias.shape[1]
    rel = kpos[None, :] - qpos[:, None]
    idx = jnp.asarray(_t5_bucket_np(rel))
    mask = jnp.asarray(np.where((kpos[None, :] // CHUNK) <= (qpos[:, None] // CHUNK), 0.0, MASK_VALUE)
                       .astype(np.float32))
    nq, nk = rel.shape
    return pl.pallas_call(
        _bias_kernel,
        grid=(nheads,),
        in_specs=[
            pl.BlockSpec(memory_space=pltpu.SMEM),
            pl.BlockSpec((nq, nk), lambda h: (0, 0)),
            pl.BlockSpec((nq, nk), lambda h: (0, 0)),
        ],
        out_specs=pl.BlockSpec((None, nq, nk), lambda h: (h, 0, 0)),
        out_shape=jax.ShapeDtypeStruct((nheads, nq, nk), F32),
        compiler_params=_params(("arbitrary",), 6 * nq * nk * 4),
        name="relative_bias_table",
    )(rel_bias, idx, mask)


def _stack_components(q):
    lo = lax.broadcasted_iota(jnp.int32, q.shape, 1) < HALF
    zero = jnp.zeros_like(q)
    return jnp.concatenate([jnp.where(lo, q, zero), jnp.where(lo, zero, q)], axis=0)


def _softmax_step(qs, k, v, bias, stats, g):
    m_ref, l_ref, acc_ref = stats
    s = lax.dot_general(qs, k, (((1,), (1,)), ((), ())), preferred_element_type=F32)
    m_prev = m_ref[g]
    if bias.ndim == 2:
        t = bias.shape[0]
        s = (s.reshape(2, t, s.shape[-1]) + bias[None]).reshape(s.shape)
        m_new = jnp.maximum(m_prev, jnp.max(s, axis=-1, keepdims=True))
        shift = m_new
    else:
        m_new = jnp.maximum(m_prev, jnp.max(s, axis=-1, keepdims=True) + bias)
        shift = m_new - bias
    alpha = jnp.exp(m_prev - m_new)
    p = jnp.exp(s - shift)
    l_ref[g] = alpha * l_ref[g] + jnp.sum(p, axis=-1, keepdims=True)
    acc_ref[g] = alpha * acc_ref[g] + jnp.dot(p.astype(v.dtype), v, preferred_element_type=F32)
    m_ref[g] = m_new


def _softmax_init(stats):
    m_ref, l_ref, acc_ref = stats
    m_ref[...] = jnp.full_like(m_ref, MASK_VALUE)
    l_ref[...] = jnp.zeros_like(l_ref)
    acc_ref[...] = jnp.zeros_like(acc_ref)


def _softmax_scratch(groups, t):
    return [pltpu.VMEM((groups, 2 * t, 1), F32), pltpu.VMEM((groups, 2 * t, 1), F32),
            pltpu.VMEM((groups, 2 * t, HEAD), F32)]


def _lambda_value(lam_ref, lam_init):
    a = lam_ref[...]
    e1 = jnp.exp(jnp.sum(a[0:1] * a[1:2], axis=-1, keepdims=True))
    e2 = jnp.exp(jnp.sum(a[2:3] * a[3:4], axis=-1, keepdims=True))
    return e1 - e2 + lam_init


def _diff_finish(t, lam, lam_init, subln, stats, g):
    _, l_ref, acc_ref = stats
    acc = acc_ref[g]
    l = l_ref[g]
    o = acc[:t] / l[:t] - lam * (acc[t:] / l[t:])
    return o * _rms_scale(o) * subln * (1.0 - lam_init)


def _attn_prompt_kernel(rb_ref, q_ref, k_ref, v_ref, bias_ref, lam_ref, sub_ref, o_ref, *stats,
                        t, groups, lam_init, far_bucket):
    hp = pl.program_id(1)
    qi = pl.program_id(2)
    lanes = [slice(g * HEAD, (g + 1) * HEAD) for g in range(groups)]
    qs = [_stack_components(q_ref[:, sl]) for sl in lanes]
    far_bias = [rb_ref[far_bucket, hp * groups + g] for g in range(groups)]
    _softmax_init(stats)

    def step(j, bias_of):
        start = pl.multiple_of(j * t, t)
        for g, sl in enumerate(lanes):
            _softmax_step(qs[g], k_ref[pl.ds(start, t), sl], v_ref[pl.ds(start, t), sl], bias_of(g), stats, g)

    def far_step(j, carry):
        step(j, lambda g: far_bias[g])
        return carry

    lax.fori_loop(0, jnp.maximum(qi - 1, 0), far_step, 0)

    @pl.when(qi > 0)
    def _():
        step(qi - 1, lambda g: bias_ref[g, :, :t])

    step(qi, lambda g: bias_ref[g, :, t:])

    lam = _lambda_value(lam_ref, lam_init)
    for g, sl in enumerate(lanes):
        o_ref[:, sl] = _diff_finish(t, lam, lam_init, sub_ref[...], stats, g).astype(o_ref.dtype)


def _attn_prompt(p_all, rel_bias, lam_params, subln2, layer, nbatch, seq, nheads, lam_init):
    m = p_all.shape[0]
    u = nheads * HEAD
    t = _pick(seq, ATTN_BLOCKS)
    groups = _pick(nheads, HEADS_PER_STEP)
    assert t % CHUNK == 0 and t + 1 >= _bucket_thresholds()[-1]
    nq = seq // t
    hb = nheads // groups
    w = groups * HEAD
    r = np.arange(t)
    bias = _bias_table(rel_bias, r + t, np.arange(2 * t))
    pbytes = jnp.dtype(p_all.dtype).itemsize
    est = (2 * (2 * t * w * pbytes + 2 * seq * w * pbytes + groups * 2 * t * t * 4)
           + groups * (2 * t * (HEAD + 2 * 128) * 4 + 6 * 2 * t * t * 4))
    return pl.pallas_call(
        functools.partial(_attn_prompt_kernel, t=t, groups=groups, lam_init=lam_init,
                          far_bucket=N_BUCKETS // 2 - 1),
        grid=(nbatch, hb, nq),
        in_specs=[
            pl.BlockSpec(memory_space=pltpu.SMEM),
            pl.BlockSpec((t, w), lambda b, h, i: (b * nq + i, SEG_DQ * hb + h)),
            pl.BlockSpec((seq, w), lambda b, h, i: (b, SEG_DK * hb + h)),
            pl.BlockSpec((seq, w), lambda b, h, i: (b, SEG_DV * hb + h)),
            pl.BlockSpec((groups, t, 2 * t), lambda b, h, i: (h, 0, 0)),
            pl.BlockSpec((None, 4, HALF), lambda b, h, i: (layer, 0, 0)),
            pl.BlockSpec((None, 1, HEAD), lambda b, h, i: (layer, 0, 0)),
        ],
        out_specs=pl.BlockSpec((t, w), lambda b, h, i: (b * nq + i, h)),
        out_shape=jax.ShapeDtypeStruct((m, u), p_all.dtype),
        scratch_shapes=_softmax_scratch(groups, t),
        compiler_params=_params(("parallel", "parallel", "arbitrary"), est),
        name="diff_attention_prompt",
    )(rel_bias, p_all, p_all, p_all, bias, lam_params, subln2)


def _attn_sample_kernel(q_ref, kn_ref, vn_ref, kc_ref, vc_ref, bc_ref, bn_ref, lam_ref, sub_ref, prev_ref,
                        o_ref, *stats, t, tk, ncache, nheads, lam_init):
    del prev_ref
    j = pl.program_id(1)
    lanes = [slice(h * HEAD, (h + 1) * HEAD) for h in range(nheads)]

    @pl.when(j == 0)
    def _():
        _softmax_init(stats)

    for h, sl in enumerate(lanes):
        qs = _stack_components(q_ref[:, sl])
        k = kc_ref[pl.ds(h, tk, stride=nheads), :].astype(qs.dtype)
        v = vc_ref[pl.ds(h, tk, stride=nheads), :].astype(qs.dtype)
        _softmax_step(qs, k, v, bc_ref[h], stats, h)

    @pl.when(j == ncache - 1)
    def _():
        lam = _lambda_value(lam_ref, lam_init)
        for h, sl in enumerate(lanes):
            qs = _stack_components(q_ref[:, sl])
            _softmax_step(qs, kn_ref[:, sl], vn_ref[:, sl], bn_ref[h], stats, h)
            o_ref[:, sl] = _diff_finish(t, lam, lam_init, sub_ref[...], stats, h).astype(o_ref.dtype)


def _attn_sample(p_all, cache_k, cache_v, rel_bias, lam_params, subln2, prev_out, layer, nbatch, t, past,
                 row0, nheads, lam_init):
    m = p_all.shape[0]
    u = nheads * HEAD
    assert row0 % t == 0
    rb0 = row0 // t
    tk = _pick(past, CACHE_TILES)
    qpos = past + np.arange(t)
    ncache = past // tk
    bias_c = _bias_table(rel_bias, qpos, np.arange(past))
    bias_c = bias_c.reshape(nheads, t, ncache, tk).transpose(2, 0, 1, 3)
    bias_n = _bias_table(rel_bias, qpos, qpos)
    pbytes = jnp.dtype(p_all.dtype).itemsize
    est = (2 * (4 * t * u * pbytes + 2 * tk * nheads * HEAD * 4 + nheads * t * (tk + t) * 4)
           + nheads * (2 * t * (HEAD + 2 * 128) * 4 + 6 * 2 * t * tk * 4))
    rows = lambda seg: pl.BlockSpec((t, u), lambda b, j: (rb0 + b, seg))
    cache = pl.BlockSpec((None, tk * nheads, HEAD), lambda b, j: (layer, b * ncache + j, 0))
    return pl.pallas_call(
        functools.partial(_attn_sample_kernel, t=t, tk=tk, ncache=ncache, nheads=nheads, lam_init=lam_init),
        grid=(nbatch, ncache),
        in_specs=[
            rows(SEG_DQ), rows(SEG_DK), rows(SEG_DV), cache, cache,
            pl.BlockSpec((None, nheads, t, tk), lambda b, j: (j, 0, 0, 0)),
            pl.BlockSpec((nheads, t, t), lambda b, j: (0, 0, 0)),
            pl.BlockSpec((None, 4, HALF), lambda b, j: (layer, 0, 0)),
            pl.BlockSpec((None, 1, HEAD), lambda b, j: (layer, 0, 0)),
            pl.BlockSpec(memory_space=pl.ANY),
        ],
        out_specs=pl.BlockSpec((t, u), lambda b, j: (rb0 + b, 0)),
        out_shape=jax.ShapeDtypeStruct((m, u), p_all.dtype),
        scratch_shapes=_softmax_scratch(nheads, t),
        input_output_aliases={9: 0},
        compiler_params=_params(("parallel", "arbitrary"), est),
        name="diff_attention_sample",
    )(p_all, p_all, p_all, cache_k, cache_v, bias_c, bias_n, lam_params, subln2, prev_out)


def _merge_kernel(a_ref, b_ref, ga0_ref, ga1_ref, gb0_ref, gb1_ref, wa_ref, wb_ref, o_ref, *, u):
    ya = jnp.dot(a_ref[...], wa_ref[...], preferred_element_type=F32)
    yb = jnp.dot(b_ref[...], wb_ref[...], preferred_element_type=F32)
    for c, (ga, gb) in enumerate(((ga0_ref, gb0_ref), (ga1_ref, gb1_ref))):
        sl = slice(c * u, (c + 1) * u)
        o_ref[:, sl] = (ga[...].astype(F32) * ya[:, sl] + gb[...].astype(F32) * yb[:, sl]).astype(o_ref.dtype)


def _merge(ret_out, dif_out, p_all, w_ret_up, w_dif_up, layer):
    m, u = ret_out.shape
    d = 2 * u
    tm = _pick(m, ROW_TILES)
    gate = lambda c: pl.BlockSpec((tm, u), lambda i: (i, c))
    rows = pl.BlockSpec((tm, u), lambda i: (i, 0))
    wspec = pl.BlockSpec((None, u, d), lambda i: (layer, 0, 0))
    pbytes = jnp.dtype(p_all.dtype).itemsize
    est = 2 * (6 * tm * u * pbytes + 2 * u * d * pbytes + tm * d * pbytes) + 3 * tm * d * 4
    return pl.pallas_call(
        functools.partial(_merge_kernel, u=u),
        grid=(m // tm,),
        in_specs=[rows, rows, gate(SEG_GR), gate(SEG_GR + 1), gate(SEG_GD), gate(SEG_GD + 1), wspec, wspec],
        out_specs=pl.BlockSpec((tm, d), lambda i: (i, 0)),
        out_shape=jax.ShapeDtypeStruct((m, d), p_all.dtype),
        compiler_params=_params(("parallel",), est),
        name="gated_merge",
    )(ret_out, dif_out, p_all, p_all, p_all, p_all, w_ret_up, w_dif_up)


def _out_proj_kernel(x_ref, a_ref, w_ref, o_ref):
    o_ref[...] = x_ref[...] + jnp.dot(a_ref[...], w_ref[...], preferred_element_type=F32)


def _out_proj(x, merged, w_out, layer):
    m, d = x.shape
    tm = _pick(m, ROW_TILES)
    wbytes = jnp.dtype(w_out.dtype).itemsize
    est = 2 * (2 * tm * d * 4 + tm * d * wbytes + d * d * wbytes) + tm * d * 4
    return pl.pallas_call(
        _out_proj_kernel,
        grid=(m // tm,),
        in_specs=[
            pl.BlockSpec((tm, d), lambda i: (i, 0)),
            pl.BlockSpec((tm, d), lambda i: (i, 0)),
            pl.BlockSpec((None, d, d), lambda i: (layer, 0, 0)),
        ],
        out_specs=pl.BlockSpec((tm, d), lambda i: (i, 0)),
        out_shape=jax.ShapeDtypeStruct((m, d), F32),
        compiler_params=_params(("parallel",), est),
        name="output_projection",
    )(x, merged, w_out)


def _rotary_tables(pos):
    inv = ROPE_BASE ** (-jnp.arange(HALF, dtype=F32) / HALF)
    ang = pos.astype(F32)[:, None] * inv[None, :]
    cos, sin = jnp.cos(ang), jnp.sin(ang)
    return jnp.concatenate([cos, cos], axis=-1), jnp.concatenate([-sin, sin], axis=-1)


def kernel(x_prompt, x_sample, cache_diff_k, cache_diff_v, state_ret, ffn1_norm, ffn1_gate, ffn1_up, ffn1_down, mix_norm, w_in, q_norm, k_norm, lambda_q1, lambda_k1, lambda_q2, lambda_k2, subln, w_ret_up, w_dif_up, w_out, ffn2_norm, ffn2_gate, ffn2_up, ffn2_down, rel_bias):
    nb, seq, d = x_prompt.shape
    db, dseq, _ = x_sample.shape
    depth, _, past, nh_d, _ = cache_diff_k.shape
    nh_r = state_ret.shape[2]
    u = d // 2
    assert nh_r * HEAD == u and nh_d * HEAD == u and dseq == CHUNK and seq % CHUNK == 0
    mp, ms = nb * seq, db * dseq

    cast = lambda w: w.astype(MXU_DTYPE)
    wg1, wu1, wd1 = cast(ffn1_gate), cast(ffn1_up), cast(ffn1_down)
    wg2, wu2, wd2 = cast(ffn2_gate), cast(ffn2_up), cast(ffn2_down)
    w_in_c, w_ret_c, w_dif_c, w_out_c = cast(w_in), cast(w_ret_up), cast(w_dif_up), cast(w_out)
    row3 = lambda g: g.reshape(depth, 1, g.shape[-1])
    n1, nmix, n2 = row3(ffn1_norm), row3(mix_norm), row3(ffn2_norm)
    qn2 = row3(jnp.concatenate([q_norm, q_norm], axis=-1))
    kn2 = row3(jnp.concatenate([k_norm, k_norm], axis=-1))
    subln2 = row3(subln)
    lam_params = jnp.stack([lambda_q1, lambda_k1, lambda_q2, lambda_k2], axis=1)
    cache_k = cache_diff_k.reshape(depth, db * past * nh_d, HEAD)
    cache_v = cache_diff_v.reshape(depth, db * past * nh_d, HEAD)

    pos = jnp.concatenate([jnp.tile(jnp.arange(seq, dtype=jnp.int32), nb),
                           past + jnp.tile(jnp.arange(dseq, dtype=jnp.int32), db)])
    cos2, sin2 = _rotary_tables(pos)

    x = jnp.concatenate([x_prompt.reshape(mp, d), x_sample.reshape(ms, d)], axis=0)
    zero_state = jnp.zeros((nb, nh_r, HEAD, HEAD), F32)
    t_ret = _pick(seq, RET_BLOCKS)

    ks, vs, states_p, states_s = [], [], [], []
    for l in range(depth):
        lam_init = 0.8 - 0.6 * math.exp(-0.3 * l)
        x = _ffn(x, n1, wg1, wu1, wd1, l)
        p_all, k_new, v_new = _project(x, nmix, w_in_c, cos2, sin2, qn2, kn2, l)

        ret_out, st_p = _retention(p_all, zero_state, t_ret, nb, seq // t_ret, 0)
        ret_out, st_s = _retention(p_all, state_ret[l].astype(F32), dseq, db, 1, mp // dseq, prev_out=ret_out)

        dif_out = _attn_prompt(p_all, rel_bias, lam_params, subln2, l, nb, seq, nh_d, lam_init)
        dif_out = _attn_sample(p_all, cache_k, cache_v, rel_bias, lam_params, subln2, dif_out, l, db, dseq, past,
                               mp, nh_d, lam_init)

        merged = _merge(ret_out, dif_out, p_all, w_ret_c, w_dif_c, l)
        x = _out_proj(x, merged, w_out_c, l)
        x = _ffn(x, n2, wg2, wu2, wd2, l)

        ks.append(k_new)
        vs.append(v_new)
        states_p.append(st_p)
        states_s.append(st_s)

    k_all, v_all = jnp.stack(ks), jnp.stack(vs)
    kv_p = lambda a: a[:, :mp].reshape(depth, nb, seq, nh_d, HEAD)
    kv_s = lambda a: a[:, mp:].reshape(depth, db, dseq, nh_d, HEAD)
    return (x[:mp].reshape(nb, seq, d), x[mp:].reshape(db, dseq, d),
            kv_p(k_all).astype(cache_diff_k.dtype), kv_p(v_all).astype(cache_diff_v.dtype),
            jnp.stack(states_p).astype(state_ret.dtype),
            kv_s(k_all).astype(cache_diff_k.dtype), kv_s(v_all).astype(cache_diff_v.dtype),
            jnp.stack(states_s).astype(state_ret.dtype))
```

```python
import functools
import math

import numpy as np
import jax
import jax.numpy as jnp
from jax import lax
from jax.experimental import pallas as pl
from jax.experimental.pallas import tpu as pltpu

F32 = jnp.float32
MXU_DTYPE = jnp.bfloat16

CHUNK = 64
HEAD = 128
HALF = HEAD // 2
ROPE_BASE = 10000.0
N_BUCKETS = 32
MAX_DISTANCE = 128
EPS = 1e-6
MASK_VALUE = -1e30

VMEM_LIMIT_CAP = 60 * 1024 * 1024
MIB = 1024 * 1024

ROW_TILES = (512, 256, 128, 64)
FF_TILES = (512, 256, 128)
ATTN_BLOCKS = (512, 256, 128)
HEADS_PER_STEP = (2, 1)
RET_BLOCKS = (256, 128, 64)
CACHE_TILES = (512, 256, 128, 64)


def _pick(n, prefs):
    for p in prefs:
        if n % p == 0:
            return p
    raise ValueError(f"no tile in {prefs} divides {n}")


def _params(semantics, est_bytes):
    limit = int(min(max(est_bytes + 8 * MIB, 32 * MIB), VMEM_LIMIT_CAP))
    return pltpu.CompilerParams(dimension_semantics=semantics, vmem_limit_bytes=limit)


def _rms_scale(x):
    return lax.rsqrt(jnp.mean(x * x, axis=-1, keepdims=True) + EPS)


def _group_specs(tm, width, n_first, col=0):
    first = pl.BlockSpec((tm, width), lambda i, *_: (jnp.minimum(i, n_first - 1), col))
    second = pl.BlockSpec((tm, width), lambda i, *_: (jnp.maximum(i - n_first, 0), col))
    return [first, second]


def _group_tile(refs, n_first):
    if len(refs) == 1:
        return refs[0][...]
    return jnp.where(pl.program_id(0) < n_first, refs[0][...], refs[1][...])


def _store_group_tile(refs, n_first, value):
    if len(refs) == 1:
        refs[0][...] = value
        return
    i = pl.program_id(0)

    @pl.when(i < n_first)
    def _():
        refs[0][...] = value

    @pl.when(i >= n_first)
    def _():
        refs[1][...] = value


def _ffn_kernel(*refs, nf, n_in, n_out, n_first):
    x_refs = refs[:n_in]
    g_ref, wg_ref, wu_ref, wd_ref = refs[n_in:n_in + 4]
    o_refs = refs[n_in + 4:n_in + 4 + n_out]
    h_ref, acc_ref = refs[n_in + 4 + n_out:]
    f = pl.program_id(1)

    @pl.when(f == 0)
    def _():
        x = _group_tile(x_refs, n_first)
        h_ref[...] = (x * _rms_scale(x) * g_ref[...]).astype(h_ref.dtype)
        acc_ref[...] = jnp.zeros_like(acc_ref)

    h = h_ref[...]
    gate = jnp.dot(h, wg_ref[...], preferred_element_type=F32)
    up = jnp.dot(h, wu_ref[...], preferred_element_type=F32)
    act = (gate * jax.nn.sigmoid(gate) * up).astype(wd_ref.dtype)
    acc_ref[...] += jnp.dot(act, wd_ref[...], preferred_element_type=F32)

    @pl.when(f == nf - 1)
    def _():
        _store_group_tile(o_refs, n_first, _group_tile(x_refs, n_first) + 0.5 * acc_ref[...])


def _ffn(xs, gain, wg, wu, wd, layer, tm, split_out=None):
    xs = tuple(xs) if isinstance(xs, (tuple, list)) else (xs,)
    m = sum(x.shape[0] for x in xs)
    d = xs[0].shape[1]
    n_first = (xs[0].shape[0] if len(xs) == 2 else split_out[0] if split_out else m) // tm
    ff = wg.shape[-1]
    tf = _pick(ff, FF_TILES)
    nf = ff // tf
    wbytes = jnp.dtype(wg.dtype).itemsize
    est = (2 * ((len(xs) + (2 if split_out else 1)) * tm * d * 4 + 3 * d * tf * wbytes)
           + tm * d * (4 + wbytes) + 4 * tm * tf * 4)
    x_specs = _group_specs(tm, d, n_first) if len(xs) == 2 else [pl.BlockSpec((tm, d), lambda i, f: (i, 0))]
    if split_out:
        out_specs = _group_specs(tm, d, n_first)
        out_shape = [jax.ShapeDtypeStruct((rows, d), F32) for rows in split_out]
    else:
        out_specs = pl.BlockSpec((tm, d), lambda i, f: (i, 0))
        out_shape = jax.ShapeDtypeStruct((m, d), F32)
    return pl.pallas_call(
        functools.partial(_ffn_kernel, nf=nf, n_in=len(xs), n_out=2 if split_out else 1, n_first=n_first),
        grid=(m // tm, nf),
        in_specs=x_specs + [
            pl.BlockSpec((None, 1, d), lambda i, f: (layer, 0, 0)),
            pl.BlockSpec((None, d, tf), lambda i, f: (layer, 0, f)),
            pl.BlockSpec((None, d, tf), lambda i, f: (layer, 0, f)),
            pl.BlockSpec((None, tf, d), lambda i, f: (layer, f, 0)),
        ],
        out_specs=out_specs,
        out_shape=out_shape,
        scratch_shapes=[pltpu.VMEM((tm, d), wg.dtype), pltpu.VMEM((tm, d), F32)],
        compiler_params=_params(("arbitrary", "arbitrary"), est),
        name="swiglu_half_step",
    )(*xs, gain, wg, wu, wd)


SEG_RQ, SEG_RK, SEG_RV, SEG_RG, SEG_DQ, SEG_DK, SEG_DV, SEG_GR, SEG_GD, N_SEG = 0, 1, 2, 3, 4, 5, 6, 7, 9, 11


def _rotate_half_pairs(a, cos2, sin2):
    return a * cos2 + pltpu.roll(a, HALF, 1) * sin2


def _component_rms_norm(a, gain2):
    lo = lax.broadcasted_iota(jnp.int32, a.shape, 1) < HALF
    sq = a * a
    s_all = jnp.sum(sq, axis=-1, keepdims=True)
    s_lo = jnp.sum(jnp.where(lo, sq, 0.0), axis=-1, keepdims=True)
    ms = jnp.where(lo, s_lo, s_all - s_lo) * (1.0 / HALF)
    return a * lax.rsqrt(ms + EPS) * gain2


def _proj_kernel(x_ref, g_ref, w_ref, cos_ref, sin_ref, qn_ref, kn_ref, p_ref, kp_ref, ks_ref, vp_ref, vs_ref,
                 h_ref, *, nheads, n_first):
    j = pl.program_id(1)

    @pl.when(j == 0)
    def _():
        x = x_ref[...]
        h_ref[...] = (x * _rms_scale(x) * g_ref[...]).astype(h_ref.dtype)

    acc = jnp.dot(h_ref[...], w_ref[...], preferred_element_type=F32)
    heads = [slice(h * HEAD, (h + 1) * HEAD) for h in range(nheads)]

    @pl.when(j == SEG_RQ)
    def _():
        for sl in heads:
            p_ref[:, sl] = _rotate_half_pairs(acc[:, sl], cos_ref[...], sin_ref[...]).astype(p_ref.dtype)

    @pl.when(j == SEG_RK)
    def _():
        for sl in heads:
            r = _rotate_half_pairs(acc[:, sl], cos_ref[...], sin_ref[...]) * (HEAD ** -0.5)
            p_ref[:, sl] = r.astype(p_ref.dtype)

    @pl.when(j == SEG_RV)
    def _():
        p_ref[...] = acc.astype(p_ref.dtype)

    @pl.when(j == SEG_RG)
    def _():
        p_ref[...] = (acc * jax.nn.sigmoid(acc)).astype(p_ref.dtype)

    @pl.when(j == SEG_DQ)
    def _():
        for sl in heads:
            r = _component_rms_norm(acc[:, sl], qn_ref[...]) * (HALF ** -0.5)
            p_ref[:, sl] = r.astype(p_ref.dtype)

    @pl.when(j == SEG_DK)
    def _():
        normed = jnp.concatenate([_component_rms_norm(acc[:, sl], kn_ref[...]) for sl in heads], axis=1)
        _store_group_tile((kp_ref, ks_ref), n_first, normed)
        p_ref[...] = normed.astype(p_ref.dtype)

    @pl.when(j == SEG_DV)
    def _():
        _store_group_tile((vp_ref, vs_ref), n_first, acc)
        p_ref[...] = acc.astype(p_ref.dtype)

    @pl.when(j >= SEG_GR)
    def _():
        p_ref[...] = jax.nn.sigmoid(acc).astype(p_ref.dtype)


def _project(x, gain, w_in, cos2, sin2, qn2, kn2, layer, tm, group_rows):
    m, d = x.shape
    u = d // 2
    assert w_in.shape[-1] == N_SEG * u and u % HEAD == 0
    n_first = group_rows[0] // tm
    wbytes = jnp.dtype(w_in.dtype).itemsize
    est = (2 * (tm * d * 4 + d * u * wbytes + tm * u * wbytes + 4 * tm * u * 4 + 2 * tm * HEAD * 4)
           + tm * d * wbytes + 3 * tm * u * 4)
    kv_specs = _group_specs(tm, u, n_first)
    kv_shapes = [jax.ShapeDtypeStruct((rows, u), F32) for rows in group_rows]
    return pl.pallas_call(
        functools.partial(_proj_kernel, nheads=u // HEAD, n_first=n_first),
        grid=(m // tm, N_SEG),
        in_specs=[
            pl.BlockSpec((tm, d), lambda i, j: (i, 0)),
            pl.BlockSpec((None, 1, d), lambda i, j: (layer, 0, 0)),
            pl.BlockSpec((None, d, u), lambda i, j: (layer, 0, j)),
            pl.BlockSpec((tm, HEAD), lambda i, j: (i, 0)),
            pl.BlockSpec((tm, HEAD), lambda i, j: (i, 0)),
            pl.BlockSpec((None, 1, HEAD), lambda i, j: (layer, 0, 0)),
            pl.BlockSpec((None, 1, HEAD), lambda i, j: (layer, 0, 0)),
        ],
        out_specs=[pl.BlockSpec((tm, u), lambda i, j: (i, j))] + kv_specs + kv_specs,
        out_shape=[jax.ShapeDtypeStruct((m, N_SEG * u), w_in.dtype)] + kv_shapes + kv_shapes,
        scratch_shapes=[pltpu.VMEM((tm, d), w_in.dtype)],
        compiler_params=_params(("arbitrary", "arbitrary"), est),
        name="input_projection",
    )(x, gain, w_in, cos2, sin2, qn2, kn2)


def _retention_kernel(q_ref, k_ref, v_ref, g_ref, s0_ref, d_ref, wq_ref, we_ref, dec_ref, o_ref, sout_ref, st_ref,
                      *, nheads, nblk):
    t = pl.program_id(1)

    @pl.when(t == 0)
    def _():
        st_ref[...] = s0_ref[...]

    for h in range(nheads):
        sl = slice(h * HEAD, (h + 1) * HEAD)
        q = q_ref[:, sl]
        k = k_ref[:, sl]
        v = v_ref[:, sl]
        s = lax.dot_general(q, k, (((1,), (1,)), ((), ())), preferred_element_type=F32) * d_ref[h]
        o = jnp.dot(s.astype(v.dtype), v, preferred_element_type=F32)
        state = st_ref[h]
        o = o + wq_ref[:, sl] * jnp.dot(q, state.astype(q.dtype), preferred_element_type=F32)
        kw = (k.astype(F32) * we_ref[:, sl]).astype(k.dtype)
        kv = lax.dot_general(kw, v, (((0,), (0,)), ((), ())), preferred_element_type=F32)
        st_ref[h] = state * dec_ref[h:h + 1, :] + kv
        r = o * _rms_scale(o)
        o_ref[:, sl] = (r * g_ref[:, sl].astype(F32)).astype(o_ref.dtype)

    @pl.when(t == nblk - 1)
    def _():
        sout_ref[...] = st_ref[...]


def _retention_tables(t, nheads):
    log_g = jnp.log(1.0 - 2.0 ** (-5.0 - jnp.arange(nheads, dtype=F32)))
    idx = jnp.arange(t, dtype=F32)
    dist = jnp.abs(idx[:, None] - idx[None, :])
    ci = np.arange(t) // CHUNK
    visible = jnp.asarray(ci[None, :] <= ci[:, None])
    dmat = jnp.where(visible[None], jnp.exp(log_g[:, None, None] * dist[None]), 0.0)
    wq = jnp.exp(log_g[None, :] * (idx + 1.0)[:, None])
    we = jnp.exp(log_g[None, :] * (t - 1.0 - idx)[:, None])
    dec = jnp.exp(log_g * t)
    expand = lambda a: jnp.repeat(a, HEAD, axis=1)
    return dmat, expand(wq), expand(we), jnp.broadcast_to(dec[:, None], (nheads, HEAD))


def _retention(p_all, s0, t, nbatch, nblk, row_block0):
    nheads = s0.shape[1]
    u = nheads * HEAD
    dmat, wq, we, dec = _retention_tables(t, nheads)
    rows = lambda c: pl.BlockSpec((t, u), lambda b, i: (row_block0 + b * nblk + i, c))
    whole = lambda a: pl.BlockSpec(a.shape, lambda b, i: (0,) * a.ndim)
    state_spec = pl.BlockSpec((None, nheads, HEAD, HEAD), lambda b, i: (b, 0, 0, 0))
    pbytes = jnp.dtype(p_all.dtype).itemsize
    est = (2 * (5 * t * u * pbytes + 2 * nheads * HEAD * HEAD * 4 + nheads * t * t * 4 + 2 * t * u * 4)
           + nheads * HEAD * HEAD * 4 + 6 * t * max(t, HEAD) * 4)
    return pl.pallas_call(
        functools.partial(_retention_kernel, nheads=nheads, nblk=nblk),
        grid=(nbatch, nblk),
        in_specs=[rows(SEG_RQ), rows(SEG_RK), rows(SEG_RV), rows(SEG_RG), state_spec,
                  whole(dmat), whole(wq), whole(we), whole(dec)],
        out_specs=[
            pl.BlockSpec((t, u), lambda b, i: (b * nblk + i, 0)),
            state_spec,
        ],
        out_shape=[
            jax.ShapeDtypeStruct((nbatch * nblk * t, u), p_all.dtype),
            jax.ShapeDtypeStruct((nbatch, nheads, HEAD, HEAD), F32),
        ],
        scratch_shapes=[pltpu.VMEM((nheads, HEAD, HEAD), F32)],
        compiler_params=_params(("parallel", "arbitrary"), est),
        name="retention",
    )(p_all, p_all, p_all, p_all, s0, dmat, wq, we, dec)


def _bucket_thresholds():
    nb = N_BUCKETS // 2
    me = nb // 2
    out = []
    for k in range(1, nb - me):
        n = me
        while n ** (nb - me) * me ** k < me ** (nb - me) * MAX_DISTANCE ** k:
            n += 1
        out.append(n)
    return out


def _t5_bucket_np(rel):
    nb = N_BUCKETS // 2
    me = nb // 2
    n = np.abs(rel)
    large = np.full(rel.shape, me, np.int64)
    for thr in _bucket_thresholds():
        large += (n >= thr)
    large = np.minimum(large, nb - 1)
    return (np.where(rel > 0, nb, 0) + np.where(n < me, n, large)).astype(np.int32)


def _bias_kernel(rb_ref, idx_ref, mask_ref, o_ref):
    h = pl.program_id(0)
    idx = idx_ref[...]
    out = mask_ref[...]
    for b in range(N_BUCKETS):
        out = out + jnp.where(idx == b, rb_ref[b, h], 0.0)
    o_ref[...] = out


def _bias_table(rel_bias, qpos, kpos):
    nheads = rel_bias.shape[1]
    rel = kpos[None, :] - qpos[:, None]
    idx = jnp.asarray(_t5_bucket_np(rel))
    mask = jnp.asarray(np.where((kpos[None, :] // CHUNK) <= (qpos[:, None] // CHUNK), 0.0, MASK_VALUE)
                       .astype(np.float32))
    nq, nk = rel.shape
    return pl.pallas_call(
        _bias_kernel,
        grid=(nheads,),
        in_specs=[
            pl.BlockSpec(memory_space=pltpu.SMEM),
            pl.BlockSpec((nq, nk), lambda h: (0, 0)),
            pl.BlockSpec((nq, nk), lambda h: (0, 0)),
        ],
        out_specs=pl.BlockSpec((None, nq, nk), lambda h: (h, 0, 0)),
        out_shape=jax.ShapeDtypeStruct((nheads, nq, nk), F32),
        compiler_params=_params(("arbitrary",), 6 * nq * nk * 4),
        name="relative_bias_table",
    )(rel_bias, idx, mask)


def _stack_components(q):
    lo = lax.broadcasted_iota(jnp.int32, q.shape, 1) < HALF
    zero = jnp.zeros_like(q)
    return jnp.concatenate([jnp.where(lo, q, zero), jnp.where(lo, zero, q)], axis=0)


def _softmax_step(qs, k, v, bias, stats, g):
    m_ref, l_ref, acc_ref = stats
    s = lax.dot_general(qs, k, (((1,), (1,)), ((), ())), preferred_element_type=F32)
    m_prev = m_ref[g]
    if bias.ndim == 2:
        t = bias.shape[0]
        s = (s.reshape(2, t, s.shape[-1]) + bias[None]).reshape(s.shape)
        m_new = jnp.maximum(m_prev, jnp.max(s, axis=-1, keepdims=True))
        shift = m_new
    else:
        m_new = jnp.maximum(m_prev, jnp.max(s, axis=-1, keepdims=True) + bias)
        shift = m_new - bias
    alpha = jnp.exp(m_prev - m_new)
    p = jnp.exp(s - shift)
    l_ref[g] = alpha * l_ref[g] + jnp.sum(p, axis=-1, keepdims=True)
    acc_ref[g] = alpha * acc_ref[g] + jnp.dot(p.astype(v.dtype), v, preferred_element_type=F32)
    m_ref[g] = m_new


def _softmax_init(stats):
    m_ref, l_ref, acc_ref = stats
    m_ref[...] = jnp.full_like(m_ref, MASK_VALUE)
    l_ref[...] = jnp.zeros_like(l_ref)
    acc_ref[...] = jnp.zeros_like(acc_ref)


def _softmax_scratch(groups, t):
    return [pltpu.VMEM((groups, 2 * t, 1), F32), pltpu.VMEM((groups, 2 * t, 1), F32),
            pltpu.VMEM((groups, 2 * t, HEAD), F32)]


def _lambda_value(lam_ref, lam_init):
    a = lam_ref[...]
    e1 = jnp.exp(jnp.sum(a[0:1] * a[1:2], axis=-1, keepdims=True))
    e2 = jnp.exp(jnp.sum(a[2:3] * a[3:4], axis=-1, keepdims=True))
    return e1 - e2 + lam_init


def _diff_finish(t, lam, lam_init, subln, stats, g):
    _, l_ref, acc_ref = stats
    acc = acc_ref[g]
    l = l_ref[g]
    o = acc[:t] / l[:t] - lam * (acc[t:] / l[t:])
    return o * _rms_scale(o) * subln * (1.0 - lam_init)


def _biased_scores(qs, k, bias):
    s = lax.dot_general(qs, k, (((1,), (1,)), ((), ())), preferred_element_type=F32)
    if bias.ndim == 2:
        t = bias.shape[0]
        return (s.reshape(2, t, s.shape[-1]) + bias[None]).reshape(s.shape), 0.0
    return s, bias


def _attn_prompt_kernel(rb_ref, q_ref, k_ref, v_ref, bias_ref, lam_ref, sub_ref, o_ref, *stats,
                        t, groups, lam_init, far_bucket):
    hp = pl.program_id(1)
    qi = pl.program_id(2)
    m_ref, l_ref, acc_ref = stats
    lanes = [slice(g * HEAD, (g + 1) * HEAD) for g in range(groups)]
    qs = [_stack_components(q_ref[:, sl]) for sl in lanes]
    far_bias = [rb_ref[far_bucket, hp * groups + g] for g in range(groups)]
    _softmax_init(stats)

    def sweep(step):
        def run(j, bias_of):
            rows = pl.ds(pl.multiple_of(j * t, t), t)
            for g in range(groups):
                step(g, rows, bias_of(g))

        def far_step(j, carry):
            run(j, lambda g: far_bias[g])
            return carry

        lax.fori_loop(0, jnp.maximum(qi - 1, 0), far_step, 0)

        @pl.when(qi > 0)
        def _():
            run(qi - 1, lambda g: bias_ref[g, :, :t])

        run(qi, lambda g: bias_ref[g, :, t:])

    def max_step(g, rows, bias):
        s, c = _biased_scores(qs[g], k_ref[rows, lanes[g]], bias)
        m_ref[g] = jnp.maximum(m_ref[g], jnp.max(s, axis=-1, keepdims=True) + c)

    def acc_step(g, rows, bias):
        s, c = _biased_scores(qs[g], k_ref[rows, lanes[g]], bias)
        p = jnp.exp(s - (m_ref[g] - c))
        l_ref[g] += jnp.sum(p, axis=-1, keepdims=True)
        v = v_ref[rows, lanes[g]]
        acc_ref[g] += jnp.dot(p.astype(v.dtype), v, preferred_element_type=F32)

    sweep(max_step)
    sweep(acc_step)

    lam = _lambda_value(lam_ref, lam_init)
    for g, sl in enumerate(lanes):
        o_ref[:, sl] = _diff_finish(t, lam, lam_init, sub_ref[...], stats, g).astype(o_ref.dtype)


def _attn_prompt(p_all, rel_bias, lam_params, subln2, layer, nbatch, seq, nheads, lam_init):
    u = nheads * HEAD
    t = _pick(seq, ATTN_BLOCKS)
    groups = _pick(nheads, HEADS_PER_STEP)
    assert t % CHUNK == 0 and t + 1 >= _bucket_thresholds()[-1]
    nq = seq // t
    hb = nheads // groups
    w = groups * HEAD
    r = np.arange(t)
    bias = _bias_table(rel_bias, r + t, np.arange(2 * t))
    pbytes = jnp.dtype(p_all.dtype).itemsize
    est = (2 * (2 * t * w * pbytes + 2 * seq * w * pbytes + groups * 2 * t * t * 4)
           + groups * (2 * t * (HEAD + 2 * 128) * 4 + 6 * 2 * t * t * 4))
    return pl.pallas_call(
        functools.partial(_attn_prompt_kernel, t=t, groups=groups, lam_init=lam_init,
                          far_bucket=N_BUCKETS // 2 - 1),
        grid=(nbatch, hb, nq),
        in_specs=[
            pl.BlockSpec(memory_space=pltpu.SMEM),
            pl.BlockSpec((t, w), lambda b, h, i: (b * nq + i, SEG_DQ * hb + h)),
            pl.BlockSpec((seq, w), lambda b, h, i: (b, SEG_DK * hb + h)),
            pl.BlockSpec((seq, w), lambda b, h, i: (b, SEG_DV * hb + h)),
            pl.BlockSpec((groups, t, 2 * t), lambda b, h, i: (h, 0, 0)),
            pl.BlockSpec((None, 4, HALF), lambda b, h, i: (layer, 0, 0)),
            pl.BlockSpec((None, 1, HEAD), lambda b, h, i: (layer, 0, 0)),
        ],
        out_specs=pl.BlockSpec((t, w), lambda b, h, i: (b * nq + i, h)),
        out_shape=jax.ShapeDtypeStruct((nbatch * seq, u), p_all.dtype),
        scratch_shapes=_softmax_scratch(groups, t),
        compiler_params=_params(("parallel", "parallel", "arbitrary"), est),
        name="diff_attention_prompt",
    )(rel_bias, p_all, p_all, p_all, bias, lam_params, subln2)


def _attn_sample_kernel(q_ref, kn_ref, vn_ref, kc_ref, vc_ref, bc_ref, bn_ref, lam_ref, sub_ref,
                        o_ref, *stats, t, tk, ncache, nheads, lam_init):
    j = pl.program_id(1)
    lanes = [slice(h * HEAD, (h + 1) * HEAD) for h in range(nheads)]

    @pl.when(j == 0)
    def _():
        _softmax_init(stats)

    for h, sl in enumerate(lanes):
        qs = _stack_components(q_ref[:, sl])
        k = kc_ref[pl.ds(h, tk, stride=nheads), :].astype(qs.dtype)
        v = vc_ref[pl.ds(h, tk, stride=nheads), :].astype(qs.dtype)
        _softmax_step(qs, k, v, bc_ref[h], stats, h)

    @pl.when(j == ncache - 1)
    def _():
        lam = _lambda_value(lam_ref, lam_init)
        for h, sl in enumerate(lanes):
            qs = _stack_components(q_ref[:, sl])
            _softmax_step(qs, kn_ref[:, sl], vn_ref[:, sl], bn_ref[h], stats, h)
            o_ref[:, sl] = _diff_finish(t, lam, lam_init, sub_ref[...], stats, h).astype(o_ref.dtype)


def _attn_sample(p_all, cache_k, cache_v, rel_bias, lam_params, subln2, layer, nbatch, t, past,
                 row0, nheads, lam_init):
    u = nheads * HEAD
    assert row0 % t == 0
    rb0 = row0 // t
    tk = _pick(past, CACHE_TILES)
    qpos = past + np.arange(t)
    ncache = past // tk
    bias_c = _bias_table(rel_bias, qpos, np.arange(past))
    bias_c = bias_c.reshape(nheads, t, ncache, tk).transpose(2, 0, 1, 3)
    bias_n = _bias_table(rel_bias, qpos, qpos)
    pbytes = jnp.dtype(p_all.dtype).itemsize
    est = (2 * (4 * t * u * pbytes + 2 * tk * nheads * HEAD * 4 + nheads * t * (tk + t) * 4)
           + nheads * (2 * t * (HEAD + 2 * 128) * 4 + 6 * 2 * t * tk * 4))
    rows = lambda seg: pl.BlockSpec((t, u), lambda b, j: (rb0 + b, seg))
    cache = pl.BlockSpec((None, tk * nheads, HEAD), lambda b, j: (layer, b * ncache + j, 0))
    return pl.pallas_call(
        functools.partial(_attn_sample_kernel, t=t, tk=tk, ncache=ncache, nheads=nheads, lam_init=lam_init),
        grid=(nbatch, ncache),
        in_specs=[
            rows(SEG_DQ), rows(SEG_DK), rows(SEG_DV), cache, cache,
            pl.BlockSpec((None, nheads, t, tk), lambda b, j: (j, 0, 0, 0)),
            pl.BlockSpec((nheads, t, t), lambda b, j: (0, 0, 0)),
            pl.BlockSpec((None, 4, HALF), lambda b, j: (layer, 0, 0)),
            pl.BlockSpec((None, 1, HEAD), lambda b, j: (layer, 0, 0)),
        ],
        out_specs=pl.BlockSpec((t, u), lambda b, j: (b, 0)),
        out_shape=jax.ShapeDtypeStruct((nbatch * t, u), p_all.dtype),
        scratch_shapes=_softmax_scratch(nheads, t),
        compiler_params=_params(("parallel", "arbitrary"), est),
        name="diff_attention_sample",
    )(p_all, p_all, p_all, cache_k, cache_v, bias_c, bias_n, lam_params, subln2)


def _merge_kernel(ap_ref, as_ref, bp_ref, bs_ref, ga0_ref, ga1_ref, gb0_ref, gb1_ref, wa_ref, wb_ref, o_ref,
                  *, u, n_first):
    a = _group_tile((ap_ref, as_ref), n_first)
    b = _group_tile((bp_ref, bs_ref), n_first)
    ya = jnp.dot(a, wa_ref[...], preferred_element_type=F32)
    yb = jnp.dot(b, wb_ref[...], preferred_element_type=F32)
    for c, (ga, gb) in enumerate(((ga0_ref, gb0_ref), (ga1_ref, gb1_ref))):
        sl = slice(c * u, (c + 1) * u)
        o_ref[:, sl] = (ga[...].astype(F32) * ya[:, sl] + gb[...].astype(F32) * yb[:, sl]).astype(o_ref.dtype)


def _merge(ret_outs, dif_outs, p_all, w_ret_up, w_dif_up, layer, tm):
    m = p_all.shape[0]
    u = ret_outs[0].shape[1]
    d = 2 * u
    n_first = ret_outs[0].shape[0] // tm
    gate = lambda c: pl.BlockSpec((tm, u), lambda i: (i, c))
    rows = _group_specs(tm, u, n_first)
    wspec = pl.BlockSpec((None, u, d), lambda i: (layer, 0, 0))
    pbytes = jnp.dtype(p_all.dtype).itemsize
    est = 2 * (8 * tm * u * pbytes + 2 * u * d * pbytes + tm * d * pbytes) + 3 * tm * d * 4
    return pl.pallas_call(
        functools.partial(_merge_kernel, u=u, n_first=n_first),
        grid=(m // tm,),
        in_specs=rows + rows + [gate(SEG_GR), gate(SEG_GR + 1), gate(SEG_GD), gate(SEG_GD + 1), wspec, wspec],
        out_specs=pl.BlockSpec((tm, d), lambda i: (i, 0)),
        out_shape=jax.ShapeDtypeStruct((m, d), p_all.dtype),
        compiler_params=_params(("parallel",), est),
        name="gated_merge",
    )(*ret_outs, *dif_outs, p_all, p_all, p_all, p_all, w_ret_up, w_dif_up)


def _out_proj_kernel(x_ref, a_ref, w_ref, o_ref):
    o_ref[...] = x_ref[...] + jnp.dot(a_ref[...], w_ref[...], preferred_element_type=F32)


def _out_proj(x, merged, w_out, layer, tm):
    m, d = x.shape
    wbytes = jnp.dtype(w_out.dtype).itemsize
    est = 2 * (2 * tm * d * 4 + tm * d * wbytes + d * d * wbytes) + tm * d * 4
    return pl.pallas_call(
        _out_proj_kernel,
        grid=(m // tm,),
        in_specs=[
            pl.BlockSpec((tm, d), lambda i: (i, 0)),
            pl.BlockSpec((tm, d), lambda i: (i, 0)),
            pl.BlockSpec((None, d, d), lambda i: (layer, 0, 0)),
        ],
        out_specs=pl.BlockSpec((tm, d), lambda i: (i, 0)),
        out_shape=jax.ShapeDtypeStruct((m, d), F32),
        compiler_params=_params(("parallel",), est),
        name="output_projection",
    )(x, merged, w_out)


def _rotary_tables(pos):
    inv = ROPE_BASE ** (-jnp.arange(HALF, dtype=F32) / HALF)
    ang = pos.astype(F32)[:, None] * inv[None, :]
    cos, sin = jnp.cos(ang), jnp.sin(ang)
    return jnp.concatenate([cos, cos], axis=-1), jnp.concatenate([-sin, sin], axis=-1)


def kernel(x_prompt, x_sample, cache_diff_k, cache_diff_v, state_ret, ffn1_norm, ffn1_gate, ffn1_up, ffn1_down, mix_norm, w_in, q_norm, k_norm, lambda_q1, lambda_k1, lambda_q2, lambda_k2, subln, w_ret_up, w_dif_up, w_out, ffn2_norm, ffn2_gate, ffn2_up, ffn2_down, rel_bias):
    nb, seq, d = x_prompt.shape
    db, dseq, _ = x_sample.shape
    depth, _, past, nh_d, _ = cache_diff_k.shape
    nh_r = state_ret.shape[2]
    u = d // 2
    assert nh_r * HEAD == u and nh_d * HEAD == u and dseq == CHUNK and seq % CHUNK == 0
    mp, ms = nb * seq, db * dseq

    cast = lambda w: w.astype(MXU_DTYPE)
    wg1, wu1, wd1 = cast(ffn1_gate), cast(ffn1_up), cast(ffn1_down)
    wg2, wu2, wd2 = cast(ffn2_gate), cast(ffn2_up), cast(ffn2_down)
    w_in_c, w_ret_c, w_dif_c, w_out_c = cast(w_in), cast(w_ret_up), cast(w_dif_up), cast(w_out)
    row3 = lambda g: g.reshape(depth, 1, g.shape[-1])
    n1, nmix, n2 = row3(ffn1_norm), row3(mix_norm), row3(ffn2_norm)
    qn2 = row3(jnp.concatenate([q_norm, q_norm], axis=-1))
    kn2 = row3(jnp.concatenate([k_norm, k_norm], axis=-1))
    subln2 = row3(subln)
    lam_params = jnp.stack([lambda_q1, lambda_k1, lambda_q2, lambda_k2], axis=1)
    cache_k = cache_diff_k.reshape(depth, db * past * nh_d, HEAD)
    cache_v = cache_diff_v.reshape(depth, db * past * nh_d, HEAD)

    pos = jnp.concatenate([jnp.tile(jnp.arange(seq, dtype=jnp.int32), nb),
                           past + jnp.tile(jnp.arange(dseq, dtype=jnp.int32), db)])
    cos2, sin2 = _rotary_tables(pos)

    tm = _pick(math.gcd(mp, ms), ROW_TILES)
    groups = (mp, ms)
    zero_state = jnp.zeros((nb, nh_r, HEAD, HEAD), F32)
    t_ret = _pick(seq, RET_BLOCKS)

    x = (x_prompt.reshape(mp, d), x_sample.reshape(ms, d))
    kps, kss, vps, vss, states_p, states_s = [], [], [], [], [], []
    for l in range(depth):
        lam_init = 0.8 - 0.6 * math.exp(-0.3 * l)
        x = _ffn(x, n1, wg1, wu1, wd1, l, tm)
        p_all, k_p, k_s, v_p, v_s = _project(x, nmix, w_in_c, cos2, sin2, qn2, kn2, l, tm, groups)

        ret_p, st_p = _retention(p_all, zero_state, t_ret, nb, seq // t_ret, 0)
        ret_s, st_s = _retention(p_all, state_ret[l].astype(F32), dseq, db, 1, mp // dseq)

        dif_p = _attn_prompt(p_all, rel_bias, lam_params, subln2, l, nb, seq, nh_d, lam_init)
        dif_s = _attn_sample(p_all, cache_k, cache_v, rel_bias, lam_params, subln2, l, db, dseq, past,
                             mp, nh_d, lam_init)

        merged = _merge((ret_p, ret_s), (dif_p, dif_s), p_all, w_ret_c, w_dif_c, l, tm)
        x = _out_proj(x, merged, w_out_c, l, tm)
        x = _ffn(x, n2, wg2, wu2, wd2, l, tm, split_out=groups if l == depth - 1 else None)

        for acc, val in ((kps, k_p), (kss, k_s), (vps, v_p), (vss, v_s), (states_p, st_p), (states_s, st_s)):
            acc.append(val)

    y_p, y_s = x
    kv_p = lambda parts: jnp.stack(parts).reshape(depth, nb, seq, nh_d, HEAD)
    kv_s = lambda parts: jnp.stack(parts).reshape(depth, db, dseq, nh_d, HEAD)
    return (y_p.reshape(nb, seq, d), y_s.reshape(db, dseq, d),
            kv_p(kps).astype(cache_diff_k.dtype), kv_p(vps).astype(cache_diff_v.dtype),
            jnp.stack(states_p).astype(state_ret.dtype),
            kv_s(kss).astype(cache_diff_k.dtype), kv_s(vss).astype(cache_diff_v.dtype),
            jnp.stack(states_s).astype(state_ret.dtype))
```

```python
import functools
import math

import numpy as np
import jax
import jax.numpy as jnp
from jax import lax
from jax.experimental import pallas as pl
from jax.experimental.pallas import tpu as pltpu

F32 = jnp.float32
MXU_DTYPE = jnp.bfloat16

CHUNK = 64
HEAD = 128
HALF = HEAD // 2
ROPE_BASE = 10000.0
N_BUCKETS = 32
MAX_DISTANCE = 128
EPS = 1e-6
MASK_VALUE = -1e30

VMEM_LIMIT_CAP = 60 * 1024 * 1024
MIB = 1024 * 1024

ROW_TILES = (512, 256, 128, 64)
FF_TILES = (512, 256, 128)
ATTN_BLOCKS = (512, 256, 128)
HEADS_PER_STEP = (2, 1)
RET_BLOCKS = (256, 128, 64)
CACHE_TILES = (512, 256, 128, 64)


def _pick(n, prefs):
    for p in prefs:
        if n % p == 0:
            return p
    raise ValueError(f"no tile in {prefs} divides {n}")


def _params(semantics, est_bytes):
    limit = int(min(max(est_bytes + 8 * MIB, 32 * MIB), VMEM_LIMIT_CAP))
    return pltpu.CompilerParams(dimension_semantics=semantics, vmem_limit_bytes=limit)


def _rms_scale(x):
    return lax.rsqrt(jnp.mean(x * x, axis=-1, keepdims=True) + EPS)


def _group_specs(tm, width, n_first, row_axis=0):
    first = pl.BlockSpec((tm, width), lambda *g: (jnp.minimum(g[row_axis], n_first - 1), 0))
    second = pl.BlockSpec((tm, width), lambda *g: (jnp.maximum(g[row_axis] - n_first, 0), 0))
    return [first, second]


def _group_tile(refs, n_first, row_axis=0):
    if len(refs) == 1:
        return refs[0][...]
    return jnp.where(pl.program_id(row_axis) < n_first, refs[0][...], refs[1][...])


def _store_group_tile(refs, n_first, value, row_axis=0):
    if len(refs) == 1:
        refs[0][...] = value
        return
    i = pl.program_id(row_axis)

    @pl.when(i < n_first)
    def _():
        refs[0][...] = value

    @pl.when(i >= n_first)
    def _():
        refs[1][...] = value


def _ffn_kernel(*refs, nf, n_in, n_out, n_first, norm_out):
    x_refs = refs[:n_in]
    g_ref, wg_ref, wu_ref, wd_ref = refs[n_in:n_in + 4]
    pos = n_in + 4
    g2_ref = refs[pos] if norm_out else None
    pos += int(norm_out)
    o_refs = refs[pos:pos + n_out]
    pos += n_out
    hn_ref = refs[pos] if norm_out else None
    pos += int(norm_out)
    h_ref, acc_ref = refs[pos:]
    f = pl.program_id(1)

    @pl.when(f == 0)
    def _():
        x = _group_tile(x_refs, n_first)
        h_ref[...] = (x * _rms_scale(x) * g_ref[...]).astype(h_ref.dtype)
        acc_ref[...] = jnp.zeros_like(acc_ref)

    h = h_ref[...]
    gate = jnp.dot(h, wg_ref[...], preferred_element_type=F32)
    up = jnp.dot(h, wu_ref[...], preferred_element_type=F32)
    act = (gate * jax.nn.sigmoid(gate) * up).astype(wd_ref.dtype)
    acc_ref[...] += jnp.dot(act, wd_ref[...], preferred_element_type=F32)

    @pl.when(f == nf - 1)
    def _():
        y = _group_tile(x_refs, n_first) + 0.5 * acc_ref[...]
        _store_group_tile(o_refs, n_first, y)
        if norm_out:
            hn_ref[...] = (y * _rms_scale(y) * g2_ref[...]).astype(hn_ref.dtype)


def _ffn(xs, gain, wg, wu, wd, layer, tm, split_out=None, next_gain=None):
    xs = tuple(xs) if isinstance(xs, (tuple, list)) else (xs,)
    m = sum(x.shape[0] for x in xs)
    d = xs[0].shape[1]
    n_first = (xs[0].shape[0] if len(xs) == 2 else split_out[0] if split_out else m) // tm
    ff = wg.shape[-1]
    tf = _pick(ff, FF_TILES)
    nf = ff // tf
    norm_out = next_gain is not None
    wbytes = jnp.dtype(wg.dtype).itemsize
    est = (2 * ((len(xs) + (2 if split_out else 1)) * tm * d * 4 + 3 * d * tf * wbytes + tm * d * wbytes)
           + tm * d * (4 + wbytes) + 4 * tm * tf * 4)
    rows = lambda: pl.BlockSpec((tm, d), lambda i, f: (i, 0))
    gain_spec = pl.BlockSpec((None, 1, d), lambda i, f: (layer, 0, 0))
    in_specs = (_group_specs(tm, d, n_first) if len(xs) == 2 else [rows()]) + [
        gain_spec,
        pl.BlockSpec((None, d, tf), lambda i, f: (layer, 0, f)),
        pl.BlockSpec((None, d, tf), lambda i, f: (layer, 0, f)),
        pl.BlockSpec((None, tf, d), lambda i, f: (layer, f, 0)),
    ]
    args = [*xs, gain, wg, wu, wd]
    if split_out:
        out_specs = _group_specs(tm, d, n_first)
        out_shape = [jax.ShapeDtypeStruct((r, d), F32) for r in split_out]
    else:
        out_specs = [rows()]
        out_shape = [jax.ShapeDtypeStruct((m, d), F32)]
    if norm_out:
        in_specs.append(gain_spec)
        args.append(next_gain)
        out_specs.append(rows())
        out_shape.append(jax.ShapeDtypeStruct((m, d), wg.dtype))
    outs = pl.pallas_call(
        functools.partial(_ffn_kernel, nf=nf, n_in=len(xs), n_out=2 if split_out else 1, n_first=n_first,
                          norm_out=norm_out),
        grid=(m // tm, nf),
        in_specs=in_specs,
        out_specs=out_specs,
        out_shape=out_shape,
        scratch_shapes=[pltpu.VMEM((tm, d), wg.dtype), pltpu.VMEM((tm, d), F32)],
        compiler_params=_params(("arbitrary", "arbitrary"), est),
        name="swiglu_half_step",
    )(*args)
    return outs if len(outs) > 1 else outs[0]


SEG_RQ, SEG_RK, SEG_RV, SEG_RG, SEG_DQ, SEG_DK, SEG_DV, SEG_GATES, N_SEG = 0, 1, 2, 3, 4, 5, 6, 7, 11


def _rotate_half_pairs(a, cos2, sin2):
    return a * cos2 + pltpu.roll(a, HALF, 1) * sin2


def _component_rms_norm(a, gain2):
    lo = lax.broadcasted_iota(jnp.int32, a.shape, 1) < HALF
    sq = a * a
    s_all = jnp.sum(sq, axis=-1, keepdims=True)
    s_lo = jnp.sum(jnp.where(lo, sq, 0.0), axis=-1, keepdims=True)
    ms = jnp.where(lo, s_lo, s_all - s_lo) * (1.0 / HALF)
    return a * lax.rsqrt(ms + EPS) * gain2


def _segment_kernel(h_ref, w_ref, *refs, kind, scale, nheads, n_first):
    acc = jnp.dot(h_ref[...], w_ref[...], preferred_element_type=F32)
    heads = [slice(h * HEAD, (h + 1) * HEAD) for h in range(nheads)]
    if kind == "rotary":
        cos_ref, sin_ref, o_ref = refs
        for sl in heads:
            r = _rotate_half_pairs(acc[:, sl], cos_ref[...], sin_ref[...])
            o_ref[:, sl] = (r if scale == 1.0 else r * scale).astype(o_ref.dtype)
    elif kind == "cast":
        (o_ref,) = refs
        o_ref[...] = acc.astype(o_ref.dtype)
    elif kind == "silu":
        (o_ref,) = refs
        o_ref[...] = (acc * jax.nn.sigmoid(acc)).astype(o_ref.dtype)
    elif kind == "sigmoid":
        (o_ref,) = refs
        o_ref[...] = jax.nn.sigmoid(acc).astype(o_ref.dtype)
    elif kind == "norm":
        gain_ref, o_ref = refs
        for sl in heads:
            o_ref[:, sl] = (_component_rms_norm(acc[:, sl], gain_ref[...]) * scale).astype(o_ref.dtype)
    elif kind == "norm_keep":
        gain_ref, o_ref, fp_ref, fs_ref = refs
        normed = jnp.concatenate([_component_rms_norm(acc[:, sl], gain_ref[...]) for sl in heads], axis=1)
        _store_group_tile((fp_ref, fs_ref), n_first, normed, row_axis=1)
        o_ref[...] = normed.astype(o_ref.dtype)
    elif kind == "keep":
        o_ref, fp_ref, fs_ref = refs
        _store_group_tile((fp_ref, fs_ref), n_first, acc, row_axis=1)
        o_ref[...] = acc.astype(o_ref.dtype)
    else:
        raise ValueError(kind)


def _segment(h, w_in, layer, seg0, nseg, tm, kind, *, scale=1.0, tables=(), gain=None, group_rows=None):
    m, d = h.shape
    u = d // 2
    nheads = u // HEAD
    n_first = group_rows[0] // tm if group_rows else 0
    in_specs = [pl.BlockSpec((tm, d), lambda s, i: (i, 0)),
                pl.BlockSpec((None, d, u), lambda s, i: (layer, 0, seg0 + s))]
    args = [h, w_in]
    for tab in tables:
        in_specs.append(pl.BlockSpec((tm, HEAD), lambda s, i: (i, 0)))
        args.append(tab)
    if gain is not None:
        in_specs.append(pl.BlockSpec((None, 1, HEAD), lambda s, i: (layer, 0, 0)))
        args.append(gain)
    out_specs = [pl.BlockSpec((tm, u), lambda s, i: (i, s))]
    out_shape = [jax.ShapeDtypeStruct((m, nseg * u), h.dtype)]
    if group_rows:
        out_specs += _group_specs(tm, u, n_first, row_axis=1)
        out_shape += [jax.ShapeDtypeStruct((r, u), F32) for r in group_rows]
    hb = jnp.dtype(h.dtype).itemsize
    est = 2 * (tm * d * hb + d * u * hb + tm * u * hb + 2 * tm * u * 4 + 2 * tm * HEAD * 4) + 4 * tm * u * 4
    outs = pl.pallas_call(
        functools.partial(_segment_kernel, kind=kind, scale=scale, nheads=nheads, n_first=n_first),
        grid=(nseg, m // tm),
        in_specs=in_specs,
        out_specs=out_specs,
        out_shape=out_shape,
        compiler_params=_params(("arbitrary", "arbitrary"), est),
        name="input_projection_" + kind,
    )(*args)
    return outs if len(outs) > 1 else outs[0]


def _retention_kernel(q_ref, k_ref, v_ref, g_ref, s0_ref, d_ref, wq_ref, we_ref, dec_ref, o_ref, sout_ref, st_ref,
                      *, nheads, nblk):
    t = pl.program_id(1)

    @pl.when(t == 0)
    def _():
        st_ref[...] = s0_ref[...]

    for h in range(nheads):
        sl = slice(h * HEAD, (h + 1) * HEAD)
        q = q_ref[:, sl]
        k = k_ref[:, sl]
        v = v_ref[:, sl]
        s = lax.dot_general(q, k, (((1,), (1,)), ((), ())), preferred_element_type=F32) * d_ref[h]
        o = jnp.dot(s.astype(v.dtype), v, preferred_element_type=F32)
        state = st_ref[h]
        o = o + wq_ref[:, sl] * jnp.dot(q, state.astype(q.dtype), preferred_element_type=F32)
        kw = (k.astype(F32) * we_ref[:, sl]).astype(k.dtype)
        kv = lax.dot_general(kw, v, (((0,), (0,)), ((), ())), preferred_element_type=F32)
        st_ref[h] = state * dec_ref[h:h + 1, :] + kv
        r = o * _rms_scale(o)
        o_ref[:, sl] = (r * g_ref[:, sl].astype(F32)).astype(o_ref.dtype)

    @pl.when(t == nblk - 1)
    def _():
        sout_ref[...] = st_ref[...]


def _retention_tables(t, nheads):
    log_g = jnp.log(1.0 - 2.0 ** (-5.0 - jnp.arange(nheads, dtype=F32)))
    idx = jnp.arange(t, dtype=F32)
    dist = jnp.abs(idx[:, None] - idx[None, :])
    ci = np.arange(t) // CHUNK
    visible = jnp.asarray(ci[None, :] <= ci[:, None])
    dmat = jnp.where(visible[None], jnp.exp(log_g[:, None, None] * dist[None]), 0.0)
    wq = jnp.exp(log_g[None, :] * (idx + 1.0)[:, None])
    we = jnp.exp(log_g[None, :] * (t - 1.0 - idx)[:, None])
    dec = jnp.exp(log_g * t)
    expand = lambda a: jnp.repeat(a, HEAD, axis=1)
    return dmat, expand(wq), expand(we), jnp.broadcast_to(dec[:, None], (nheads, HEAD))


def _retention(q, k, v, g, s0, t, nbatch, nblk, row_block0):
    nheads = s0.shape[1]
    u = nheads * HEAD
    dmat, wq, we, dec = _retention_tables(t, nheads)
    rows = pl.BlockSpec((t, u), lambda b, i: (row_block0 + b * nblk + i, 0))
    whole = lambda a: pl.BlockSpec(a.shape, lambda b, i: (0,) * a.ndim)
    state_spec = pl.BlockSpec((None, nheads, HEAD, HEAD), lambda b, i: (b, 0, 0, 0))
    pbytes = jnp.dtype(q.dtype).itemsize
    est = (2 * (5 * t * u * pbytes + 2 * nheads * HEAD * HEAD * 4 + nheads * t * t * 4 + 2 * t * u * 4)
           + nheads * HEAD * HEAD * 4 + 6 * t * max(t, HEAD) * 4)
    return pl.pallas_call(
        functools.partial(_retention_kernel, nheads=nheads, nblk=nblk),
        grid=(nbatch, nblk),
        in_specs=[rows, rows, rows, rows, state_spec, whole(dmat), whole(wq), whole(we), whole(dec)],
        out_specs=[
            pl.BlockSpec((t, u), lambda b, i: (b * nblk + i, 0)),
            state_spec,
        ],
        out_shape=[
            jax.ShapeDtypeStruct((nbatch * nblk * t, u), q.dtype),
            jax.ShapeDtypeStruct((nbatch, nheads, HEAD, HEAD), F32),
        ],
        scratch_shapes=[pltpu.VMEM((nheads, HEAD, HEAD), F32)],
        compiler_params=_params(("parallel", "arbitrary"), est),
        name="retention",
    )(q, k, v, g, s0, dmat, wq, we, dec)


def _bucket_thresholds():
    nb = N_BUCKETS // 2
    me = nb // 2
    out = []
    for k in range(1, nb - me):
        n = me
        while n ** (nb - me) * me ** k < me ** (nb - me) * MAX_DISTANCE ** k:
            n += 1
        out.append(n)
    return out


def _t5_bucket_np(rel):
    nb = N_BUCKETS // 2
    me = nb // 2
    n = np.abs(rel)
    large = np.full(rel.shape, me, np.int64)
    for thr in _bucket_thresholds():
        large += (n >= thr)
    large = np.minimum(large, nb - 1)
    return (np.where(rel > 0, nb, 0) + np.where(n < me, n, large)).astype(np.int32)


def _bias_kernel(rb_ref, idx_ref, mask_ref, o_ref):
    h = pl.program_id(0)
    idx = idx_ref[...]
    out = mask_ref[...]
    for b in range(N_BUCKETS):
        out = out + jnp.where(idx == b, rb_ref[b, h], 0.0)
    o_ref[...] = out


def _bias_table(rel_bias, qpos, kpos):
    nheads = rel_bias.shape[1]
    rel = kpos[None, :] - qpos[:, None]
    idx = jnp.asarray(_t5_bucket_np(rel))
    mask = jnp.asarray(np.where((kpos[None, :] // CHUNK) <= (qpos[:, None] // CHUNK), 0.0, MASK_VALUE)
                       .astype(np.float32))
    nq, nk = rel.shape
    return pl.pallas_call(
        _bias_kernel,
        grid=(nheads,),
        in_specs=[
            pl.BlockSpec(memory_space=pltpu.SMEM),
            pl.BlockSpec((nq, nk), lambda h: (0, 0)),
            pl.BlockSpec((nq, nk), lambda h: (0, 0)),
        ],
        out_specs=pl.BlockSpec((None, nq, nk), lambda h: (h, 0, 0)),
        out_shape=jax.ShapeDtypeStruct((nheads, nq, nk), F32),
        compiler_params=_params(("arbitrary",), 6 * nq * nk * 4),
        name="relative_bias_table",
    )(rel_bias, idx, mask)


def _stack_components(q):
    lo = lax.broadcasted_iota(jnp.int32, q.shape, 1) < HALF
    zero = jnp.zeros_like(q)
    return jnp.concatenate([jnp.where(lo, q, zero), jnp.where(lo, zero, q)], axis=0)


def _biased_scores(qs, k, bias):
    s = lax.dot_general(qs, k, (((1,), (1,)), ((), ())), preferred_element_type=F32)
    if bias.ndim == 2:
        t = bias.shape[0]
        return (s.reshape(2, t, s.shape[-1]) + bias[None]).reshape(s.shape), 0.0
    return s, bias


def _softmax_step(qs, k, v, bias, stats, g):
    m_ref, l_ref, acc_ref = stats
    s, c = _biased_scores(qs, k, bias)
    m_prev = m_ref[g]
    m_new = jnp.maximum(m_prev, jnp.max(s, axis=-1, keepdims=True) + c)
    alpha = jnp.exp(m_prev - m_new)
    p = jnp.exp(s - (m_new - c))
    l_ref[g] = alpha * l_ref[g] + jnp.sum(p, axis=-1, keepdims=True)
    acc_ref[g] = alpha * acc_ref[g] + jnp.dot(p.astype(v.dtype), v, preferred_element_type=F32)
    m_ref[g] = m_new


def _softmax_init(stats):
    m_ref, l_ref, acc_ref = stats
    m_ref[...] = jnp.full_like(m_ref, MASK_VALUE)
    l_ref[...] = jnp.zeros_like(l_ref)
    acc_ref[...] = jnp.zeros_like(acc_ref)


def _softmax_scratch(groups, t):
    return [pltpu.VMEM((groups, 2 * t, 1), F32), pltpu.VMEM((groups, 2 * t, 1), F32),
            pltpu.VMEM((groups, 2 * t, HEAD), F32)]


def _lambda_value(lam_ref, lam_init):
    a = lam_ref[...]
    e1 = jnp.exp(jnp.sum(a[0:1] * a[1:2], axis=-1, keepdims=True))
    e2 = jnp.exp(jnp.sum(a[2:3] * a[3:4], axis=-1, keepdims=True))
    return e1 - e2 + lam_init


def _diff_finish(t, lam, lam_init, subln, stats, g):
    _, l_ref, acc_ref = stats
    acc = acc_ref[g]
    l = l_ref[g]
    o = acc[:t] / l[:t] - lam * (acc[t:] / l[t:])
    return o * _rms_scale(o) * subln * (1.0 - lam_init)


def _attn_prompt_kernel(rb_ref, q_ref, k_ref, v_ref, bias_ref, lam_ref, sub_ref, o_ref, *stats,
                        t, groups, lam_init, far_bucket):
    hp = pl.program_id(1)
    qi = pl.program_id(2)
    m_ref, l_ref, acc_ref = stats
    lanes = [slice(g * HEAD, (g + 1) * HEAD) for g in range(groups)]
    qs = [_stack_components(q_ref[:, sl]) for sl in lanes]
    far_bias = [rb_ref[far_bucket, hp * groups + g] for g in range(groups)]
    _softmax_init(stats)

    def sweep(step):
        def run(j, bias_of):
            rows = pl.ds(pl.multiple_of(j * t, t), t)
            for g in range(groups):
                step(g, rows, bias_of(g))

        def far_step(j, carry):
            run(j, lambda g: far_bias[g])
            return carry

        lax.fori_loop(0, jnp.maximum(qi - 1, 0), far_step, 0)

        @pl.when(qi > 0)
        def _():
            run(qi - 1, lambda g: bias_ref[g, :, :t])

        run(qi, lambda g: bias_ref[g, :, t:])

    def max_step(g, rows, bias):
        s, c = _biased_scores(qs[g], k_ref[rows, lanes[g]], bias)
        m_ref[g] = jnp.maximum(m_ref[g], jnp.max(s, axis=-1, keepdims=True) + c)

    def acc_step(g, rows, bias):
        s, c = _biased_scores(qs[g], k_ref[rows, lanes[g]], bias)
        p = jnp.exp(s - (m_ref[g] - c))
        l_ref[g] += jnp.sum(p, axis=-1, keepdims=True)
        v = v_ref[rows, lanes[g]]
        acc_ref[g] += jnp.dot(p.astype(v.dtype), v, preferred_element_type=F32)

    sweep(max_step)
    sweep(acc_step)

    lam = _lambda_value(lam_ref, lam_init)
    for g, sl in enumerate(lanes):
        o_ref[:, sl] = _diff_finish(t, lam, lam_init, sub_ref[...], stats, g).astype(o_ref.dtype)


def _attn_prompt(q, k, v, rel_bias, lam_params, subln2, layer, nbatch, seq, nheads, lam_init):
    u = nheads * HEAD
    t = _pick(seq, ATTN_BLOCKS)
    groups = _pick(nheads, HEADS_PER_STEP)
    assert t % CHUNK == 0 and t + 1 >= _bucket_thresholds()[-1]
    nq = seq // t
    w = groups * HEAD
    r = np.arange(t)
    bias = _bias_table(rel_bias, r + t, np.arange(2 * t))
    pbytes = jnp.dtype(q.dtype).itemsize
    est = (2 * (2 * t * w * pbytes + 2 * seq * w * pbytes + groups * 2 * t * t * 4)
           + groups * (2 * t * (HEAD + 2 * 128) * 4 + 6 * 2 * t * t * 4))
    return pl.pallas_call(
        functools.partial(_attn_prompt_kernel, t=t, groups=groups, lam_init=lam_init,
                          far_bucket=N_BUCKETS // 2 - 1),
        grid=(nbatch, nheads // groups, nq),
        in_specs=[
            pl.BlockSpec(memory_space=pltpu.SMEM),
            pl.BlockSpec((t, w), lambda b, h, i: (b * nq + i, h)),
            pl.BlockSpec((seq, w), lambda b, h, i: (b, h)),
            pl.BlockSpec((seq, w), lambda b, h, i: (b, h)),
            pl.BlockSpec((groups, t, 2 * t), lambda b, h, i: (h, 0, 0)),
            pl.BlockSpec((None, 4, HALF), lambda b, h, i: (layer, 0, 0)),
            pl.BlockSpec((None, 1, HEAD), lambda b, h, i: (layer, 0, 0)),
        ],
        out_specs=pl.BlockSpec((t, w), lambda b, h, i: (b * nq + i, h)),
        out_shape=jax.ShapeDtypeStruct((nbatch * seq, u), q.dtype),
        scratch_shapes=_softmax_scratch(groups, t),
        compiler_params=_params(("parallel", "parallel", "arbitrary"), est),
        name="diff_attention_prompt",
    )(rel_bias, q, k, v, bias, lam_params, subln2)


def _attn_sample_kernel(q_ref, kn_ref, vn_ref, kc_ref, vc_ref, bc_ref, bn_ref, lam_ref, sub_ref,
                        o_ref, *stats, t, tk, ncache, nheads, lam_init):
    j = pl.program_id(1)
    lanes = [slice(h * HEAD, (h + 1) * HEAD) for h in range(nheads)]

    @pl.when(j == 0)
    def _():
        _softmax_init(stats)

    for h, sl in enumerate(lanes):
        qs = _stack_components(q_ref[:, sl])
        k = kc_ref[pl.ds(h, tk, stride=nheads), :].astype(qs.dtype)
        v = vc_ref[pl.ds(h, tk, stride=nheads), :].astype(qs.dtype)
        _softmax_step(qs, k, v, bc_ref[h], stats, h)

    @pl.when(j == ncache - 1)
    def _():
        lam = _lambda_value(lam_ref, lam_init)
        for h, sl in enumerate(lanes):
            qs = _stack_components(q_ref[:, sl])
            _softmax_step(qs, kn_ref[:, sl], vn_ref[:, sl], bn_ref[h], stats, h)
            o_ref[:, sl] = _diff_finish(t, lam, lam_init, sub_ref[...], stats, h).astype(o_ref.dtype)


def _attn_sample(q, k, v, cache_k, cache_v, rel_bias, lam_params, subln2, layer, nbatch, t, past,
                 row0, nheads, lam_init):
    u = nheads * HEAD
    assert row0 % t == 0
    rb0 = row0 // t
    tk = _pick(past, CACHE_TILES)
    qpos = past + np.arange(t)
    ncache = past // tk
    bias_c = _bias_table(rel_bias, qpos, np.arange(past))
    bias_c = bias_c.reshape(nheads, t, ncache, tk).transpose(2, 0, 1, 3)
    bias_n = _bias_table(rel_bias, qpos, qpos)
    pbytes = jnp.dtype(q.dtype).itemsize
    est = (2 * (4 * t * u * pbytes + 2 * tk * nheads * HEAD * 4 + nheads * t * (tk + t) * 4)
           + nheads * (2 * t * (HEAD + 2 * 128) * 4 + 6 * 2 * t * tk * 4))
    rows = pl.BlockSpec((t, u), lambda b, j: (rb0 + b, 0))
    cache = pl.BlockSpec((None, tk * nheads, HEAD), lambda b, j: (layer, b * ncache + j, 0))
    return pl.pallas_call(
        functools.partial(_attn_sample_kernel, t=t, tk=tk, ncache=ncache, nheads=nheads, lam_init=lam_init),
        grid=(nbatch, ncache),
        in_specs=[
            rows, rows, rows, cache, cache,
            pl.BlockSpec((None, nheads, t, tk), lambda b, j: (j, 0, 0, 0)),
            pl.BlockSpec((nheads, t, t), lambda b, j: (0, 0, 0)),
            pl.BlockSpec((None, 4, HALF), lambda b, j: (layer, 0, 0)),
            pl.BlockSpec((None, 1, HEAD), lambda b, j: (layer, 0, 0)),
        ],
        out_specs=pl.BlockSpec((t, u), lambda b, j: (b, 0)),
        out_shape=jax.ShapeDtypeStruct((nbatch * t, u), q.dtype),
        scratch_shapes=_softmax_scratch(nheads, t),
        compiler_params=_params(("parallel", "arbitrary"), est),
        name="diff_attention_sample",
    )(q, k, v, cache_k, cache_v, bias_c, bias_n, lam_params, subln2)


def _merge_kernel(ap_ref, as_ref, bp_ref, bs_ref, ga0_ref, ga1_ref, gb0_ref, gb1_ref, wa_ref, wb_ref, o_ref,
                  *, u, n_first):
    a = _group_tile((ap_ref, as_ref), n_first)
    b = _group_tile((bp_ref, bs_ref), n_first)
    ya = jnp.dot(a, wa_ref[...], preferred_element_type=F32)
    yb = jnp.dot(b, wb_ref[...], preferred_element_type=F32)
    for c, (ga, gb) in enumerate(((ga0_ref, gb0_ref), (ga1_ref, gb1_ref))):
        sl = slice(c * u, (c + 1) * u)
        o_ref[:, sl] = (ga[...].astype(F32) * ya[:, sl] + gb[...].astype(F32) * yb[:, sl]).astype(o_ref.dtype)


def _merge(ret_outs, dif_outs, gates, w_ret_up, w_dif_up, layer, tm):
    m = gates.shape[0]
    u = ret_outs[0].shape[1]
    d = 2 * u
    n_first = ret_outs[0].shape[0] // tm
    gate = lambda c: pl.BlockSpec((tm, u), lambda i: (i, c))
    rows = _group_specs(tm, u, n_first)
    wspec = pl.BlockSpec((None, u, d), lambda i: (layer, 0, 0))
    pbytes = jnp.dtype(gates.dtype).itemsize
    est = 2 * (8 * tm * u * pbytes + 2 * u * d * pbytes + tm * d * pbytes) + 3 * tm * d * 4
    return pl.pallas_call(
        functools.partial(_merge_kernel, u=u, n_first=n_first),
        grid=(m // tm,),
        in_specs=rows + rows + [gate(0), gate(1), gate(2), gate(3), wspec, wspec],
        out_specs=pl.BlockSpec((tm, d), lambda i: (i, 0)),
        out_shape=jax.ShapeDtypeStruct((m, d), gates.dtype),
        compiler_params=_params(("arbitrary",), est),
        name="gated_merge",
    )(*ret_outs, *dif_outs, gates, gates, gates, gates, w_ret_up, w_dif_up)


def _out_proj_kernel(x_ref, a_ref, w_ref, o_ref):
    o_ref[...] = x_ref[...] + jnp.dot(a_ref[...], w_ref[...], preferred_element_type=F32)


def _out_proj(x, merged, w_out, layer, tm):
    m, d = x.shape
    wbytes = jnp.dtype(w_out.dtype).itemsize
    est = 2 * (2 * tm * d * 4 + tm * d * wbytes + d * d * wbytes) + tm * d * 4
    return pl.pallas_call(
        _out_proj_kernel,
        grid=(m // tm,),
        in_specs=[
            pl.BlockSpec((tm, d), lambda i: (i, 0)),
            pl.BlockSpec((tm, d), lambda i: (i, 0)),
            pl.BlockSpec((None, d, d), lambda i: (layer, 0, 0)),
        ],
        out_specs=pl.BlockSpec((tm, d), lambda i: (i, 0)),
        out_shape=jax.ShapeDtypeStruct((m, d), F32),
        compiler_params=_params(("parallel",), est),
        name="output_projection",
    )(x, merged, w_out)


def _rotary_tables(pos):
    inv = ROPE_BASE ** (-jnp.arange(HALF, dtype=F32) / HALF)
    ang = pos.astype(F32)[:, None] * inv[None, :]
    cos, sin = jnp.cos(ang), jnp.sin(ang)
    return jnp.concatenate([cos, cos], axis=-1), jnp.concatenate([-sin, sin], axis=-1)


def kernel(x_prompt, x_sample, cache_diff_k, cache_diff_v, state_ret, ffn1_norm, ffn1_gate, ffn1_up, ffn1_down, mix_norm, w_in, q_norm, k_norm, lambda_q1, lambda_k1, lambda_q2, lambda_k2, subln, w_ret_up, w_dif_up, w_out, ffn2_norm, ffn2_gate, ffn2_up, ffn2_down, rel_bias):
    nb, seq, d = x_prompt.shape
    db, dseq, _ = x_sample.shape
    depth, _, past, nh_d, _ = cache_diff_k.shape
    nh_r = state_ret.shape[2]
    u = d // 2
    assert nh_r * HEAD == u and nh_d * HEAD == u and dseq == CHUNK and seq % CHUNK == 0
    assert w_in.shape[-1] == N_SEG * u
    mp, ms = nb * seq, db * dseq

    cast = lambda w: w.astype(MXU_DTYPE)
    wg1, wu1, wd1 = cast(ffn1_gate), cast(ffn1_up), cast(ffn1_down)
    wg2, wu2, wd2 = cast(ffn2_gate), cast(ffn2_up), cast(ffn2_down)
    w_in_c, w_ret_c, w_dif_c, w_out_c = cast(w_in), cast(w_ret_up), cast(w_dif_up), cast(w_out)
    row3 = lambda g: g.reshape(depth, 1, g.shape[-1])
    n1, nmix, n2 = row3(ffn1_norm), row3(mix_norm), row3(ffn2_norm)
    qn2 = row3(jnp.concatenate([q_norm, q_norm], axis=-1))
    kn2 = row3(jnp.concatenate([k_norm, k_norm], axis=-1))
    subln2 = row3(subln)
    lam_params = jnp.stack([lambda_q1, lambda_k1, lambda_q2, lambda_k2], axis=1)
    cache_k = cache_diff_k.reshape(depth, db * past * nh_d, HEAD)
    cache_v = cache_diff_v.reshape(depth, db * past * nh_d, HEAD)

    pos = jnp.concatenate([jnp.tile(jnp.arange(seq, dtype=jnp.int32), nb),
                           past + jnp.tile(jnp.arange(dseq, dtype=jnp.int32), db)])
    rot = _rotary_tables(pos)

    tm = _pick(math.gcd(mp, ms), ROW_TILES)
    groups = (mp, ms)
    zero_state = jnp.zeros((nb, nh_r, HEAD, HEAD), F32)
    t_ret = _pick(seq, RET_BLOCKS)

    x = (x_prompt.reshape(mp, d), x_sample.reshape(ms, d))
    kps, kss, vps, vss, states_p, states_s = [], [], [], [], [], []
    for l in range(depth):
        lam_init = 0.8 - 0.6 * math.exp(-0.3 * l)
        x, h = _ffn(x, n1, wg1, wu1, wd1, l, tm, next_gain=nmix)

        seg = functools.partial(_segment, h, w_in_c, l)
        rq = seg(SEG_RQ, 1, tm, "rotary", tables=rot)
        rk = seg(SEG_RK, 1, tm, "rotary", tables=rot, scale=HEAD ** -0.5)
        rv = seg(SEG_RV, 1, tm, "cast")
        rg = seg(SEG_RG, 1, tm, "silu")
        dq = seg(SEG_DQ, 1, tm, "norm", gain=qn2, scale=HALF ** -0.5)
        dk, k_p, k_s = seg(SEG_DK, 1, tm, "norm_keep", gain=kn2, group_rows=groups)
        dv, v_p, v_s = seg(SEG_DV, 1, tm, "keep", group_rows=groups)
        gates = seg(SEG_GATES, N_SEG - SEG_GATES, tm, "sigmoid")

        ret_p, st_p = _retention(rq, rk, rv, rg, zero_state, t_ret, nb, seq // t_ret, 0)
        ret_s, st_s = _retention(rq, rk, rv, rg, state_ret[l].astype(F32), dseq, db, 1, mp // dseq)

        dif_p = _attn_prompt(dq, dk, dv, rel_bias, lam_params, subln2, l, nb, seq, nh_d, lam_init)
        dif_s = _attn_sample(dq, dk, dv, cache_k, cache_v, rel_bias, lam_params, subln2, l, db, dseq, past,
                             mp, nh_d, lam_init)

        merged = _merge((ret_p, ret_s), (dif_p, dif_s), gates, w_ret_c, w_dif_c, l, tm)
        x = _out_proj(x, merged, w_out_c, l, tm)
        x = _ffn(x, n2, wg2, wu2, wd2, l, tm, split_out=groups if l == depth - 1 else None)

        for acc, val in ((kps, k_p), (kss, k_s), (vps, v_p), (vss, v_s), (states_p, st_p), (states_s, st_s)):
            acc.append(val)

    y_p, y_s = x
    kv_p = lambda parts: jnp.stack(parts).reshape(depth, nb, seq, nh_d, HEAD)
    kv_s = lambda parts: jnp.stack(parts).reshape(depth, db, dseq, nh_d, HEAD)
    return (y_p.reshape(nb, seq, d), y_s.reshape(db, dseq, d),
            kv_p(kps).astype(cache_diff_k.dtype), kv_p(vps).astype(cache_diff_v.dtype),
            jnp.stack(states_p).astype(state_ret.dtype),
            kv_s(kss).astype(cache_diff_k.dtype), kv_s(vss).astype(cache_diff_v.dtype),
            jnp.stack(states_s).astype(state_ret.dtype))
```

```python
import functools
import math

import numpy as np
import jax
import jax.numpy as jnp
from jax import lax
from jax.experimental import pallas as pl
from jax.experimental.pallas import tpu as pltpu

F32 = jnp.float32
MXU_DTYPE = jnp.bfloat16

CHUNK = 64
HEAD = 128
HALF = HEAD // 2
ROPE_BASE = 10000.0
N_BUCKETS = 32
MAX_DISTANCE = 128
EPS = 1e-6
MASK_VALUE = -1e30

VMEM_LIMIT_CAP = 60 * 1024 * 1024
MIB = 1024 * 1024

ROW_TILES = (512, 256, 128, 64)
PROJ_ROW_TILES = (1536, 1024, 768, 512, 256, 128, 64)
FF_TILES = (512, 256, 128)
ATTN_BLOCKS = (512, 256, 128)
HEADS_PER_STEP = (4, 2, 1)
RET_BLOCKS = (256, 128, 64)
CACHE_TILES = (512, 256, 128, 64)


def _pick(n, prefs):
    for p in prefs:
        if n % p == 0:
            return p
    raise ValueError(f"no tile in {prefs} divides {n}")


def _params(semantics, est_bytes):
    limit = int(min(max(est_bytes + 8 * MIB, 32 * MIB), VMEM_LIMIT_CAP))
    return pltpu.CompilerParams(dimension_semantics=semantics, vmem_limit_bytes=limit)


def _rms_scale(x):
    return lax.rsqrt(jnp.mean(x * x, axis=-1, keepdims=True) + EPS)


def _group_specs(tm, width, n_first, row_axis=0):
    first = pl.BlockSpec((tm, width), lambda *g: (jnp.minimum(g[row_axis], n_first - 1), 0))
    second = pl.BlockSpec((tm, width), lambda *g: (jnp.maximum(g[row_axis] - n_first, 0), 0))
    return [first, second]


def _group_tile(refs, n_first, row_axis=0):
    if len(refs) == 1:
        return refs[0][...]
    return jnp.where(pl.program_id(row_axis) < n_first, refs[0][...], refs[1][...])


def _store_group_tile(refs, n_first, value, row_axis=0):
    if len(refs) == 1:
        refs[0][...] = value
        return
    i = pl.program_id(row_axis)

    @pl.when(i < n_first)
    def _():
        refs[0][...] = value

    @pl.when(i >= n_first)
    def _():
        refs[1][...] = value


def _ffn_kernel(*refs, nf, n_in, n_out, n_first, norm_out):
    x_refs = refs[:n_in]
    g_ref, wg_ref, wu_ref, wd_ref = refs[n_in:n_in + 4]
    pos = n_in + 4
    g2_ref = refs[pos] if norm_out else None
    pos += int(norm_out)
    o_refs = refs[pos:pos + n_out]
    pos += n_out
    hn_ref = refs[pos] if norm_out else None
    pos += int(norm_out)
    h_ref, acc_ref = refs[pos:]
    f = pl.program_id(1)

    @pl.when(f == 0)
    def _():
        x = _group_tile(x_refs, n_first)
        h_ref[...] = (x * _rms_scale(x) * g_ref[...]).astype(h_ref.dtype)
        acc_ref[...] = jnp.zeros_like(acc_ref)

    h = h_ref[...]
    gate = jnp.dot(h, wg_ref[...], preferred_element_type=F32)
    up = jnp.dot(h, wu_ref[...], preferred_element_type=F32)
    act = (gate * jax.nn.sigmoid(gate) * up).astype(wd_ref.dtype)
    acc_ref[...] += jnp.dot(act, wd_ref[...], preferred_element_type=F32)

    @pl.when(f == nf - 1)
    def _():
        y = _group_tile(x_refs, n_first) + 0.5 * acc_ref[...]
        _store_group_tile(o_refs, n_first, y)
        if norm_out:
            hn_ref[...] = (y * _rms_scale(y) * g2_ref[...]).astype(hn_ref.dtype)


def _ffn(xs, gain, wg, wu, wd, layer, tm, split_out=None, next_gain=None):
    xs = tuple(xs) if isinstance(xs, (tuple, list)) else (xs,)
    m = sum(x.shape[0] for x in xs)
    d = xs[0].shape[1]
    n_first = (xs[0].shape[0] if len(xs) == 2 else split_out[0] if split_out else m) // tm
    ff = wg.shape[-1]
    tf = _pick(ff, FF_TILES)
    nf = ff // tf
    norm_out = next_gain is not None
    wbytes = jnp.dtype(wg.dtype).itemsize
    est = (2 * ((len(xs) + (2 if split_out else 1)) * tm * d * 4 + 3 * d * tf * wbytes + tm * d * wbytes)
           + tm * d * (4 + wbytes) + 4 * tm * tf * 4)
    rows = lambda: pl.BlockSpec((tm, d), lambda i, f: (i, 0))
    gain_spec = pl.BlockSpec((None, 1, d), lambda i, f: (layer, 0, 0))
    in_specs = (_group_specs(tm, d, n_first) if len(xs) == 2 else [rows()]) + [
        gain_spec,
        pl.BlockSpec((None, d, tf), lambda i, f: (layer, 0, f)),
        pl.BlockSpec((None, d, tf), lambda i, f: (layer, 0, f)),
        pl.BlockSpec((None, tf, d), lambda i, f: (layer, f, 0)),
    ]
    args = [*xs, gain, wg, wu, wd]
    if split_out:
        out_specs = _group_specs(tm, d, n_first)
        out_shape = [jax.ShapeDtypeStruct((r, d), F32) for r in split_out]
    else:
        out_specs = [rows()]
        out_shape = [jax.ShapeDtypeStruct((m, d), F32)]
    if norm_out:
        in_specs.append(gain_spec)
        args.append(next_gain)
        out_specs.append(rows())
        out_shape.append(jax.ShapeDtypeStruct((m, d), wg.dtype))
    outs = pl.pallas_call(
        functools.partial(_ffn_kernel, nf=nf, n_in=len(xs), n_out=2 if split_out else 1, n_first=n_first,
                          norm_out=norm_out),
        grid=(m // tm, nf),
        in_specs=in_specs,
        out_specs=out_specs,
        out_shape=out_shape,
        scratch_shapes=[pltpu.VMEM((tm, d), wg.dtype), pltpu.VMEM((tm, d), F32)],
        compiler_params=_params(("arbitrary", "arbitrary"), est),
        name="swiglu_half_step",
    )(*args)
    return outs if len(outs) > 1 else outs[0]


SEG_RQ, SEG_RK, SEG_RV, SEG_RG, SEG_DQ, SEG_DK, SEG_DV, SEG_GATES, N_SEG = 0, 1, 2, 3, 4, 5, 6, 7, 11


def _rotate_half_pairs(a, cos2, sin2):
    return a * cos2 + pltpu.roll(a, HALF, 1) * sin2


def _component_rms_norm(a, gain2):
    lo = lax.broadcasted_iota(jnp.int32, a.shape, 1) < HALF
    sq = a * a
    s_all = jnp.sum(sq, axis=-1, keepdims=True)
    s_lo = jnp.sum(jnp.where(lo, sq, 0.0), axis=-1, keepdims=True)
    ms = jnp.where(lo, s_lo, s_all - s_lo) * (1.0 / HALF)
    return a * lax.rsqrt(ms + EPS) * gain2


def _segment_kernel(h_ref, w_ref, *refs, kind, scale, nheads, n_first):
    acc = jnp.dot(h_ref[...], w_ref[...], preferred_element_type=F32)
    heads = [slice(h * HEAD, (h + 1) * HEAD) for h in range(nheads)]
    if kind == "rotary":
        cos_ref, sin_ref, o_ref = refs
        for sl in heads:
            r = _rotate_half_pairs(acc[:, sl], cos_ref[...], sin_ref[...])
            o_ref[:, sl] = (r if scale == 1.0 else r * scale).astype(o_ref.dtype)
    elif kind == "cast":
        (o_ref,) = refs
        o_ref[...] = acc.astype(o_ref.dtype)
    elif kind == "silu":
        (o_ref,) = refs
        o_ref[...] = (acc * jax.nn.sigmoid(acc)).astype(o_ref.dtype)
    elif kind == "sigmoid":
        (o_ref,) = refs
        o_ref[...] = jax.nn.sigmoid(acc).astype(o_ref.dtype)
    elif kind == "norm":
        gain_ref, o_ref = refs
        for sl in heads:
            o_ref[:, sl] = (_component_rms_norm(acc[:, sl], gain_ref[...]) * scale).astype(o_ref.dtype)
    elif kind == "norm_keep":
        gain_ref, o_ref, fp_ref, fs_ref = refs
        normed = jnp.concatenate([_component_rms_norm(acc[:, sl], gain_ref[...]) for sl in heads], axis=1)
        _store_group_tile((fp_ref, fs_ref), n_first, normed, row_axis=1)
        o_ref[...] = normed.astype(o_ref.dtype)
    elif kind == "keep":
        o_ref, fp_ref, fs_ref = refs
        _store_group_tile((fp_ref, fs_ref), n_first, acc, row_axis=1)
        o_ref[...] = acc.astype(o_ref.dtype)
    else:
        raise ValueError(kind)


def _segment(h, w_in, layer, seg0, nseg, tm, kind, *, scale=1.0, tables=(), gain=None, group_rows=None):
    m, d = h.shape
    u = d // 2
    nheads = u // HEAD
    n_first = group_rows[0] // tm if group_rows else 0
    in_specs = [pl.BlockSpec((tm, d), lambda s, i: (i, 0)),
                pl.BlockSpec((None, d, u), lambda s, i: (layer, 0, seg0 + s))]
    args = [h, w_in]
    for tab in tables:
        in_specs.append(pl.BlockSpec((tm, HEAD), lambda s, i: (i, 0)))
        args.append(tab)
    if gain is not None:
        in_specs.append(pl.BlockSpec((None, 1, HEAD), lambda s, i: (layer, 0, 0)))
        args.append(gain)
    out_specs = [pl.BlockSpec((tm, u), lambda s, i: (i, s))]
    out_shape = [jax.ShapeDtypeStruct((m, nseg * u), h.dtype)]
    if group_rows:
        out_specs += _group_specs(tm, u, n_first, row_axis=1)
        out_shape += [jax.ShapeDtypeStruct((r, u), F32) for r in group_rows]
    hb = jnp.dtype(h.dtype).itemsize
    est = 2 * (tm * d * hb + d * u * hb + tm * u * hb + 2 * tm * u * 4 + 2 * tm * HEAD * 4) + 4 * tm * u * 4
    outs = pl.pallas_call(
        functools.partial(_segment_kernel, kind=kind, scale=scale, nheads=nheads, n_first=n_first),
        grid=(nseg, m // tm),
        in_specs=in_specs,
        out_specs=out_specs,
        out_shape=out_shape,
        compiler_params=_params(("arbitrary", "arbitrary"), est),
        name="input_projection_" + kind,
    )(*args)
    return outs if len(outs) > 1 else outs[0]


def _retention_kernel(q_ref, k_ref, v_ref, g_ref, s0_ref, d_ref, wq_ref, we_ref, dec_ref, o_ref, sout_ref, st_ref,
                      *, nheads, nblk):
    t = pl.program_id(1)

    @pl.when(t == 0)
    def _():
        st_ref[...] = s0_ref[...]

    for h in range(nheads):
        sl = slice(h * HEAD, (h + 1) * HEAD)
        q = q_ref[:, sl]
        k = k_ref[:, sl]
        v = v_ref[:, sl]
        s = lax.dot_general(q, k, (((1,), (1,)), ((), ())), preferred_element_type=F32) * d_ref[h]
        o = jnp.dot(s.astype(v.dtype), v, preferred_element_type=F32)
        state = st_ref[h]
        o = o + wq_ref[:, sl] * jnp.dot(q, state.astype(q.dtype), preferred_element_type=F32)
        kw = (k.astype(F32) * we_ref[:, sl]).astype(k.dtype)
        kv = lax.dot_general(kw, v, (((0,), (0,)), ((), ())), preferred_element_type=F32)
        st_ref[h] = state * dec_ref[h:h + 1, :] + kv
        r = o * _rms_scale(o)
        o_ref[:, sl] = (r * g_ref[:, sl].astype(F32)).astype(o_ref.dtype)

    @pl.when(t == nblk - 1)
    def _():
        sout_ref[...] = st_ref[...]


def _retention_tables(t, nheads):
    log_g = jnp.log(1.0 - 2.0 ** (-5.0 - jnp.arange(nheads, dtype=F32)))
    idx = jnp.arange(t, dtype=F32)
    dist = jnp.abs(idx[:, None] - idx[None, :])
    ci = np.arange(t) // CHUNK
    visible = jnp.asarray(ci[None, :] <= ci[:, None])
    dmat = jnp.where(visible[None], jnp.exp(log_g[:, None, None] * dist[None]), 0.0)
    wq = jnp.exp(log_g[None, :] * (idx + 1.0)[:, None])
    we = jnp.exp(log_g[None, :] * (t - 1.0 - idx)[:, None])
    dec = jnp.exp(log_g * t)
    expand = lambda a: jnp.repeat(a, HEAD, axis=1)
    return dmat, expand(wq), expand(we), jnp.broadcast_to(dec[:, None], (nheads, HEAD))


def _retention(q, k, v, g, s0, t, nbatch, nblk, row_block0):
    nheads = s0.shape[1]
    u = nheads * HEAD
    dmat, wq, we, dec = _retention_tables(t, nheads)
    rows = pl.BlockSpec((t, u), lambda b, i: (row_block0 + b * nblk + i, 0))
    whole = lambda a: pl.BlockSpec(a.shape, lambda b, i: (0,) * a.ndim)
    state_spec = pl.BlockSpec((None, nheads, HEAD, HEAD), lambda b, i: (b, 0, 0, 0))
    pbytes = jnp.dtype(q.dtype).itemsize
    est = (2 * (5 * t * u * pbytes + 2 * nheads * HEAD * HEAD * 4 + nheads * t * t * 4 + 2 * t * u * 4)
           + nheads * HEAD * HEAD * 4 + 6 * t * max(t, HEAD) * 4)
    return pl.pallas_call(
        functools.partial(_retention_kernel, nheads=nheads, nblk=nblk),
        grid=(nbatch, nblk),
        in_specs=[rows, rows, rows, rows, state_spec, whole(dmat), whole(wq), whole(we), whole(dec)],
        out_specs=[
            pl.BlockSpec((t, u), lambda b, i: (b * nblk + i, 0)),
            state_spec,
        ],
        out_shape=[
            jax.ShapeDtypeStruct((nbatch * nblk * t, u), q.dtype),
            jax.ShapeDtypeStruct((nbatch, nheads, HEAD, HEAD), F32),
        ],
        scratch_shapes=[pltpu.VMEM((nheads, HEAD, HEAD), F32)],
        compiler_params=_params(("parallel", "arbitrary"), est),
        name="retention",
    )(q, k, v, g, s0, dmat, wq, we, dec)


def _bucket_thresholds():
    nb = N_BUCKETS // 2
    me = nb // 2
    out = []
    for k in range(1, nb - me):
        n = me
        while n ** (nb - me) * me ** k < me ** (nb - me) * MAX_DISTANCE ** k:
            n += 1
        out.append(n)
    return out


def _t5_bucket_np(rel):
    nb = N_BUCKETS // 2
    me = nb // 2
    n = np.abs(rel)
    large = np.full(rel.shape, me, np.int64)
    for thr in _bucket_thresholds():
        large += (n >= thr)
    large = np.minimum(large, nb - 1)
    return (np.where(rel > 0, nb, 0) + np.where(n < me, n, large)).astype(np.int32)


def _bias_kernel(rb_ref, idx_ref, mask_ref, o_ref):
    h = pl.program_id(0)
    idx = idx_ref[...]
    out = mask_ref[...]
    for b in range(N_BUCKETS):
        out = out + jnp.where(idx == b, rb_ref[b, h], 0.0)
    o_ref[...] = out


def _bias_table(rel_bias, qpos, kpos):
    nheads = rel_bias.shape[1]
    rel = kpos[None, :] - qpos[:, None]
    idx = jnp.asarray(_t5_bucket_np(rel))
    mask = jnp.asarray(np.where((kpos[None, :] // CHUNK) <= (qpos[:, None] // CHUNK), 0.0, MASK_VALUE)
                       .astype(np.float32))
    nq, nk = rel.shape
    return pl.pallas_call(
        _bias_kernel,
        grid=(nheads,),
        in_specs=[
            pl.BlockSpec(memory_space=pltpu.SMEM),
            pl.BlockSpec((nq, nk), lambda h: (0, 0)),
            pl.BlockSpec((nq, nk), lambda h: (0, 0)),
        ],
        out_specs=pl.BlockSpec((None, nq, nk), lambda h: (h, 0, 0)),
        out_shape=jax.ShapeDtypeStruct((nheads, nq, nk), F32),
        compiler_params=_params(("arbitrary",), 6 * nq * nk * 4),
        name="relative_bias_table",
    )(rel_bias, idx, mask)


def _stack_components(q):
    lo = lax.broadcasted_iota(jnp.int32, q.shape, 1) < HALF
    zero = jnp.zeros_like(q)
    return jnp.concatenate([jnp.where(lo, q, zero), jnp.where(lo, zero, q)], axis=0)


def _biased_scores(qs, k, bias):
    s = lax.dot_general(qs, k, (((1,), (1,)), ((), ())), preferred_element_type=F32)
    if bias.ndim == 2:
        t = bias.shape[0]
        return (s.reshape(2, t, s.shape[-1]) + bias[None]).reshape(s.shape), 0.0
    return s, bias


def _softmax_step(qs, k, v, bias, stats, g):
    m_ref, l_ref, acc_ref = stats
    s, c = _biased_scores(qs, k, bias)
    m_prev = m_ref[g]
    m_new = jnp.maximum(m_prev, jnp.max(s, axis=-1, keepdims=True) + c)
    alpha = jnp.exp(m_prev - m_new)
    p = jnp.exp(s - (m_new - c))
    l_ref[g] = alpha * l_ref[g] + jnp.sum(p, axis=-1, keepdims=True)
    acc_ref[g] = alpha * acc_ref[g] + jnp.dot(p.astype(v.dtype), v, preferred_element_type=F32)
    m_ref[g] = m_new


def _softmax_init(stats):
    m_ref, l_ref, acc_ref = stats
    m_ref[...] = jnp.full_like(m_ref, MASK_VALUE)
    l_ref[...] = jnp.zeros_like(l_ref)
    acc_ref[...] = jnp.zeros_like(acc_ref)


def _softmax_scratch(groups, t):
    return [pltpu.VMEM((groups, 2 * t, 1), F32), pltpu.VMEM((groups, 2 * t, 1), F32),
            pltpu.VMEM((groups, 2 * t, HEAD), F32)]


def _lambda_value(lam_ref, lam_init):
    a = lam_ref[...]
    e1 = jnp.exp(jnp.sum(a[0:1] * a[1:2], axis=-1, keepdims=True))
    e2 = jnp.exp(jnp.sum(a[2:3] * a[3:4], axis=-1, keepdims=True))
    return e1 - e2 + lam_init


def _diff_finish(t, lam, lam_init, subln, stats, g):
    _, l_ref, acc_ref = stats
    acc = acc_ref[g]
    l = l_ref[g]
    o = acc[:t] / l[:t] - lam * (acc[t:] / l[t:])
    return o * _rms_scale(o) * subln * (1.0 - lam_init)


def _attn_prompt_kernel(rb_ref, q_ref, k_ref, v_ref, bias_ref, lam_ref, sub_ref, o_ref, *stats,
                        t, groups, lam_init, far_bucket):
    hp = pl.program_id(1)
    qi = pl.program_id(2)
    m_ref, l_ref, acc_ref = stats
    lanes = [slice(g * HEAD, (g + 1) * HEAD) for g in range(groups)]
    qs = [_stack_components(q_ref[:, sl]) for sl in lanes]
    far_bias = [rb_ref[far_bucket, hp * groups + g] for g in range(groups)]
    _softmax_init(stats)

    def sweep(step):
        def run(j, bias_of):
            step(pl.ds(pl.multiple_of(j * t, t), t), [bias_of(g) for g in range(groups)])

        def far_step(j, carry):
            run(j, lambda g: far_bias[g])
            return carry

        lax.fori_loop(0, jnp.maximum(qi - 1, 0), far_step, 0)

        @pl.when(qi > 0)
        def _():
            run(qi - 1, lambda g: bias_ref[g, :, :t])

        run(qi, lambda g: bias_ref[g, :, t:])

    def max_step(rows, biases):
        scores = [_biased_scores(qs[g], k_ref[rows, lanes[g]], biases[g]) for g in range(groups)]
        for g, (s, c) in enumerate(scores):
            m_ref[g] = jnp.maximum(m_ref[g], jnp.max(s, axis=-1, keepdims=True) + c)

    def acc_step(rows, biases):
        scores = [_biased_scores(qs[g], k_ref[rows, lanes[g]], biases[g]) for g in range(groups)]
        probs = [jnp.exp(s - (m_ref[g] - c)) for g, (s, c) in enumerate(scores)]
        for g, p in enumerate(probs):
            l_ref[g] += jnp.sum(p, axis=-1, keepdims=True)
            v = v_ref[rows, lanes[g]]
            acc_ref[g] += jnp.dot(p.astype(v.dtype), v, preferred_element_type=F32)

    sweep(max_step)
    sweep(acc_step)

    lam = _lambda_value(lam_ref, lam_init)
    for g, sl in enumerate(lanes):
        o_ref[:, sl] = _diff_finish(t, lam, lam_init, sub_ref[...], stats, g).astype(o_ref.dtype)


def _attn_prompt(q, k, v, rel_bias, lam_params, subln2, layer, nbatch, seq, nheads, lam_init):
    u = nheads * HEAD
    t = _pick(seq, ATTN_BLOCKS)
    groups = _pick(nheads, HEADS_PER_STEP)
    assert t % CHUNK == 0 and t + 1 >= _bucket_thresholds()[-1]
    nq = seq // t
    w = groups * HEAD
    r = np.arange(t)
    bias = _bias_table(rel_bias, r + t, np.arange(2 * t))
    pbytes = jnp.dtype(q.dtype).itemsize
    est = (2 * (2 * t * w * pbytes + 2 * seq * w * pbytes + groups * 2 * t * t * 4)
           + groups * (2 * t * (HEAD + 2 * 128) * 4 + 6 * 2 * t * t * 4))
    return pl.pallas_call(
        functools.partial(_attn_prompt_kernel, t=t, groups=groups, lam_init=lam_init,
                          far_bucket=N_BUCKETS // 2 - 1),
        grid=(nbatch, nheads // groups, nq),
        in_specs=[
            pl.BlockSpec(memory_space=pltpu.SMEM),
            pl.BlockSpec((t, w), lambda b, h, i: (b * nq + i, h)),
            pl.BlockSpec((seq, w), lambda b, h, i: (b, h)),
            pl.BlockSpec((seq, w), lambda b, h, i: (b, h)),
            pl.BlockSpec((groups, t, 2 * t), lambda b, h, i: (h, 0, 0)),
            pl.BlockSpec((None, 4, HALF), lambda b, h, i: (layer, 0, 0)),
            pl.BlockSpec((None, 1, HEAD), lambda b, h, i: (layer, 0, 0)),
        ],
        out_specs=pl.BlockSpec((t, w), lambda b, h, i: (b * nq + i, h)),
        out_shape=jax.ShapeDtypeStruct((nbatch * seq, u), q.dtype),
        scratch_shapes=_softmax_scratch(groups, t),
        compiler_params=_params(("parallel", "parallel", "arbitrary"), est),
        name="diff_attention_prompt",
    )(rel_bias, q, k, v, bias, lam_params, subln2)


def _attn_sample_kernel(q_ref, kn_ref, vn_ref, kc_ref, vc_ref, bc_ref, bn_ref, lam_ref, sub_ref,
                        o_ref, *stats, t, tk, ncache, nheads, lam_init):
    j = pl.program_id(1)
    lanes = [slice(h * HEAD, (h + 1) * HEAD) for h in range(nheads)]

    @pl.when(j == 0)
    def _():
        _softmax_init(stats)

    qs = [_stack_components(q_ref[:, sl]) for sl in lanes]
    head_rows = lambda ref, h: ref[pl.ds(h, tk, stride=nheads), :].astype(qs[0].dtype)
    for h in range(nheads):
        _softmax_step(qs[h], head_rows(kc_ref, h), head_rows(vc_ref, h), bc_ref[h], stats, h)

    @pl.when(j == ncache - 1)
    def _():
        lam = _lambda_value(lam_ref, lam_init)
        for h, sl in enumerate(lanes):
            _softmax_step(qs[h], kn_ref[:, sl], vn_ref[:, sl], bn_ref[h], stats, h)
            o_ref[:, sl] = _diff_finish(t, lam, lam_init, sub_ref[...], stats, h).astype(o_ref.dtype)


def _attn_sample(q, k, v, cache_k, cache_v, rel_bias, lam_params, subln2, layer, nbatch, t, past,
                 row0, nheads, lam_init):
    u = nheads * HEAD
    assert row0 % t == 0
    rb0 = row0 // t
    tk = _pick(past, CACHE_TILES)
    qpos = past + np.arange(t)
    ncache = past // tk
    bias_c = _bias_table(rel_bias, qpos, np.arange(past))
    bias_c = bias_c.reshape(nheads, t, ncache, tk).transpose(2, 0, 1, 3)
    bias_n = _bias_table(rel_bias, qpos, qpos)
    pbytes = jnp.dtype(q.dtype).itemsize
    est = (2 * (4 * t * u * pbytes + 2 * tk * nheads * HEAD * 4 + nheads * t * (tk + t) * 4)
           + nheads * (2 * t * (HEAD + 2 * 128) * 4 + 6 * 2 * t * tk * 4))
    rows = pl.BlockSpec((t, u), lambda b, j: (rb0 + b, 0))
    cache = pl.BlockSpec((None, tk * nheads, HEAD), lambda b, j: (layer, b * ncache + j, 0))
    return pl.pallas_call(
        functools.partial(_attn_sample_kernel, t=t, tk=tk, ncache=ncache, nheads=nheads, lam_init=lam_init),
        grid=(nbatch, ncache),
        in_specs=[
            rows, rows, rows, cache, cache,
            pl.BlockSpec((None, nheads, t, tk), lambda b, j: (j, 0, 0, 0)),
            pl.BlockSpec((nheads, t, t), lambda b, j: (0, 0, 0)),
            pl.BlockSpec((None, 4, HALF), lambda b, j: (layer, 0, 0)),
            pl.BlockSpec((None, 1, HEAD), lambda b, j: (layer, 0, 0)),
        ],
        out_specs=pl.BlockSpec((t, u), lambda b, j: (b, 0)),
        out_shape=jax.ShapeDtypeStruct((nbatch * t, u), q.dtype),
        scratch_shapes=_softmax_scratch(nheads, t),
        compiler_params=_params(("parallel", "arbitrary"), est),
        name="diff_attention_sample",
    )(q, k, v, cache_k, cache_v, bias_c, bias_n, lam_params, subln2)


def _merge_kernel(ap_ref, as_ref, bp_ref, bs_ref, ga0_ref, ga1_ref, gb0_ref, gb1_ref, wa_ref, wb_ref, o_ref,
                  *, u, n_first):
    a = _group_tile((ap_ref, as_ref), n_first)
    b = _group_tile((bp_ref, bs_ref), n_first)
    ya = jnp.dot(a, wa_ref[...], preferred_element_type=F32)
    yb = jnp.dot(b, wb_ref[...], preferred_element_type=F32)
    for c, (ga, gb) in enumerate(((ga0_ref, gb0_ref), (ga1_ref, gb1_ref))):
        sl = slice(c * u, (c + 1) * u)
        o_ref[:, sl] = (ga[...].astype(F32) * ya[:, sl] + gb[...].astype(F32) * yb[:, sl]).astype(o_ref.dtype)


def _merge(ret_outs, dif_outs, gates, w_ret_up, w_dif_up, layer, tm):
    m = gates.shape[0]
    u = ret_outs[0].shape[1]
    d = 2 * u
    n_first = ret_outs[0].shape[0] // tm
    gate = lambda c: pl.BlockSpec((tm, u), lambda i: (i, c))
    rows = _group_specs(tm, u, n_first)
    wspec = pl.BlockSpec((None, u, d), lambda i: (layer, 0, 0))
    pbytes = jnp.dtype(gates.dtype).itemsize
    est = 2 * (8 * tm * u * pbytes + 2 * u * d * pbytes + tm * d * pbytes) + 3 * tm * d * 4
    return pl.pallas_call(
        functools.partial(_merge_kernel, u=u, n_first=n_first),
        grid=(m // tm,),
        in_specs=rows + rows + [gate(0), gate(1), gate(2), gate(3), wspec, wspec],
        out_specs=pl.BlockSpec((tm, d), lambda i: (i, 0)),
        out_shape=jax.ShapeDtypeStruct((m, d), gates.dtype),
        compiler_params=_params(("arbitrary",), est),
        name="gated_merge",
    )(*ret_outs, *dif_outs, gates, gates, gates, gates, w_ret_up, w_dif_up)


def _out_proj_kernel(x_ref, a_ref, w_ref, o_ref):
    o_ref[...] = x_ref[...] + jnp.dot(a_ref[...], w_ref[...], preferred_element_type=F32)


def _out_proj(x, merged, w_out, layer, tm):
    m, d = x.shape
    wbytes = jnp.dtype(w_out.dtype).itemsize
    est = 2 * (2 * tm * d * 4 + tm * d * wbytes + d * d * wbytes) + tm * d * 4
    return pl.pallas_call(
        _out_proj_kernel,
        grid=(m // tm,),
        in_specs=[
            pl.BlockSpec((tm, d), lambda i: (i, 0)),
            pl.BlockSpec((tm, d), lambda i: (i, 0)),
            pl.BlockSpec((None, d, d), lambda i: (layer, 0, 0)),
        ],
        out_specs=pl.BlockSpec((tm, d), lambda i: (i, 0)),
        out_shape=jax.ShapeDtypeStruct((m, d), F32),
        compiler_params=_params(("parallel",), est),
        name="output_projection",
    )(x, merged, w_out)


def _rotary_tables(pos):
    inv = ROPE_BASE ** (-jnp.arange(HALF, dtype=F32) / HALF)
    ang = pos.astype(F32)[:, None] * inv[None, :]
    cos, sin = jnp.cos(ang), jnp.sin(ang)
    return jnp.concatenate([cos, cos], axis=-1), jnp.concatenate([-sin, sin], axis=-1)


def kernel(x_prompt, x_sample, cache_diff_k, cache_diff_v, state_ret, ffn1_norm, ffn1_gate, ffn1_up, ffn1_down, mix_norm, w_in, q_norm, k_norm, lambda_q1, lambda_k1, lambda_q2, lambda_k2, subln, w_ret_up, w_dif_up, w_out, ffn2_norm, ffn2_gate, ffn2_up, ffn2_down, rel_bias):
    nb, seq, d = x_prompt.shape
    db, dseq, _ = x_sample.shape
    depth, _, past, nh_d, _ = cache_diff_k.shape
    nh_r = state_ret.shape[2]
    u = d // 2
    assert nh_r * HEAD == u and nh_d * HEAD == u and dseq == CHUNK and seq % CHUNK == 0
    assert w_in.shape[-1] == N_SEG * u
    mp, ms = nb * seq, db * dseq

    cast = lambda w: w.astype(MXU_DTYPE)
    wg1, wu1, wd1 = cast(ffn1_gate), cast(ffn1_up), cast(ffn1_down)
    wg2, wu2, wd2 = cast(ffn2_gate), cast(ffn2_up), cast(ffn2_down)
    w_in_c, w_ret_c, w_dif_c, w_out_c = cast(w_in), cast(w_ret_up), cast(w_dif_up), cast(w_out)
    row3 = lambda g: g.reshape(depth, 1, g.shape[-1])
    n1, nmix, n2 = row3(ffn1_norm), row3(mix_norm), row3(ffn2_norm)
    qn2 = row3(jnp.concatenate([q_norm, q_norm], axis=-1))
    kn2 = row3(jnp.concatenate([k_norm, k_norm], axis=-1))
    subln2 = row3(subln)
    lam_params = jnp.stack([lambda_q1, lambda_k1, lambda_q2, lambda_k2], axis=1)
    cache_k = cache_diff_k.reshape(depth, db * past * nh_d, HEAD)
    cache_v = cache_diff_v.reshape(depth, db * past * nh_d, HEAD)

    pos = jnp.concatenate([jnp.tile(jnp.arange(seq, dtype=jnp.int32), nb),
                           past + jnp.tile(jnp.arange(dseq, dtype=jnp.int32), db)])
    rot = _rotary_tables(pos)

    tm = _pick(math.gcd(mp, ms), ROW_TILES)
    tm_proj = _pick(mp + ms, PROJ_ROW_TILES)
    groups = (mp, ms)
    zero_state = jnp.zeros((nb, nh_r, HEAD, HEAD), F32)
    t_ret = _pick(seq, RET_BLOCKS)

    x = (x_prompt.reshape(mp, d), x_sample.reshape(ms, d))
    kps, kss, vps, vss, states_p, states_s = [], [], [], [], [], []
    for l in range(depth):
        lam_init = 0.8 - 0.6 * math.exp(-0.3 * l)
        x, h = _ffn(x, n1, wg1, wu1, wd1, l, tm, next_gain=nmix)

        seg = functools.partial(_segment, h, w_in_c, l)
        rq = seg(SEG_RQ, 1, tm_proj, "rotary", tables=rot)
        rk = seg(SEG_RK, 1, tm_proj, "rotary", tables=rot, scale=HEAD ** -0.5)
        rv = seg(SEG_RV, 1, tm_proj, "cast")
        rg = seg(SEG_RG, 1, tm_proj, "silu")
        dq = seg(SEG_DQ, 1, tm_proj, "norm", gain=qn2, scale=HALF ** -0.5)
        dk, k_p, k_s = seg(SEG_DK, 1, tm, "norm_keep", gain=kn2, group_rows=groups)
        dv, v_p, v_s = seg(SEG_DV, 1, tm, "keep", group_rows=groups)
        gates = seg(SEG_GATES, N_SEG - SEG_GATES, tm_proj, "sigmoid")

        ret_p, st_p = _retention(rq, rk, rv, rg, zero_state, t_ret, nb, seq // t_ret, 0)
        ret_s, st_s = _retention(rq, rk, rv, rg, state_ret[l].astype(F32), dseq, db, 1, mp // dseq)

        dif_p = _attn_prompt(dq, dk, dv, rel_bias, lam_params, subln2, l, nb, seq, nh_d, lam_init)
        dif_s = _attn_sample(dq, dk, dv, cache_k, cache_v, rel_bias, lam_params, subln2, l, db, dseq, past,
                             mp, nh_d, lam_init)

        merged = _merge((ret_p, ret_s), (dif_p, dif_s), gates, w_ret_c, w_dif_c, l, tm)
        x = _out_proj(x, merged, w_out_c, l, tm)
        x = _ffn(x, n2, wg2, wu2, wd2, l, tm, split_out=groups if l == depth - 1 else None)

        for acc, val in ((kps, k_p), (kss, k_s), (vps, v_p), (vss, v_s), (states_p, st_p), (states_s, st_s)):
            acc.append(val)

    y_p, y_s = x
    kv_p = lambda parts: jnp.stack(parts).reshape(depth, nb, seq, nh_d, HEAD)
    kv_s = lambda parts: jnp.stack(parts).reshape(depth, db, dseq, nh_d, HEAD)
    return (y_p.reshape(nb, seq, d), y_s.reshape(db, dseq, d),
            kv_p(kps).astype(cache_diff_k.dtype), kv_p(vps).astype(cache_diff_v.dtype),
            jnp.stack(states_p).astype(state_ret.dtype),
            kv_s(kss).astype(cache_diff_k.dtype), kv_s(vss).astype(cache_diff_v.dtype),
            jnp.stack(states_s).astype(state_ret.dtype))
```

```python
import functools
import math

import numpy as np
import jax
import jax.numpy as jnp
from jax import lax
from jax.experimental import pallas as pl
from jax.experimental.pallas import tpu as pltpu

F32 = jnp.float32
MXU_DTYPE = jnp.bfloat16

CHUNK = 64
HEAD = 128
HALF = HEAD // 2
ROPE_BASE = 10000.0
N_BUCKETS = 32
MAX_DISTANCE = 128
EPS = 1e-6
MASK_VALUE = -1e30

VMEM_LIMIT_CAP = 60 * 1024 * 1024
MIB = 1024 * 1024

ROW_TILES = (512, 256, 128, 64)
PROJ_ROW_TILES = (1536, 1024, 768, 512, 256, 128, 64)
FF_TILES = (512, 256, 128)
ATTN_BLOCKS = (512, 256, 128)
HEADS_PER_STEP = (4, 2, 1)
RET_BLOCKS = (256, 128, 64)
CACHE_TILES = (512, 256, 128, 64)


def _pick(n, prefs):
    for p in prefs:
        if n % p == 0:
            return p
    raise ValueError(f"no tile in {prefs} divides {n}")


def _params(semantics, est_bytes):
    limit = int(min(max(est_bytes + 8 * MIB, 32 * MIB), VMEM_LIMIT_CAP))
    return pltpu.CompilerParams(dimension_semantics=semantics, vmem_limit_bytes=limit)


def _rms_scale(x):
    return lax.rsqrt(jnp.mean(x * x, axis=-1, keepdims=True) + EPS)


def _group_specs(tm, width, n_first, row_axis=0):
    first = pl.BlockSpec((tm, width), lambda *g: (jnp.minimum(g[row_axis], n_first - 1), 0))
    second = pl.BlockSpec((tm, width), lambda *g: (jnp.maximum(g[row_axis] - n_first, 0), 0))
    return [first, second]


def _group_tile(refs, n_first, row_axis=0):
    if len(refs) == 1:
        return refs[0][...]
    return jnp.where(pl.program_id(row_axis) < n_first, refs[0][...], refs[1][...])


def _store_group_tile(refs, n_first, value, row_axis=0):
    if len(refs) == 1:
        refs[0][...] = value
        return
    i = pl.program_id(row_axis)

    @pl.when(i < n_first)
    def _():
        refs[0][...] = value

    @pl.when(i >= n_first)
    def _():
        refs[1][...] = value


def _ffn_kernel(*refs, nf, n_in, n_out, n_first, norm_out):
    x_refs = refs[:n_in]
    g_ref, wg_ref, wu_ref, wd_ref = refs[n_in:n_in + 4]
    pos = n_in + 4
    g2_ref = refs[pos] if norm_out else None
    pos += int(norm_out)
    o_refs = refs[pos:pos + n_out]
    pos += n_out
    hn_ref = refs[pos] if norm_out else None
    pos += int(norm_out)
    h_ref, acc_ref = refs[pos:]
    f = pl.program_id(1)

    @pl.when(f == 0)
    def _():
        x = _group_tile(x_refs, n_first)
        h_ref[...] = (x * _rms_scale(x) * g_ref[...]).astype(h_ref.dtype)
        acc_ref[...] = jnp.zeros_like(acc_ref)

    h = h_ref[...]
    gate = jnp.dot(h, wg_ref[...], preferred_element_type=F32)
    up = jnp.dot(h, wu_ref[...], preferred_element_type=F32)
    act = (gate * jax.nn.sigmoid(gate) * up).astype(wd_ref.dtype)
    acc_ref[...] += jnp.dot(act, wd_ref[...], preferred_element_type=F32)

    @pl.when(f == nf - 1)
    def _():
        y = _group_tile(x_refs, n_first) + 0.5 * acc_ref[...]
        _store_group_tile(o_refs, n_first, y)
        if norm_out:
            hn_ref[...] = (y * _rms_scale(y) * g2_ref[...]).astype(hn_ref.dtype)


def _ffn(xs, gain, wg, wu, wd, layer, tm, split_out=None, next_gain=None):
    xs = tuple(xs) if isinstance(xs, (tuple, list)) else (xs,)
    m = sum(x.shape[0] for x in xs)
    d = xs[0].shape[1]
    n_first = (xs[0].shape[0] if len(xs) == 2 else split_out[0] if split_out else m) // tm
    ff = wg.shape[-1]
    tf = _pick(ff, FF_TILES)
    nf = ff // tf
    norm_out = next_gain is not None
    wbytes = jnp.dtype(wg.dtype).itemsize
    est = (2 * ((len(xs) + (2 if split_out else 1)) * tm * d * 4 + 3 * d * tf * wbytes + tm * d * wbytes)
           + tm * d * (4 + wbytes) + 4 * tm * tf * 4)
    rows = lambda: pl.BlockSpec((tm, d), lambda i, f: (i, 0))
    gain_spec = pl.BlockSpec((None, 1, d), lambda i, f: (layer, 0, 0))
    in_specs = (_group_specs(tm, d, n_first) if len(xs) == 2 else [rows()]) + [
        gain_spec,
        pl.BlockSpec((None, d, tf), lambda i, f: (layer, 0, f)),
        pl.BlockSpec((None, d, tf), lambda i, f: (layer, 0, f)),
        pl.BlockSpec((None, tf, d), lambda i, f: (layer, f, 0)),
    ]
    args = [*xs, gain, wg, wu, wd]
    if split_out:
        out_specs = _group_specs(tm, d, n_first)
        out_shape = [jax.ShapeDtypeStruct((r, d), F32) for r in split_out]
    else:
        out_specs = [rows()]
        out_shape = [jax.ShapeDtypeStruct((m, d), F32)]
    if norm_out:
        in_specs.append(gain_spec)
        args.append(next_gain)
        out_specs.append(rows())
        out_shape.append(jax.ShapeDtypeStruct((m, d), wg.dtype))
    outs = pl.pallas_call(
        functools.partial(_ffn_kernel, nf=nf, n_in=len(xs), n_out=2 if split_out else 1, n_first=n_first,
                          norm_out=norm_out),
        grid=(m // tm, nf),
        in_specs=in_specs,
        out_specs=out_specs,
        out_shape=out_shape,
        scratch_shapes=[pltpu.VMEM((tm, d), wg.dtype), pltpu.VMEM((tm, d), F32)],
        compiler_params=_params(("arbitrary", "arbitrary"), est),
        name="swiglu_half_step",
    )(*args)
    return outs if len(outs) > 1 else outs[0]


SEG_RQ, SEG_RK, SEG_RV, SEG_RG, SEG_DQ, SEG_DK, SEG_DV, SEG_GATES, N_SEG = 0, 1, 2, 3, 4, 5, 6, 7, 11


def _rotate_half_pairs(a, cos2, sin2):
    return a * cos2 + pltpu.roll(a, HALF, 1) * sin2


def _component_rms_norm(a, gain2):
    lo = lax.broadcasted_iota(jnp.int32, a.shape, 1) < HALF
    sq = a * a
    s_all = jnp.sum(sq, axis=-1, keepdims=True)
    s_lo = jnp.sum(jnp.where(lo, sq, 0.0), axis=-1, keepdims=True)
    ms = jnp.where(lo, s_lo, s_all - s_lo) * (1.0 / HALF)
    return a * lax.rsqrt(ms + EPS) * gain2


def _keep_f32(fp_ref, fs_ref, prev_refs, n_first, value):
    i = pl.program_id(1)

    @pl.when(i < n_first)
    def _():
        if prev_refs:
            for l, prev in enumerate(prev_refs):
                fp_ref[l] = prev[...]
            fp_ref[len(prev_refs)] = value
        else:
            fp_ref[...] = value

    @pl.when(i >= n_first)
    def _():
        fs_ref[...] = value


def _segment_kernel(h_ref, w_ref, *refs, kind, scale, nheads, n_first, n_prev):
    *refs, wc_ref = refs

    @pl.when(pl.program_id(1) == 0)
    def _():
        wc_ref[...] = w_ref[...].astype(wc_ref.dtype)

    acc = jnp.dot(h_ref[...], wc_ref[...], preferred_element_type=F32)
    heads = [slice(h * HEAD, (h + 1) * HEAD) for h in range(nheads)]
    if kind == "rotary":
        cos_ref, sin_ref, o_ref = refs
        for sl in heads:
            r = _rotate_half_pairs(acc[:, sl], cos_ref[...], sin_ref[...])
            o_ref[:, sl] = (r if scale == 1.0 else r * scale).astype(o_ref.dtype)
    elif kind == "cast":
        (o_ref,) = refs
        o_ref[...] = acc.astype(o_ref.dtype)
    elif kind == "silu":
        (o_ref,) = refs
        o_ref[...] = (acc * jax.nn.sigmoid(acc)).astype(o_ref.dtype)
    elif kind == "sigmoid":
        (o_ref,) = refs
        o_ref[...] = jax.nn.sigmoid(acc).astype(o_ref.dtype)
    elif kind == "norm":
        gain_ref, o_ref = refs
        for sl in heads:
            o_ref[:, sl] = (_component_rms_norm(acc[:, sl], gain_ref[...]) * scale).astype(o_ref.dtype)
    elif kind == "norm_keep":
        gain_ref, *prev_refs, o_ref, fp_ref, fs_ref = refs
        normed = jnp.concatenate([_component_rms_norm(acc[:, sl], gain_ref[...]) for sl in heads], axis=1)
        _keep_f32(fp_ref, fs_ref, prev_refs, n_first, normed)
        o_ref[...] = normed.astype(o_ref.dtype)
    elif kind == "keep":
        *prev_refs, o_ref, fp_ref, fs_ref = refs
        _keep_f32(fp_ref, fs_ref, prev_refs, n_first, acc)
        o_ref[...] = acc.astype(o_ref.dtype)
    else:
        raise ValueError(kind)
    assert kind not in ("norm_keep", "keep") or len(prev_refs) == n_prev


def _segment(h, w_in, layer, seg0, nseg, tm, kind, *, scale=1.0, tables=(), gain=None, group_rows=None,
             prev_prompt=()):
    m, d = h.shape
    u = d // 2
    nheads = u // HEAD
    n_first = group_rows[0] // tm if group_rows else 0
    n_prev = len(prev_prompt)
    in_specs = [pl.BlockSpec((tm, d), lambda s, i: (i, 0)),
                pl.BlockSpec((None, d, u), lambda s, i: (layer, 0, seg0 + s))]
    args = [h, w_in]
    for tab in tables:
        in_specs.append(pl.BlockSpec((tm, HEAD), lambda s, i: (i, 0)))
        args.append(tab)
    if gain is not None:
        in_specs.append(pl.BlockSpec((None, 1, HEAD), lambda s, i: (layer, 0, 0)))
        args.append(gain)
    out_specs = [pl.BlockSpec((tm, u), lambda s, i: (i, s))]
    out_shape = [jax.ShapeDtypeStruct((m, nseg * u), h.dtype)]
    if group_rows:
        first, second = _group_specs(tm, u, n_first, row_axis=1)
        in_specs += [first] * n_prev
        args += list(prev_prompt)
        if n_prev:
            first = pl.BlockSpec((n_prev + 1, tm, u), lambda s, i: (0, jnp.minimum(i, n_first - 1), 0))
        out_specs += [first, second]
        out_shape += [jax.ShapeDtypeStruct(((n_prev + 1, group_rows[0], u) if n_prev else (group_rows[0], u)), F32),
                      jax.ShapeDtypeStruct((group_rows[1], u), F32)]
    hb = jnp.dtype(h.dtype).itemsize
    est = (2 * (tm * d * hb + d * u * 4 + tm * u * hb + (2 * n_prev + 2) * tm * u * 4 + 2 * tm * HEAD * 4)
           + d * u * hb + 4 * tm * u * 4)
    outs = pl.pallas_call(
        functools.partial(_segment_kernel, kind=kind, scale=scale, nheads=nheads, n_first=n_first, n_prev=n_prev),
        grid=(nseg, m // tm),
        in_specs=in_specs,
        out_specs=out_specs,
        out_shape=out_shape,
        scratch_shapes=[pltpu.VMEM((d, u), h.dtype)],
        compiler_params=_params(("arbitrary", "arbitrary"), est),
        name="input_projection_" + kind,
    )(*args)
    return outs if len(outs) > 1 else outs[0]


def _retention_kernel(q_ref, k_ref, v_ref, g_ref, s0_ref, d_ref, wq_ref, we_ref, dec_ref, o_ref, sout_ref, st_ref,
                      *, nheads, nblk):
    t = pl.program_id(1)

    @pl.when(t == 0)
    def _():
        st_ref[...] = s0_ref[...]

    lanes = [slice(h * HEAD, (h + 1) * HEAD) for h in range(nheads)]
    scores, inter = [], []
    for h, sl in enumerate(lanes):
        q, k, v = q_ref[:, sl], k_ref[:, sl], v_ref[:, sl]
        state = st_ref[h]
        scores.append(lax.dot_general(q, k, (((1,), (1,)), ((), ())), preferred_element_type=F32))
        inter.append(jnp.dot(q, state.astype(q.dtype), preferred_element_type=F32))
        kw = (k.astype(F32) * we_ref[:, sl]).astype(k.dtype)
        kv = lax.dot_general(kw, v, (((0,), (0,)), ((), ())), preferred_element_type=F32)
        st_ref[h] = state * dec_ref[h:h + 1, :] + kv
    for h, sl in enumerate(lanes):
        v = v_ref[:, sl]
        s = scores[h] * d_ref[h]
        o = jnp.dot(s.astype(v.dtype), v, preferred_element_type=F32) + wq_ref[:, sl] * inter[h]
        r = o * _rms_scale(o)
        o_ref[:, sl] = (r * g_ref[:, sl].astype(F32)).astype(o_ref.dtype)

    @pl.when(t == nblk - 1)
    def _():
        sout_ref[...] = st_ref[...]


def _retention_tables(t, nheads):
    log_g = jnp.log(1.0 - 2.0 ** (-5.0 - jnp.arange(nheads, dtype=F32)))
    idx = jnp.arange(t, dtype=F32)
    dist = jnp.abs(idx[:, None] - idx[None, :])
    ci = np.arange(t) // CHUNK
    visible = jnp.asarray(ci[None, :] <= ci[:, None])
    dmat = jnp.where(visible[None], jnp.exp(log_g[:, None, None] * dist[None]), 0.0)
    wq = jnp.exp(log_g[None, :] * (idx + 1.0)[:, None])
    we = jnp.exp(log_g[None, :] * (t - 1.0 - idx)[:, None])
    dec = jnp.exp(log_g * t)
    expand = lambda a: jnp.repeat(a, HEAD, axis=1)
    return dmat, expand(wq), expand(we), jnp.broadcast_to(dec[:, None], (nheads, HEAD))


def _retention(q, k, v, g, s0, t, nbatch, nblk, row_block0):
    nheads = s0.shape[1]
    u = nheads * HEAD
    dmat, wq, we, dec = _retention_tables(t, nheads)
    rows = pl.BlockSpec((t, u), lambda b, i: (row_block0 + b * nblk + i, 0))
    whole = lambda a: pl.BlockSpec(a.shape, lambda b, i: (0,) * a.ndim)
    state_spec = pl.BlockSpec((None, nheads, HEAD, HEAD), lambda b, i: (b, 0, 0, 0))
    pbytes = jnp.dtype(q.dtype).itemsize
    est = (2 * (5 * t * u * pbytes + 2 * nheads * HEAD * HEAD * 4 + nheads * t * t * 4 + 2 * t * u * 4)
           + nheads * HEAD * HEAD * 4 + 6 * t * max(t, HEAD) * 4)
    return pl.pallas_call(
        functools.partial(_retention_kernel, nheads=nheads, nblk=nblk),
        grid=(nbatch, nblk),
        in_specs=[rows, rows, rows, rows, state_spec, whole(dmat), whole(wq), whole(we), whole(dec)],
        out_specs=[
            pl.BlockSpec((t, u), lambda b, i: (b * nblk + i, 0)),
            state_spec,
        ],
        out_shape=[
            jax.ShapeDtypeStruct((nbatch * nblk * t, u), q.dtype),
            jax.ShapeDtypeStruct((nbatch, nheads, HEAD, HEAD), F32),
        ],
        scratch_shapes=[pltpu.VMEM((nheads, HEAD, HEAD), F32)],
        compiler_params=_params(("parallel", "arbitrary"), est),
        name="retention",
    )(q, k, v, g, s0, dmat, wq, we, dec)


def _bucket_thresholds():
    nb = N_BUCKETS // 2
    me = nb // 2
    out = []
    for k in range(1, nb - me):
        n = me
        while n ** (nb - me) * me ** k < me ** (nb - me) * MAX_DISTANCE ** k:
            n += 1
        out.append(n)
    return out


def _t5_bucket_np(rel):
    nb = N_BUCKETS // 2
    me = nb // 2
    n = np.abs(rel)
    large = np.full(rel.shape, me, np.int64)
    for thr in _bucket_thresholds():
        large += (n >= thr)
    large = np.minimum(large, nb - 1)
    return (np.where(rel > 0, nb, 0) + np.where(n < me, n, large)).astype(np.int32)


def _bias_kernel(rb_ref, idx_ref, mask_ref, o_ref):
    h = pl.program_id(0)
    idx = idx_ref[...]
    out = mask_ref[...]
    for b in range(N_BUCKETS):
        out = out + jnp.where(idx == b, rb_ref[b, h], 0.0)
    o_ref[...] = out


def _bias_table(rel_bias, qpos, kpos):
    nheads = rel_bias.shape[1]
    rel = kpos[None, :] - qpos[:, None]
    idx = jnp.asarray(_t5_bucket_np(rel))
    mask = jnp.asarray(np.where((kpos[None, :] // CHUNK) <= (qpos[:, None] // CHUNK), 0.0, MASK_VALUE)
                       .astype(np.float32))
    nq, nk = rel.shape
    return pl.pallas_call(
        _bias_kernel,
        grid=(nheads,),
        in_specs=[
            pl.BlockSpec(memory_space=pltpu.SMEM),
            pl.BlockSpec((nq, nk), lambda h: (0, 0)),
            pl.BlockSpec((nq, nk), lambda h: (0, 0)),
        ],
        out_specs=pl.BlockSpec((None, nq, nk), lambda h: (h, 0, 0)),
        out_shape=jax.ShapeDtypeStruct((nheads, nq, nk), F32),
        compiler_params=_params(("arbitrary",), 6 * nq * nk * 4),
        name="relative_bias_table",
    )(rel_bias, idx, mask)


def _stack_components(q):
    lo = lax.broadcasted_iota(jnp.int32, q.shape, 1) < HALF
    zero = jnp.zeros_like(q)
    return jnp.concatenate([jnp.where(lo, q, zero), jnp.where(lo, zero, q)], axis=0)


def _biased_scores(qs, k, bias):
    s = lax.dot_general(qs, k, (((1,), (1,)), ((), ())), preferred_element_type=F32)
    if bias.ndim == 2:
        t = bias.shape[0]
        return (s.reshape(2, t, s.shape[-1]) + bias[None]).reshape(s.shape), 0.0
    return s, bias


def _softmax_step(qs, k, v, bias, stats, g):
    m_ref, l_ref, acc_ref = stats
    s, c = _biased_scores(qs, k, bias)
    m_prev = m_ref[g]
    m_new = jnp.maximum(m_prev, jnp.max(s, axis=-1, keepdims=True) + c)
    alpha = jnp.exp(m_prev - m_new)
    p = jnp.exp(s - (m_new - c))
    l_ref[g] = alpha * l_ref[g] + jnp.sum(p, axis=-1, keepdims=True)
    acc_ref[g] = alpha * acc_ref[g] + jnp.dot(p.astype(v.dtype), v, preferred_element_type=F32)
    m_ref[g] = m_new


def _softmax_init(stats):
    m_ref, l_ref, acc_ref = stats
    m_ref[...] = jnp.full_like(m_ref, MASK_VALUE)
    l_ref[...] = jnp.zeros_like(l_ref)
    acc_ref[...] = jnp.zeros_like(acc_ref)


def _softmax_scratch(groups, t):
    return [pltpu.VMEM((groups, 2 * t, 1), F32), pltpu.VMEM((groups, 2 * t, 1), F32),
            pltpu.VMEM((groups, 2 * t, HEAD), F32)]


def _lambda_value(lam_ref, lam_init):
    a = lam_ref[...]
    e1 = jnp.exp(jnp.sum(a[0:1] * a[1:2], axis=-1, keepdims=True))
    e2 = jnp.exp(jnp.sum(a[2:3] * a[3:4], axis=-1, keepdims=True))
    return e1 - e2 + lam_init


def _diff_finish(t, lam, lam_init, subln, stats, g):
    _, l_ref, acc_ref = stats
    acc = acc_ref[g]
    l = l_ref[g]
    o = acc[:t] / l[:t] - lam * (acc[t:] / l[t:])
    return o * _rms_scale(o) * subln * (1.0 - lam_init)


def _attn_prompt_kernel(rb_ref, q_ref, k_ref, v_ref, bias_ref, lam_ref, sub_ref, o_ref, *stats,
                        t, groups, lam_init, far_bucket):
    hp = pl.program_id(1)
    qi = pl.program_id(2)
    m_ref, l_ref, acc_ref = stats
    lanes = [slice(g * HEAD, (g + 1) * HEAD) for g in range(groups)]
    qs = [_stack_components(q_ref[:, sl]) for sl in lanes]
    far_bias = [rb_ref[far_bucket, hp * groups + g] for g in range(groups)]
    _softmax_init(stats)

    def sweep(step):
        def run(j, bias_of):
            step(pl.ds(pl.multiple_of(j * t, t), t), [bias_of(g) for g in range(groups)])

        def far_step(j, carry):
            run(j, lambda g: far_bias[g])
            return carry

        lax.fori_loop(0, jnp.maximum(qi - 1, 0), far_step, 0)

        @pl.when(qi > 0)
        def _():
            run(qi - 1, lambda g: bias_ref[g, :, :t])

        run(qi, lambda g: bias_ref[g, :, t:])

    def max_step(rows, biases):
        scores = [_biased_scores(qs[g], k_ref[rows, lanes[g]], biases[g]) for g in range(groups)]
        for g, (s, c) in enumerate(scores):
            m_ref[g] = jnp.maximum(m_ref[g], jnp.max(s, axis=-1, keepdims=True) + c)

    def acc_step(rows, biases):
        scores = [_biased_scores(qs[g], k_ref[rows, lanes[g]], biases[g]) for g in range(groups)]
        probs = [jnp.exp(s - (m_ref[g] - c)) for g, (s, c) in enumerate(scores)]
        for g, p in enumerate(probs):
            l_ref[g] += jnp.sum(p, axis=-1, keepdims=True)
            v = v_ref[rows, lanes[g]]
            acc_ref[g] += jnp.dot(p.astype(v.dtype), v, preferred_element_type=F32)

    sweep(max_step)
    sweep(acc_step)

    lam = _lambda_value(lam_ref, lam_init)
    for g, sl in enumerate(lanes):
        o_ref[:, sl] = _diff_finish(t, lam, lam_init, sub_ref[...], stats, g).astype(o_ref.dtype)


def _attn_prompt(q, k, v, rel_bias, lam_params, subln2, layer, nbatch, seq, nheads, lam_init):
    u = nheads * HEAD
    t = _pick(seq, ATTN_BLOCKS)
    groups = _pick(nheads, HEADS_PER_STEP)
    assert t % CHUNK == 0 and t + 1 >= _bucket_thresholds()[-1]
    nq = seq // t
    w = groups * HEAD
    r = np.arange(t)
    bias = _bias_table(rel_bias, r + t, np.arange(2 * t))
    pbytes = jnp.dtype(q.dtype).itemsize
    est = (2 * (2 * t * w * pbytes + 2 * seq * w * pbytes + groups * 2 * t * t * 4)
           + groups * (2 * t * (HEAD + 2 * 128) * 4 + 6 * 2 * t * t * 4))
    return pl.pallas_call(
        functools.partial(_attn_prompt_kernel, t=t, groups=groups, lam_init=lam_init,
                          far_bucket=N_BUCKETS // 2 - 1),
        grid=(nbatch, nheads // groups, nq),
        in_specs=[
            pl.BlockSpec(memory_space=pltpu.SMEM),
            pl.BlockSpec((t, w), lambda b, h, i: (b * nq + i, h)),
            pl.BlockSpec((seq, w), lambda b, h, i: (b, h)),
            pl.BlockSpec((seq, w), lambda b, h, i: (b, h)),
            pl.BlockSpec((groups, t, 2 * t), lambda b, h, i: (h, 0, 0)),
            pl.BlockSpec((None, 4, HALF), lambda b, h, i: (layer, 0, 0)),
            pl.BlockSpec((None, 1, HEAD), lambda b, h, i: (layer, 0, 0)),
        ],
        out_specs=pl.BlockSpec((t, w), lambda b, h, i: (b * nq + i, h)),
        out_shape=jax.ShapeDtypeStruct((nbatch * seq, u), q.dtype),
        scratch_shapes=_softmax_scratch(groups, t),
        compiler_params=_params(("parallel", "parallel", "arbitrary"), est),
        name="diff_attention_prompt",
    )(rel_bias, q, k, v, bias, lam_params, subln2)


def _attn_sample_kernel(q_ref, kn_ref, vn_ref, kc_ref, vc_ref, bc_ref, bn_ref, lam_ref, sub_ref,
                        o_ref, *stats, t, tk, ncache, nheads, lam_init):
    j = pl.program_id(1)
    lanes = [slice(h * HEAD, (h + 1) * HEAD) for h in range(nheads)]

    @pl.when(j == 0)
    def _():
        _softmax_init(stats)

    qs = [_stack_components(q_ref[:, sl]) for sl in lanes]
    head_rows = lambda ref, h: ref[pl.ds(h, tk, stride=nheads), :].astype(qs[0].dtype)
    for h in range(nheads):
        _softmax_step(qs[h], head_rows(kc_ref, h), head_rows(vc_ref, h), bc_ref[h], stats, h)

    @pl.when(j == ncache - 1)
    def _():
        lam = _lambda_value(lam_ref, lam_init)
        for h, sl in enumerate(lanes):
            _softmax_step(qs[h], kn_ref[:, sl], vn_ref[:, sl], bn_ref[h], stats, h)
            o_ref[:, sl] = _diff_finish(t, lam, lam_init, sub_ref[...], stats, h).astype(o_ref.dtype)


def _attn_sample(q, k, v, cache_k, cache_v, rel_bias, lam_params, subln2, layer, nbatch, t, past,
                 row0, nheads, lam_init):
    u = nheads * HEAD
    assert row0 % t == 0
    rb0 = row0 // t
    tk = _pick(past, CACHE_TILES)
    qpos = past + np.arange(t)
    ncache = past // tk
    bias_c = _bias_table(rel_bias, qpos, np.arange(past))
    bias_c = bias_c.reshape(nheads, t, ncache, tk).transpose(2, 0, 1, 3)
    bias_n = _bias_table(rel_bias, qpos, qpos)
    pbytes = jnp.dtype(q.dtype).itemsize
    est = (2 * (4 * t * u * pbytes + 2 * tk * nheads * HEAD * 4 + nheads * t * (tk + t) * 4)
           + nheads * (2 * t * (HEAD + 2 * 128) * 4 + 6 * 2 * t * tk * 4))
    rows = pl.BlockSpec((t, u), lambda b, j: (rb0 + b, 0))
    cache = pl.BlockSpec((None, tk * nheads, HEAD), lambda b, j: (layer, b * ncache + j, 0))
    return pl.pallas_call(
        functools.partial(_attn_sample_kernel, t=t, tk=tk, ncache=ncache, nheads=nheads, lam_init=lam_init),
        grid=(nbatch, ncache),
        in_specs=[
            rows, rows, rows, cache, cache,
            pl.BlockSpec((None, nheads, t, tk), lambda b, j: (j, 0, 0, 0)),
            pl.BlockSpec((nheads, t, t), lambda b, j: (0, 0, 0)),
            pl.BlockSpec((None, 4, HALF), lambda b, j: (layer, 0, 0)),
            pl.BlockSpec((None, 1, HEAD), lambda b, j: (layer, 0, 0)),
        ],
        out_specs=pl.BlockSpec((t, u), lambda b, j: (b, 0)),
        out_shape=jax.ShapeDtypeStruct((nbatch * t, u), q.dtype),
        scratch_shapes=_softmax_scratch(nheads, t),
        compiler_params=_params(("parallel", "arbitrary"), est),
        name="diff_attention_sample",
    )(q, k, v, cache_k, cache_v, bias_c, bias_n, lam_params, subln2)


def _merge_kernel(ap_ref, as_ref, bp_ref, bs_ref, ga0_ref, ga1_ref, gb0_ref, gb1_ref, wa_ref, wb_ref, o_ref,
                  *, u, n_first):
    a = _group_tile((ap_ref, as_ref), n_first)
    b = _group_tile((bp_ref, bs_ref), n_first)
    ya = jnp.dot(a, wa_ref[...], preferred_element_type=F32)
    yb = jnp.dot(b, wb_ref[...], preferred_element_type=F32)
    for c, (ga, gb) in enumerate(((ga0_ref, gb0_ref), (ga1_ref, gb1_ref))):
        sl = slice(c * u, (c + 1) * u)
        o_ref[:, sl] = (ga[...].astype(F32) * ya[:, sl] + gb[...].astype(F32) * yb[:, sl]).astype(o_ref.dtype)


def _merge(ret_outs, dif_outs, gates, w_ret_up, w_dif_up, layer, tm):
    m = gates.shape[0]
    u = ret_outs[0].shape[1]
    d = 2 * u
    n_first = ret_outs[0].shape[0] // tm
    gate = lambda c: pl.BlockSpec((tm, u), lambda i: (i, c))
    rows = _group_specs(tm, u, n_first)
    wspec = pl.BlockSpec((None, u, d), lambda i: (layer, 0, 0))
    pbytes = jnp.dtype(gates.dtype).itemsize
    est = 2 * (8 * tm * u * pbytes + 2 * u * d * pbytes + tm * d * pbytes) + 3 * tm * d * 4
    return pl.pallas_call(
        functools.partial(_merge_kernel, u=u, n_first=n_first),
        grid=(m // tm,),
        in_specs=rows + rows + [gate(0), gate(1), gate(2), gate(3), wspec, wspec],
        out_specs=pl.BlockSpec((tm, d), lambda i: (i, 0)),
        out_shape=jax.ShapeDtypeStruct((m, d), gates.dtype),
        compiler_params=_params(("arbitrary",), est),
        name="gated_merge",
    )(*ret_outs, *dif_outs, gates, gates, gates, gates, w_ret_up, w_dif_up)


def _out_proj_kernel(x_ref, a_ref, w_ref, o_ref):
    o_ref[...] = x_ref[...] + jnp.dot(a_ref[...], w_ref[...], preferred_element_type=F32)


def _out_proj(x, merged, w_out, layer, tm):
    m, d = x.shape
    wbytes = jnp.dtype(w_out.dtype).itemsize
    est = 2 * (2 * tm * d * 4 + tm * d * wbytes + d * d * wbytes) + tm * d * 4
    return pl.pallas_call(
        _out_proj_kernel,
        grid=(m // tm,),
        in_specs=[
            pl.BlockSpec((tm, d), lambda i: (i, 0)),
            pl.BlockSpec((tm, d), lambda i: (i, 0)),
            pl.BlockSpec((None, d, d), lambda i: (layer, 0, 0)),
        ],
        out_specs=pl.BlockSpec((tm, d), lambda i: (i, 0)),
        out_shape=jax.ShapeDtypeStruct((m, d), F32),
        compiler_params=_params(("parallel",), est),
        name="output_projection",
    )(x, merged, w_out)


def _rotary_tables(pos):
    inv = ROPE_BASE ** (-jnp.arange(HALF, dtype=F32) / HALF)
    ang = pos.astype(F32)[:, None] * inv[None, :]
    cos, sin = jnp.cos(ang), jnp.sin(ang)
    return jnp.concatenate([cos, cos], axis=-1), jnp.concatenate([-sin, sin], axis=-1)


def kernel(x_prompt, x_sample, cache_diff_k, cache_diff_v, state_ret, ffn1_norm, ffn1_gate, ffn1_up, ffn1_down, mix_norm, w_in, q_norm, k_norm, lambda_q1, lambda_k1, lambda_q2, lambda_k2, subln, w_ret_up, w_dif_up, w_out, ffn2_norm, ffn2_gate, ffn2_up, ffn2_down, rel_bias):
    nb, seq, d = x_prompt.shape
    db, dseq, _ = x_sample.shape
    depth, _, past, nh_d, _ = cache_diff_k.shape
    nh_r = state_ret.shape[2]
    u = d // 2
    assert nh_r * HEAD == u and nh_d * HEAD == u and dseq == CHUNK and seq % CHUNK == 0
    assert w_in.shape[-1] == N_SEG * u
    mp, ms = nb * seq, db * dseq

    cast = lambda w: w.astype(MXU_DTYPE)
    wg1, wu1, wd1 = cast(ffn1_gate), cast(ffn1_up), cast(ffn1_down)
    wg2, wu2, wd2 = cast(ffn2_gate), cast(ffn2_up), cast(ffn2_down)
    w_ret_c, w_dif_c, w_out_c = cast(w_ret_up), cast(w_dif_up), cast(w_out)
    row3 = lambda g: g.reshape(depth, 1, g.shape[-1])
    n1, nmix, n2 = row3(ffn1_norm), row3(mix_norm), row3(ffn2_norm)
    qn2 = row3(jnp.concatenate([q_norm, q_norm], axis=-1))
    kn2 = row3(jnp.concatenate([k_norm, k_norm], axis=-1))
    subln2 = row3(subln)
    lam_params = jnp.stack([lambda_q1, lambda_k1, lambda_q2, lambda_k2], axis=1)
    cache_k = cache_diff_k.reshape(depth, db * past * nh_d, HEAD)
    cache_v = cache_diff_v.reshape(depth, db * past * nh_d, HEAD)

    pos = jnp.concatenate([jnp.tile(jnp.arange(seq, dtype=jnp.int32), nb),
                           past + jnp.tile(jnp.arange(dseq, dtype=jnp.int32), db)])
    rot = _rotary_tables(pos)

    tm = _pick(math.gcd(mp, ms), ROW_TILES)
    tm_proj = _pick(mp + ms, PROJ_ROW_TILES)
    groups = (mp, ms)
    zero_state = jnp.zeros((nb, nh_r, HEAD, HEAD), F32)
    t_ret = _pick(seq, RET_BLOCKS)

    x = (x_prompt.reshape(mp, d), x_sample.reshape(ms, d))
    kps, kss, vps, vss, states_p, states_s = [], [], [], [], [], []
    for l in range(depth):
        lam_init = 0.8 - 0.6 * math.exp(-0.3 * l)
        x, h = _ffn(x, n1, wg1, wu1, wd1, l, tm, next_gain=nmix)

        last = l == depth - 1
        seg = functools.partial(_segment, h, w_in, l)
        rq = seg(SEG_RQ, 1, tm_proj, "rotary", tables=rot)
        rk = seg(SEG_RK, 1, tm_proj, "rotary", tables=rot, scale=HEAD ** -0.5)
        rv = seg(SEG_RV, 1, tm_proj, "cast")
        rg = seg(SEG_RG, 1, tm_proj, "silu")
        dq = seg(SEG_DQ, 1, tm_proj, "norm", gain=qn2, scale=HALF ** -0.5)
        dk, k_p, k_s = seg(SEG_DK, 1, tm, "norm_keep", gain=kn2, group_rows=groups,
                           prev_prompt=kps if last else ())
        dv, v_p, v_s = seg(SEG_DV, 1, tm, "keep", group_rows=groups, prev_prompt=vps if last else ())
        gates = seg(SEG_GATES, N_SEG - SEG_GATES, tm_proj, "sigmoid")

        ret_p, st_p = _retention(rq, rk, rv, rg, zero_state, t_ret, nb, seq // t_ret, 0)
        ret_s, st_s = _retention(rq, rk, rv, rg, state_ret[l].astype(F32), dseq, db, 1, mp // dseq)

        dif_p = _attn_prompt(dq, dk, dv, rel_bias, lam_params, subln2, l, nb, seq, nh_d, lam_init)
        dif_s = _attn_sample(dq, dk, dv, cache_k, cache_v, rel_bias, lam_params, subln2, l, db, dseq, past,
                             mp, nh_d, lam_init)

        merged = _merge((ret_p, ret_s), (dif_p, dif_s), gates, w_ret_c, w_dif_c, l, tm)
        x = _out_proj(x, merged, w_out_c, l, tm)
        x = _ffn(x, n2, wg2, wu2, wd2, l, tm, split_out=groups if l == depth - 1 else None)

        for acc, val in ((kps, k_p), (kss, k_s), (vps, v_p), (vss, v_s), (states_p, st_p), (states_s, st_s)):
            acc.append(val)

    y_p, y_s = x
    kv_p = lambda parts: parts[-1].reshape(depth, nb, seq, nh_d, HEAD)
    kv_s = lambda parts: jnp.stack(parts).reshape(depth, db, dseq, nh_d, HEAD)
    return (y_p.reshape(nb, seq, d), y_s.reshape(db, dseq, d),
            kv_p(kps).astype(cache_diff_k.dtype), kv_p(vps).astype(cache_diff_v.dtype),
            jnp.stack(states_p).astype(state_ret.dtype),
            kv_s(kss).astype(cache_diff_k.dtype), kv_s(vss).astype(cache_diff_v.dtype),
            jnp.stack(states_s).astype(state_ret.dtype))
```

```python
import functools
import math

import numpy as np
import jax
import jax.numpy as jnp
from jax import lax
from jax.experimental import pallas as pl
from jax.experimental.pallas import tpu as pltpu

F32 = jnp.float32
MXU_DTYPE = jnp.bfloat16

CHUNK = 64
HEAD = 128
HALF = HEAD // 2
ROPE_BASE = 10000.0
N_BUCKETS = 32
MAX_DISTANCE = 128
EPS = 1e-6
MASK_VALUE = -1e30

VMEM_LIMIT_CAP = 60 * 1024 * 1024
MIB = 1024 * 1024

ROW_TILES = (512, 256, 128, 64)
PROJ_ROW_TILES = (1536, 1024, 768, 512, 256, 128, 64)
FF_TILES = (512, 256, 128)
FF_CHUNKS_PER_STEP = 2
ATTN_BLOCKS = (512, 256, 128)
HEADS_PER_STEP = (4, 2, 1)
RET_BLOCKS = (256, 128, 64)
CACHE_TILES = (2048, 1024, 512, 256, 128, 64)


def _pick(n, prefs):
    for p in prefs:
        if n % p == 0:
            return p
    raise ValueError(f"no tile in {prefs} divides {n}")


def _params(semantics, est_bytes):
    limit = int(min(max(est_bytes + 8 * MIB, 32 * MIB), VMEM_LIMIT_CAP))
    return pltpu.CompilerParams(dimension_semantics=semantics, vmem_limit_bytes=limit)


def _rms_scale(x):
    return lax.rsqrt(jnp.mean(x * x, axis=-1, keepdims=True) + EPS)


def _group_specs(tm, width, n_first, row_axis=0):
    first = pl.BlockSpec((tm, width), lambda *g: (jnp.minimum(g[row_axis], n_first - 1), 0))
    second = pl.BlockSpec((tm, width), lambda *g: (jnp.maximum(g[row_axis] - n_first, 0), 0),
                          pipeline_mode=pl.Buffered(1))
    return [first, second]


def _group_tile(refs, n_first, row_axis=0):
    if len(refs) == 1:
        return refs[0][...]
    return jnp.where(pl.program_id(row_axis) < n_first, refs[0][...], refs[1][...])


def _store_group_tile(refs, n_first, value, row_axis=0):
    if len(refs) == 1:
        refs[0][...] = value
        return
    i = pl.program_id(row_axis)

    @pl.when(i < n_first)
    def _():
        refs[0][...] = value

    @pl.when(i >= n_first)
    def _():
        refs[1][...] = value


def _ffn_kernel(*refs, nf, per_step, n_in, n_out, n_first, norm_out):
    x_refs = refs[:n_in]
    g_ref = refs[n_in]
    pos = n_in + 1
    w_refs = [refs[pos + 3 * c:pos + 3 * c + 3] for c in range(per_step)]
    pos += 3 * per_step
    g2_ref = refs[pos] if norm_out else None
    pos += int(norm_out)
    o_refs = refs[pos:pos + n_out]
    pos += n_out
    hn_ref = refs[pos] if norm_out else None
    pos += int(norm_out)
    h_ref, acc_ref = refs[pos:]
    f = pl.program_id(1)
    nsteps = -(-nf // per_step)

    @pl.when(f == 0)
    def _():
        x = _group_tile(x_refs, n_first)
        h_ref[...] = (x * _rms_scale(x) * g_ref[...]).astype(h_ref.dtype)
        acc_ref[...] = jnp.zeros_like(acc_ref)

    def chunks(slots):
        h = h_ref[...]
        pre = [(jnp.dot(h, wg[...], preferred_element_type=F32), jnp.dot(h, wu[...], preferred_element_type=F32))
               for wg, wu, _ in slots]
        total = acc_ref[...]
        for (gate, up), (_, _, wd) in zip(pre, slots):
            act = (gate * jax.nn.sigmoid(gate) * up).astype(wd.dtype)
            total = total + jnp.dot(act, wd[...], preferred_element_type=F32)
        acc_ref[...] = total

    tail = nf - (nsteps - 1) * per_step
    if tail == per_step:
        chunks(w_refs)
    else:
        @pl.when(f < nsteps - 1)
        def _():
            chunks(w_refs)

        @pl.when(f == nsteps - 1)
        def _():
            chunks(w_refs[:tail])

    @pl.when(f == nsteps - 1)
    def _():
        y = _group_tile(x_refs, n_first) + 0.5 * acc_ref[...]
        _store_group_tile(o_refs, n_first, y)
        if norm_out:
            hn_ref[...] = (y * _rms_scale(y) * g2_ref[...]).astype(hn_ref.dtype)


def _ffn(xs, gain, wg, wu, wd, layer, tm, split_out=None, next_gain=None):
    xs = tuple(xs) if isinstance(xs, (tuple, list)) else (xs,)
    m = sum(x.shape[0] for x in xs)
    d = xs[0].shape[1]
    n_first = (xs[0].shape[0] if len(xs) == 2 else split_out[0] if split_out else m) // tm
    ff = wg.shape[-1]
    tf = _pick(ff, FF_TILES)
    nf = ff // tf
    norm_out = next_gain is not None
    wbytes = jnp.dtype(wg.dtype).itemsize
    row_buffers = (3 if len(xs) == 2 else 2) + (3 if split_out else 2)
    vmem_need = lambda chunks: (
        row_buffers * tm * d * 4 + 2 * (3 * chunks * d * tf * wbytes + int(norm_out) * tm * d * wbytes)
        + tm * d * (4 + wbytes) + 2 * chunks * tm * tf * 4)
    per_step = min(FF_CHUNKS_PER_STEP, nf)
    while per_step > 1 and vmem_need(per_step) > VMEM_LIMIT_CAP:
        per_step -= 1
    nsteps = -(-nf // per_step)
    est = vmem_need(per_step)
    rows = lambda: pl.BlockSpec((tm, d), lambda i, f: (i, 0))
    gain_spec = pl.BlockSpec((None, 1, d), lambda i, f: (layer, 0, 0))
    in_specs = (_group_specs(tm, d, n_first) if len(xs) == 2 else [rows()]) + [gain_spec]
    args = [*xs, gain]
    for c in range(per_step):
        chunk = lambda f, c=c: jnp.minimum(f * per_step + c, nf - 1)
        in_specs += [pl.BlockSpec((None, d, tf), lambda i, f, chunk=chunk: (layer, 0, chunk(f))),
                     pl.BlockSpec((None, d, tf), lambda i, f, chunk=chunk: (layer, 0, chunk(f))),
                     pl.BlockSpec((None, tf, d), lambda i, f, chunk=chunk: (layer, chunk(f), 0))]
        args += [wg, wu, wd]
    if split_out:
        out_specs = _group_specs(tm, d, n_first)
        out_shape = [jax.ShapeDtypeStruct((r, d), F32) for r in split_out]
    else:
        out_specs = [rows()]
        out_shape = [jax.ShapeDtypeStruct((m, d), F32)]
    if norm_out:
        in_specs.append(gain_spec)
        args.append(next_gain)
        out_specs.append(rows())
        out_shape.append(jax.ShapeDtypeStruct((m, d), wg.dtype))
    outs = pl.pallas_call(
        functools.partial(_ffn_kernel, nf=nf, per_step=per_step, n_in=len(xs), n_out=2 if split_out else 1,
                          n_first=n_first, norm_out=norm_out),
        grid=(m // tm, nsteps),
        in_specs=in_specs,
        out_specs=out_specs,
        out_shape=out_shape,
        scratch_shapes=[pltpu.VMEM((tm, d), wg.dtype), pltpu.VMEM((tm, d), F32)],
        compiler_params=_params(("arbitrary", "arbitrary"), est),
        name="swiglu_half_step",
    )(*args)
    return outs if len(outs) > 1 else outs[0]


SEG_RQ, SEG_RK, SEG_RV, SEG_RG, SEG_DQ, SEG_DK, SEG_DV, SEG_GATES, N_SEG = 0, 1, 2, 3, 4, 5, 6, 7, 11


def _rotate_half_pairs(a, cos2, sin2):
    return a * cos2 + pltpu.roll(a, HALF, 1) * sin2


def _component_rms_norm(a, gain2):
    lo = lax.broadcasted_iota(jnp.int32, a.shape, 1) < HALF
    sq = a * a
    s_all = jnp.sum(sq, axis=-1, keepdims=True)
    s_lo = jnp.sum(jnp.where(lo, sq, 0.0), axis=-1, keepdims=True)
    ms = jnp.where(lo, s_lo, s_all - s_lo) * (1.0 / HALF)
    return a * lax.rsqrt(ms + EPS) * gain2


def _keep_f32(fp_ref, fs_ref, prev_refs, n_first, value):
    i = pl.program_id(1)

    @pl.when(i < n_first)
    def _():
        if prev_refs:
            for l, prev in enumerate(prev_refs):
                fp_ref[l] = prev[...]
            fp_ref[len(prev_refs)] = value
        else:
            fp_ref[...] = value

    @pl.when(i >= n_first)
    def _():
        fs_ref[...] = value


def _segment_kernel(h_ref, w_ref, *refs, kind, scale, nheads, n_first, n_prev):
    *refs, wc_ref = refs

    @pl.when(pl.program_id(1) == 0)
    def _():
        wc_ref[...] = w_ref[...].astype(wc_ref.dtype)

    acc = jnp.dot(h_ref[...], wc_ref[...], preferred_element_type=F32)
    heads = [slice(h * HEAD, (h + 1) * HEAD) for h in range(nheads)]
    if kind == "rotary":
        cos_ref, sin_ref, o_ref = refs
        for sl in heads:
            r = _rotate_half_pairs(acc[:, sl], cos_ref[...], sin_ref[...])
            o_ref[:, sl] = (r if scale == 1.0 else r * scale).astype(o_ref.dtype)
    elif kind == "cast":
        (o_ref,) = refs
        o_ref[...] = acc.astype(o_ref.dtype)
    elif kind == "silu":
        (o_ref,) = refs
        o_ref[...] = (acc * jax.nn.sigmoid(acc)).astype(o_ref.dtype)
    elif kind == "sigmoid":
        (o_ref,) = refs
        o_ref[...] = jax.nn.sigmoid(acc).astype(o_ref.dtype)
    elif kind == "norm":
        gain_ref, o_ref = refs
        for sl in heads:
            o_ref[:, sl] = (_component_rms_norm(acc[:, sl], gain_ref[...]) * scale).astype(o_ref.dtype)
    elif kind == "norm_keep":
        gain_ref, *prev_refs, o_ref, fp_ref, fs_ref = refs
        normed = jnp.concatenate([_component_rms_norm(acc[:, sl], gain_ref[...]) for sl in heads], axis=1)
        _keep_f32(fp_ref, fs_ref, prev_refs, n_first, normed)
        o_ref[...] = normed.astype(o_ref.dtype)
    elif kind == "keep":
        *prev_refs, o_ref, fp_ref, fs_ref = refs
        _keep_f32(fp_ref, fs_ref, prev_refs, n_first, acc)
        o_ref[...] = acc.astype(o_ref.dtype)
    else:
        raise ValueError(kind)
    assert kind not in ("norm_keep", "keep") or len(prev_refs) == n_prev


def _segment(h, w_in, layer, seg0, nseg, tm, kind, *, scale=1.0, tables=(), gain=None, group_rows=None,
             prev_prompt=()):
    m, d = h.shape
    u = d // 2
    nheads = u // HEAD
    n_first = group_rows[0] // tm if group_rows else 0
    n_prev = len(prev_prompt)
    in_specs = [pl.BlockSpec((tm, d), lambda s, i: (i, 0)),
                pl.BlockSpec((None, d, u), lambda s, i: (layer, 0, seg0 + s))]
    args = [h, w_in]
    for tab in tables:
        in_specs.append(pl.BlockSpec((tm, HEAD), lambda s, i: (i, 0)))
        args.append(tab)
    if gain is not None:
        in_specs.append(pl.BlockSpec((None, 1, HEAD), lambda s, i: (layer, 0, 0)))
        args.append(gain)
    out_specs = [pl.BlockSpec((tm, u), lambda s, i: (i, s))]
    out_shape = [jax.ShapeDtypeStruct((m, nseg * u), h.dtype)]
    if group_rows:
        first, second = _group_specs(tm, u, n_first, row_axis=1)
        in_specs += [first] * n_prev
        args += list(prev_prompt)
        if n_prev:
            first = pl.BlockSpec((n_prev + 1, tm, u), lambda s, i: (0, jnp.minimum(i, n_first - 1), 0))
        out_specs += [first, second]
        out_shape += [jax.ShapeDtypeStruct(((n_prev + 1, group_rows[0], u) if n_prev else (group_rows[0], u)), F32),
                      jax.ShapeDtypeStruct((group_rows[1], u), F32)]
    hb = jnp.dtype(h.dtype).itemsize
    est = (2 * (tm * d * hb + d * u * 4 + tm * u * hb + (2 * n_prev + 2) * tm * u * 4 + 2 * tm * HEAD * 4)
           + d * u * hb + 4 * tm * u * 4)
    outs = pl.pallas_call(
        functools.partial(_segment_kernel, kind=kind, scale=scale, nheads=nheads, n_first=n_first, n_prev=n_prev),
        grid=(nseg, m // tm),
        in_specs=in_specs,
        out_specs=out_specs,
        out_shape=out_shape,
        scratch_shapes=[pltpu.VMEM((d, u), h.dtype)],
        compiler_params=_params(("arbitrary", "arbitrary"), est),
        name="input_projection_" + kind,
    )(*args)
    return outs if len(outs) > 1 else outs[0]


def _retention_kernel(q_ref, k_ref, v_ref, g_ref, s0_ref, d_ref, wq_ref, we_ref, dec_ref, o_ref, sout_ref, st_ref,
                      *, nheads, nblk):
    t = pl.program_id(1)

    @pl.when(t == 0)
    def _():
        st_ref[...] = s0_ref[...]

    lanes = [slice(h * HEAD, (h + 1) * HEAD) for h in range(nheads)]
    scores, inter = [], []
    for h, sl in enumerate(lanes):
        q, k, v = q_ref[:, sl], k_ref[:, sl], v_ref[:, sl]
        state = st_ref[h]
        scores.append(lax.dot_general(q, k, (((1,), (1,)), ((), ())), preferred_element_type=F32))
        inter.append(jnp.dot(q, state.astype(q.dtype), preferred_element_type=F32))
        kw = (k.astype(F32) * we_ref[:, sl]).astype(k.dtype)
        kv = lax.dot_general(kw, v, (((0,), (0,)), ((), ())), preferred_element_type=F32)
        st_ref[h] = state * dec_ref[h:h + 1, :] + kv
    for h, sl in enumerate(lanes):
        v = v_ref[:, sl]
        s = scores[h] * d_ref[h]
        o = jnp.dot(s.astype(v.dtype), v, preferred_element_type=F32) + wq_ref[:, sl] * inter[h]
        r = o * _rms_scale(o)
        o_ref[:, sl] = (r * g_ref[:, sl].astype(F32)).astype(o_ref.dtype)

    @pl.when(t == nblk - 1)
    def _():
        sout_ref[...] = st_ref[...]


def _retention_tables(t, nheads):
    log_g = jnp.log(1.0 - 2.0 ** (-5.0 - jnp.arange(nheads, dtype=F32)))
    idx = jnp.arange(t, dtype=F32)
    dist = jnp.abs(idx[:, None] - idx[None, :])
    ci = np.arange(t) // CHUNK
    visible = jnp.asarray(ci[None, :] <= ci[:, None])
    dmat = jnp.where(visible[None], jnp.exp(log_g[:, None, None] * dist[None]), 0.0)
    wq = jnp.exp(log_g[None, :] * (idx + 1.0)[:, None])
    we = jnp.exp(log_g[None, :] * (t - 1.0 - idx)[:, None])
    dec = jnp.exp(log_g * t)
    expand = lambda a: jnp.repeat(a, HEAD, axis=1)
    return dmat, expand(wq), expand(we), jnp.broadcast_to(dec[:, None], (nheads, HEAD))


def _retention(q, k, v, g, s0, t, nbatch, nblk, row_block0):
    nheads = s0.shape[1]
    u = nheads * HEAD
    dmat, wq, we, dec = _retention_tables(t, nheads)
    rows = pl.BlockSpec((t, u), lambda b, i: (row_block0 + b * nblk + i, 0))
    whole = lambda a: pl.BlockSpec(a.shape, lambda b, i: (0,) * a.ndim)
    state_spec = pl.BlockSpec((None, nheads, HEAD, HEAD), lambda b, i: (b, 0, 0, 0))
    pbytes = jnp.dtype(q.dtype).itemsize
    est = (2 * (5 * t * u * pbytes + 2 * nheads * HEAD * HEAD * 4 + nheads * t * t * 4 + 2 * t * u * 4)
           + nheads * HEAD * HEAD * 4 + 6 * t * max(t, HEAD) * 4)
    return pl.pallas_call(
        functools.partial(_retention_kernel, nheads=nheads, nblk=nblk),
        grid=(nbatch, nblk),
        in_specs=[rows, rows, rows, rows, state_spec, whole(dmat), whole(wq), whole(we), whole(dec)],
        out_specs=[
            pl.BlockSpec((t, u), lambda b, i: (b * nblk + i, 0)),
            state_spec,
        ],
        out_shape=[
            jax.ShapeDtypeStruct((nbatch * nblk * t, u), q.dtype),
            jax.ShapeDtypeStruct((nbatch, nheads, HEAD, HEAD), F32),
        ],
        scratch_shapes=[pltpu.VMEM((nheads, HEAD, HEAD), F32)],
        compiler_params=_params(("parallel", "arbitrary"), est),
        name="retention",
    )(q, k, v, g, s0, dmat, wq, we, dec)


def _bucket_thresholds():
    nb = N_BUCKETS // 2
    me = nb // 2
    out = []
    for k in range(1, nb - me):
        n = me
        while n ** (nb - me) * me ** k < me ** (nb - me) * MAX_DISTANCE ** k:
            n += 1
        out.append(n)
    return out


def _t5_bucket_np(rel):
    nb = N_BUCKETS // 2
    me = nb // 2
    n = np.abs(rel)
    large = np.full(rel.shape, me, np.int64)
    for thr in _bucket_thresholds():
        large += (n >= thr)
    large = np.minimum(large, nb - 1)
    return (np.where(rel > 0, nb, 0) + np.where(n < me, n, large)).astype(np.int32)


def _bias_kernel(rb_ref, idx_ref, mask_ref, o_ref):
    h = pl.program_id(0)
    idx = idx_ref[...]
    out = mask_ref[...]
    for b in range(N_BUCKETS):
        out = out + jnp.where(idx == b, rb_ref[b, h], 0.0)
    o_ref[...] = out


def _bias_table(rel_bias, qpos, kpos):
    nheads = rel_bias.shape[1]
    rel = kpos[None, :] - qpos[:, None]
    idx = jnp.asarray(_t5_bucket_np(rel))
    mask = jnp.asarray(np.where((kpos[None, :] // CHUNK) <= (qpos[:, None] // CHUNK), 0.0, MASK_VALUE)
                       .astype(np.float32))
    nq, nk = rel.shape
    return pl.pallas_call(
        _bias_kernel,
        grid=(nheads,),
        in_specs=[
            pl.BlockSpec(memory_space=pltpu.SMEM),
            pl.BlockSpec((nq, nk), lambda h: (0, 0)),
            pl.BlockSpec((nq, nk), lambda h: (0, 0)),
        ],
        out_specs=pl.BlockSpec((None, nq, nk), lambda h: (h, 0, 0)),
        out_shape=jax.ShapeDtypeStruct((nheads, nq, nk), F32),
        compiler_params=_params(("arbitrary",), 6 * nq * nk * 4),
        name="relative_bias_table",
    )(rel_bias, idx, mask)


def _stack_components(q):
    lo = lax.broadcasted_iota(jnp.int32, q.shape, 1) < HALF
    zero = jnp.zeros_like(q)
    return jnp.concatenate([jnp.where(lo, q, zero), jnp.where(lo, zero, q)], axis=0)


def _biased_scores(qs, k, bias):
    s = lax.dot_general(qs, k, (((1,), (1,)), ((), ())), preferred_element_type=F32)
    if bias.ndim == 2:
        t = bias.shape[0]
        return jnp.concatenate([s[:t] + bias, s[t:] + bias], axis=0), 0.0
    return s, bias


def _softmax_step(qs, k, v, bias, stats, g):
    m_ref, l_ref, acc_ref = stats
    s, c = _biased_scores(qs, k, bias)
    m_prev = m_ref[g]
    m_new = jnp.maximum(m_prev, jnp.max(s, axis=-1, keepdims=True) + c)
    alpha = jnp.exp(m_prev - m_new)
    p = jnp.exp(s - (m_new - c))
    l_ref[g] = alpha * l_ref[g] + jnp.sum(p, axis=-1, keepdims=True)
    acc_ref[g] = alpha * acc_ref[g] + jnp.dot(p.astype(v.dtype), v, preferred_element_type=F32)
    m_ref[g] = m_new


def _softmax_init(stats):
    m_ref, l_ref, acc_ref = stats
    m_ref[...] = jnp.full_like(m_ref, MASK_VALUE)
    l_ref[...] = jnp.zeros_like(l_ref)
    acc_ref[...] = jnp.zeros_like(acc_ref)


def _softmax_scratch(groups, t):
    return [pltpu.VMEM((groups, 2 * t, 1), F32), pltpu.VMEM((groups, 2 * t, 1), F32),
            pltpu.VMEM((groups, 2 * t, HEAD), F32)]


def _lambda_value(lam_ref, lam_init):
    a = lam_ref[...]
    e1 = jnp.exp(jnp.sum(a[0:1] * a[1:2], axis=-1, keepdims=True))
    e2 = jnp.exp(jnp.sum(a[2:3] * a[3:4], axis=-1, keepdims=True))
    return e1 - e2 + lam_init


def _diff_finish(t, lam, lam_init, subln, stats, g):
    _, l_ref, acc_ref = stats
    acc = acc_ref[g]
    l = l_ref[g]
    o = acc[:t] / l[:t] - lam * (acc[t:] / l[t:])
    return o * _rms_scale(o) * subln * (1.0 - lam_init)


def _attn_prompt_kernel(rb_ref, q_ref, k_ref, v_ref, bias_ref, lam_ref, sub_ref, o_ref, *stats,
                        t, groups, lam_init, far_bucket):
    hp = pl.program_id(1)
    qi = pl.program_id(2)
    m_ref, l_ref, acc_ref = stats
    lanes = [slice(g * HEAD, (g + 1) * HEAD) for g in range(groups)]
    qs = [_stack_components(q_ref[:, sl]) for sl in lanes]
    far_bias = [rb_ref[far_bucket, hp * groups + g] for g in range(groups)]
    _softmax_init(stats)

    def sweep(step):
        def run(j, bias_of):
            step(pl.ds(pl.multiple_of(j * t, t), t), [bias_of(g) for g in range(groups)])

        def far_step(j, carry):
            run(j, lambda g: far_bias[g])
            return carry

        lax.fori_loop(0, jnp.maximum(qi - 1, 0), far_step, 0)

        @pl.when(qi > 0)
        def _():
            run(qi - 1, lambda g: bias_ref[g, :, :t])

        run(qi, lambda g: bias_ref[g, :, t:])

    def max_step(rows, biases):
        scores = [_biased_scores(qs[g], k_ref[rows, lanes[g]], biases[g]) for g in range(groups)]
        for g, (s, c) in enumerate(scores):
            m_ref[g] = jnp.maximum(m_ref[g], jnp.max(s, axis=-1, keepdims=True) + c)

    def acc_step(rows, biases):
        scores = [_biased_scores(qs[g], k_ref[rows, lanes[g]], biases[g]) for g in range(groups)]
        probs = [jnp.exp(s - (m_ref[g] - c)) for g, (s, c) in enumerate(scores)]
        for g, p in enumerate(probs):
            l_ref[g] += jnp.sum(p, axis=-1, keepdims=True)
            v = v_ref[rows, lanes[g]]
            acc_ref[g] += jnp.dot(p.astype(v.dtype), v, preferred_element_type=F32)

    sweep(max_step)
    sweep(acc_step)

    lam = _lambda_value(lam_ref, lam_init)
    for g, sl in enumerate(lanes):
        o_ref[:, sl] = _diff_finish(t, lam, lam_init, sub_ref[...], stats, g).astype(o_ref.dtype)


def _attn_prompt(q, k, v, rel_bias, lam_params, subln2, layer, nbatch, seq, nheads, lam_init):
    u = nheads * HEAD
    t = _pick(seq, ATTN_BLOCKS)
    groups = _pick(nheads, HEADS_PER_STEP)
    assert t % CHUNK == 0 and t + 1 >= _bucket_thresholds()[-1]
    nq = seq // t
    w = groups * HEAD
    r = np.arange(t)
    bias = _bias_table(rel_bias, r + t, np.arange(2 * t))
    pbytes = jnp.dtype(q.dtype).itemsize
    est = (2 * (2 * t * w * pbytes + 2 * seq * w * pbytes + groups * 2 * t * t * 4)
           + groups * (2 * t * (HEAD + 2 * 128) * 4 + 6 * 2 * t * t * 4))
    return pl.pallas_call(
        functools.partial(_attn_prompt_kernel, t=t, groups=groups, lam_init=lam_init,
                          far_bucket=N_BUCKETS // 2 - 1),
        grid=(nbatch, nheads // groups, nq),
        in_specs=[
            pl.BlockSpec(memory_space=pltpu.SMEM),
            pl.BlockSpec((t, w), lambda b, h, i: (b * nq + i, h)),
            pl.BlockSpec((seq, w), lambda b, h, i: (b, h)),
            pl.BlockSpec((seq, w), lambda b, h, i: (b, h)),
            pl.BlockSpec((groups, t, 2 * t), lambda b, h, i: (h, 0, 0)),
            pl.BlockSpec((None, 4, HALF), lambda b, h, i: (layer, 0, 0)),
            pl.BlockSpec((None, 1, HEAD), lambda b, h, i: (layer, 0, 0)),
        ],
        out_specs=pl.BlockSpec((t, w), lambda b, h, i: (b * nq + i, h)),
        out_shape=jax.ShapeDtypeStruct((nbatch * seq, u), q.dtype),
        scratch_shapes=_softmax_scratch(groups, t),
        compiler_params=_params(("parallel", "parallel", "arbitrary"), est),
        name="diff_attention_prompt",
    )(rel_bias, q, k, v, bias, lam_params, subln2)


def _attn_sample_kernel(q_ref, kn_ref, vn_ref, kc_ref, vc_ref, bc_ref, bn_ref, lam_ref, sub_ref,
                        o_ref, *stats, t, tk, ncache, nheads, lam_init):
    j = pl.program_id(1)
    lanes = [slice(h * HEAD, (h + 1) * HEAD) for h in range(nheads)]

    @pl.when(j == 0)
    def _():
        _softmax_init(stats)

    qs = [_stack_components(q_ref[:, sl]) for sl in lanes]
    head_rows = lambda ref, h: ref[pl.ds(h, tk, stride=nheads), :].astype(qs[0].dtype)
    for h in range(nheads):
        _softmax_step(qs[h], head_rows(kc_ref, h), head_rows(vc_ref, h), bc_ref[h], stats, h)

    @pl.when(j == ncache - 1)
    def _():
        lam = _lambda_value(lam_ref, lam_init)
        for h, sl in enumerate(lanes):
            _softmax_step(qs[h], kn_ref[:, sl], vn_ref[:, sl], bn_ref[h], stats, h)
            o_ref[:, sl] = _diff_finish(t, lam, lam_init, sub_ref[...], stats, h).astype(o_ref.dtype)


def _attn_sample(q, k, v, cache_k, cache_v, rel_bias, lam_params, subln2, layer, nbatch, t, past,
                 row0, nheads, lam_init):
    u = nheads * HEAD
    assert row0 % t == 0
    rb0 = row0 // t
    tk = _pick(past, CACHE_TILES)
    qpos = past + np.arange(t)
    ncache = past // tk
    bias_c = _bias_table(rel_bias, qpos, np.arange(past))
    bias_c = bias_c.reshape(nheads, t, ncache, tk).transpose(2, 0, 1, 3)
    bias_n = _bias_table(rel_bias, qpos, qpos)
    pbytes = jnp.dtype(q.dtype).itemsize
    est = (2 * (4 * t * u * pbytes + 2 * tk * nheads * HEAD * 4 + nheads * t * (tk + t) * 4)
           + nheads * (2 * t * (HEAD + 2 * 128) * 4 + 6 * 2 * t * tk * 4))
    rows = pl.BlockSpec((t, u), lambda b, j: (rb0 + b, 0))
    cache = pl.BlockSpec((None, tk * nheads, HEAD), lambda b, j: (layer, b * ncache + j, 0))
    return pl.pallas_call(
        functools.partial(_attn_sample_kernel, t=t, tk=tk, ncache=ncache, nheads=nheads, lam_init=lam_init),
        grid=(nbatch, ncache),
        in_specs=[
            rows, rows, rows, cache, cache,
            pl.BlockSpec((None, nheads, t, tk), lambda b, j: (j, 0, 0, 0)),
            pl.BlockSpec((nheads, t, t), lambda b, j: (0, 0, 0)),
            pl.BlockSpec((None, 4, HALF), lambda b, j: (layer, 0, 0)),
            pl.BlockSpec((None, 1, HEAD), lambda b, j: (layer, 0, 0)),
        ],
        out_specs=pl.BlockSpec((t, u), lambda b, j: (b, 0)),
        out_shape=jax.ShapeDtypeStruct((nbatch * t, u), q.dtype),
        scratch_shapes=_softmax_scratch(nheads, t),
        compiler_params=_params(("parallel", "arbitrary"), est),
        name="diff_attention_sample",
    )(q, k, v, cache_k, cache_v, bias_c, bias_n, lam_params, subln2)


def _merge_kernel(ap_ref, as_ref, bp_ref, bs_ref, ga0_ref, ga1_ref, gb0_ref, gb1_ref, wa_ref, wb_ref, o_ref,
                  *, u, n_first):
    a = _group_tile((ap_ref, as_ref), n_first)
    b = _group_tile((bp_ref, bs_ref), n_first)
    ya = jnp.dot(a, wa_ref[...], preferred_element_type=F32)
    yb = jnp.dot(b, wb_ref[...], preferred_element_type=F32)
    for c, (ga, gb) in enumerate(((ga0_ref, gb0_ref), (ga1_ref, gb1_ref))):
        sl = slice(c * u, (c + 1) * u)
        o_ref[:, sl] = (ga[...].astype(F32) * ya[:, sl] + gb[...].astype(F32) * yb[:, sl]).astype(o_ref.dtype)


def _merge(ret_outs, dif_outs, gates, w_ret_up, w_dif_up, layer, tm):
    m = gates.shape[0]
    u = ret_outs[0].shape[1]
    d = 2 * u
    n_first = ret_outs[0].shape[0] // tm
    gate = lambda c: pl.BlockSpec((tm, u), lambda i: (i, c))
    rows = _group_specs(tm, u, n_first)
    wspec = pl.BlockSpec((None, u, d), lambda i: (layer, 0, 0))
    pbytes = jnp.dtype(gates.dtype).itemsize
    est = 2 * (8 * tm * u * pbytes + 2 * u * d * pbytes + tm * d * pbytes) + 3 * tm * d * 4
    return pl.pallas_call(
        functools.partial(_merge_kernel, u=u, n_first=n_first),
        grid=(m // tm,),
        in_specs=rows + rows + [gate(0), gate(1), gate(2), gate(3), wspec, wspec],
        out_specs=pl.BlockSpec((tm, d), lambda i: (i, 0)),
        out_shape=jax.ShapeDtypeStruct((m, d), gates.dtype),
        compiler_params=_params(("arbitrary",), est),
        name="gated_merge",
    )(*ret_outs, *dif_outs, gates, gates, gates, gates, w_ret_up, w_dif_up)


def _out_proj_kernel(x_ref, a_ref, w_ref, o_ref):
    o_ref[...] = x_ref[...] + jnp.dot(a_ref[...], w_ref[...], preferred_element_type=F32)


def _out_proj(x, merged, w_out, layer, tm):
    m, d = x.shape
    wbytes = jnp.dtype(w_out.dtype).itemsize
    est = 2 * (2 * tm * d * 4 + tm * d * wbytes + d * d * wbytes) + tm * d * 4
    return pl.pallas_call(
        _out_proj_kernel,
        grid=(m // tm,),
        in_specs=[
            pl.BlockSpec((tm, d), lambda i: (i, 0)),
            pl.BlockSpec((tm, d), lambda i: (i, 0)),
            pl.BlockSpec((None, d, d), lambda i: (layer, 0, 0)),
        ],
        out_specs=pl.BlockSpec((tm, d), lambda i: (i, 0)),
        out_shape=jax.ShapeDtypeStruct((m, d), F32),
        compiler_params=_params(("parallel",), est),
        name="output_projection",
    )(x, merged, w_out)


def _rotary_tables(pos):
    inv = ROPE_BASE ** (-jnp.arange(HALF, dtype=F32) / HALF)
    ang = pos.astype(F32)[:, None] * inv[None, :]
    cos, sin = jnp.cos(ang), jnp.sin(ang)
    return jnp.concatenate([cos, cos], axis=-1), jnp.concatenate([-sin, sin], axis=-1)


def kernel(x_prompt, x_sample, cache_diff_k, cache_diff_v, state_ret, ffn1_norm, ffn1_gate, ffn1_up, ffn1_down, mix_norm, w_in, q_norm, k_norm, lambda_q1, lambda_k1, lambda_q2, lambda_k2, subln, w_ret_up, w_dif_up, w_out, ffn2_norm, ffn2_gate, ffn2_up, ffn2_down, rel_bias):
    nb, seq, d = x_prompt.shape
    db, dseq, _ = x_sample.shape
    depth, _, past, nh_d, _ = cache_diff_k.shape
    nh_r = state_ret.shape[2]
    u = d // 2
    assert nh_r * HEAD == u and nh_d * HEAD == u and dseq == CHUNK and seq % CHUNK == 0
    assert w_in.shape[-1] == N_SEG * u
    mp, ms = nb * seq, db * dseq

    cast = lambda w: w.astype(MXU_DTYPE)
    wg1, wu1, wd1 = cast(ffn1_gate), cast(ffn1_up), cast(ffn1_down)
    wg2, wu2, wd2 = cast(ffn2_gate), cast(ffn2_up), cast(ffn2_down)
    w_ret_c, w_dif_c, w_out_c = cast(w_ret_up), cast(w_dif_up), cast(w_out)
    row3 = lambda g: g.reshape(depth, 1, g.shape[-1])
    n1, nmix, n2 = row3(ffn1_norm), row3(mix_norm), row3(ffn2_norm)
    qn2 = row3(jnp.concatenate([q_norm, q_norm], axis=-1))
    kn2 = row3(jnp.concatenate([k_norm, k_norm], axis=-1))
    subln2 = row3(subln)
    lam_params = jnp.stack([lambda_q1, lambda_k1, lambda_q2, lambda_k2], axis=1)
    cache_k = cache_diff_k.reshape(depth, db * past * nh_d, HEAD)
    cache_v = cache_diff_v.reshape(depth, db * past * nh_d, HEAD)

    pos = jnp.concatenate([jnp.tile(jnp.arange(seq, dtype=jnp.int32), nb),
                           past + jnp.tile(jnp.arange(dseq, dtype=jnp.int32), db)])
    rot = _rotary_tables(pos)

    tm = _pick(math.gcd(mp, ms), ROW_TILES)
    tm_proj = _pick(mp + ms, PROJ_ROW_TILES)
    groups = (mp, ms)
    zero_state = jnp.zeros((nb, nh_r, HEAD, HEAD), F32)
    t_ret = _pick(seq, RET_BLOCKS)

    x = (x_prompt.reshape(mp, d), x_sample.reshape(ms, d))
    kps, kss, vps, vss, states_p, states_s = [], [], [], [], [], []
    for l in range(depth):
        lam_init = 0.8 - 0.6 * math.exp(-0.3 * l)
        x, h = _ffn(x, n1, wg1, wu1, wd1, l, tm, next_gain=nmix)

        last = l == depth - 1
        seg = functools.partial(_segment, h, w_in, l)
        rq = seg(SEG_RQ, 1, tm_proj, "rotary", tables=rot)
        rk = seg(SEG_RK, 1, tm_proj, "rotary", tables=rot, scale=HEAD ** -0.5)
        rv = seg(SEG_RV, 1, tm_proj, "cast")
        rg = seg(SEG_RG, 1, tm_proj, "silu")
        dq = seg(SEG_DQ, 1, tm_proj, "norm", gain=qn2, scale=HALF ** -0.5)
        dk, k_p, k_s = seg(SEG_DK, 1, tm, "norm_keep", gain=kn2, group_rows=groups,
                           prev_prompt=kps if last else ())
        dv, v_p, v_s = seg(SEG_DV, 1, tm, "keep", group_rows=groups, prev_prompt=vps if last else ())
        gates = seg(SEG_GATES, N_SEG - SEG_GATES, tm_proj, "sigmoid")

        ret_p, st_p = _retention(rq, rk, rv, rg, zero_state, t_ret, nb, seq // t_ret, 0)
        ret_s, st_s = _retention(rq, rk, rv, rg, state_ret[l].astype(F32), dseq, db, 1, mp // dseq)

        dif_p = _attn_prompt(dq, dk, dv, rel_bias, lam_params, subln2, l, nb, seq, nh_d, lam_init)
        dif_s = _attn_sample(dq, dk, dv, cache_k, cache_v, rel_bias, lam_params, subln2, l, db, dseq, past,
                             mp, nh_d, lam_init)

        merged = _merge((ret_p, ret_s), (dif_p, dif_s), gates, w_ret_c, w_dif_c, l, tm)
        x = _out_proj(x, merged, w_out_c, l, tm)
        x = _ffn(x, n2, wg2, wu2, wd2, l, tm, split_out=groups if l == depth - 1 else None)

        for acc, val in ((kps, k_p), (kss, k_s), (vps, v_p), (vss, v_s), (states_p, st_p), (states_s, st_s)):
            acc.append(val)

    y_p, y_s = x
    kv_p = lambda parts: parts[-1].reshape(depth, nb, seq, nh_d, HEAD)
    kv_s = lambda parts: jnp.stack(parts).reshape(depth, db, dseq, nh_d, HEAD)
    return (y_p.reshape(nb, seq, d), y_s.reshape(db, dseq, d),
            kv_p(kps).astype(cache_diff_k.dtype), kv_p(vps).astype(cache_diff_v.dtype),
            jnp.stack(states_p).astype(state_ret.dtype),
            kv_s(kss).astype(cache_diff_k.dtype), kv_s(vss).astype(cache_diff_v.dtype),
            jnp.stack(states_s).astype(state_ret.dtype))
```

```python
import functools
import math

import numpy as np
import jax
import jax.numpy as jnp
from jax import lax
from jax.experimental import pallas as pl
from jax.experimental.pallas import tpu as pltpu

F32 = jnp.float32
MXU_DTYPE = jnp.bfloat16

CHUNK = 64
HEAD = 128
HALF = HEAD // 2
ROPE_BASE = 10000.0
N_BUCKETS = 32
MAX_DISTANCE = 128
EPS = 1e-6
MASK_VALUE = -1e30

VMEM_LIMIT_CAP = 60 * 1024 * 1024
MIB = 1024 * 1024

ROW_TILES = (512, 256, 128, 64)
PROJ_ROW_TILES = (1536, 1024, 768, 512, 256, 128, 64)
FF_TILES = (512, 256, 128)
FF_CHUNKS_PER_STEP = 2
ATTN_BLOCKS = (512, 256, 128)
HEADS_PER_STEP = (4, 2, 1)
RET_BLOCKS = (256, 128, 64)
CACHE_TILES = (2048, 1024, 512, 256, 128, 64)


def _pick(n, prefs):
    for p in prefs:
        if n % p == 0:
            return p
    raise ValueError(f"no tile in {prefs} divides {n}")


def _params(semantics, est_bytes):
    limit = int(min(max(est_bytes + 8 * MIB, 32 * MIB), VMEM_LIMIT_CAP))
    return pltpu.CompilerParams(dimension_semantics=semantics, vmem_limit_bytes=limit)


def _rms_scale(x):
    return lax.rsqrt(jnp.mean(x * x, axis=-1, keepdims=True) + EPS)


def _group_specs(tm, width, n_first, row_axis=0):
    first = pl.BlockSpec((tm, width), lambda *g: (jnp.minimum(g[row_axis], n_first - 1), 0))
    second = pl.BlockSpec((tm, width), lambda *g: (jnp.maximum(g[row_axis] - n_first, 0), 0),
                          pipeline_mode=pl.Buffered(1))
    return [first, second]


def _group_tile(refs, n_first, row_axis=0):
    if len(refs) == 1:
        return refs[0][...]
    return jnp.where(pl.program_id(row_axis) < n_first, refs[0][...], refs[1][...])


def _store_group_tile(refs, n_first, value, row_axis=0):
    if len(refs) == 1:
        refs[0][...] = value
        return
    i = pl.program_id(row_axis)

    @pl.when(i < n_first)
    def _():
        refs[0][...] = value

    @pl.when(i >= n_first)
    def _():
        refs[1][...] = value


def _ffn_kernel(*refs, nf, per_step, n_in, n_out, n_first, norm_out):
    x_refs = refs[:n_in]
    g_ref = refs[n_in]
    pos = n_in + 1
    w_refs = [refs[pos + 3 * c:pos + 3 * c + 3] for c in range(per_step)]
    pos += 3 * per_step
    g2_ref = refs[pos] if norm_out else None
    pos += int(norm_out)
    o_refs = refs[pos:pos + n_out]
    pos += n_out
    hn_ref = refs[pos] if norm_out else None
    pos += int(norm_out)
    h_ref, acc_ref = refs[pos:]
    f = pl.program_id(1)
    nsteps = -(-nf // per_step)

    @pl.when(f == 0)
    def _():
        x = _group_tile(x_refs, n_first)
        h_ref[...] = (x * _rms_scale(x) * g_ref[...]).astype(h_ref.dtype)
        acc_ref[...] = jnp.zeros_like(acc_ref)

    def chunks(slots):
        h = h_ref[...]
        pre = [(jnp.dot(h, wg[...], preferred_element_type=F32), jnp.dot(h, wu[...], preferred_element_type=F32))
               for wg, wu, _ in slots]
        total = acc_ref[...]
        for (gate, up), (_, _, wd) in zip(pre, slots):
            act = (gate * jax.nn.sigmoid(gate) * up).astype(wd.dtype)
            total = total + jnp.dot(act, wd[...], preferred_element_type=F32)
        acc_ref[...] = total

    head = nf - (nsteps - 1) * per_step
    if head == per_step:
        chunks(w_refs)
    else:
        @pl.when(f == 0)
        def _():
            chunks(w_refs[:head])

        @pl.when(f > 0)
        def _():
            chunks(w_refs)

    @pl.when(f == nsteps - 1)
    def _():
        y = _group_tile(x_refs, n_first) + 0.5 * acc_ref[...]
        _store_group_tile(o_refs, n_first, y)
        if norm_out:
            hn_ref[...] = (y * _rms_scale(y) * g2_ref[...]).astype(hn_ref.dtype)


def _ffn(xs, gain, wg, wu, wd, layer, tm, split_out=None, next_gain=None):
    xs = tuple(xs) if isinstance(xs, (tuple, list)) else (xs,)
    m = sum(x.shape[0] for x in xs)
    d = xs[0].shape[1]
    n_first = (xs[0].shape[0] if len(xs) == 2 else split_out[0] if split_out else m) // tm
    ff = wg.shape[-1]
    tf = _pick(ff, FF_TILES)
    nf = ff // tf
    norm_out = next_gain is not None
    wbytes = jnp.dtype(wg.dtype).itemsize
    row_buffers = (3 if len(xs) == 2 else 2) + (3 if split_out else 2)
    vmem_need = lambda chunks: (
        row_buffers * tm * d * 4 + 2 * (3 * chunks * d * tf * wbytes + int(norm_out) * tm * d * wbytes)
        + tm * d * (4 + wbytes) + 2 * chunks * tm * tf * 4)
    per_step = min(FF_CHUNKS_PER_STEP, nf)
    while per_step > 1 and vmem_need(per_step) > VMEM_LIMIT_CAP:
        per_step -= 1
    nsteps = -(-nf // per_step)
    est = vmem_need(per_step)
    rows = lambda: pl.BlockSpec((tm, d), lambda i, f: (i, 0))
    gain_spec = pl.BlockSpec((None, 1, d), lambda i, f: (layer, 0, 0))
    in_specs = (_group_specs(tm, d, n_first) if len(xs) == 2 else [rows()]) + [gain_spec]
    args = [*xs, gain]
    head = nf - (nsteps - 1) * per_step
    for c in range(per_step):
        chunk = lambda f, c=c: jnp.where(f == 0, min(c, head - 1), head + (f - 1) * per_step + c)
        in_specs += [pl.BlockSpec((None, d, tf), lambda i, f, chunk=chunk: (layer, 0, chunk(f))),
                     pl.BlockSpec((None, d, tf), lambda i, f, chunk=chunk: (layer, 0, chunk(f))),
                     pl.BlockSpec((None, tf, d), lambda i, f, chunk=chunk: (layer, chunk(f), 0))]
        args += [wg, wu, wd]
    if split_out:
        out_specs = _group_specs(tm, d, n_first)
        out_shape = [jax.ShapeDtypeStruct((r, d), F32) for r in split_out]
    else:
        out_specs = [rows()]
        out_shape = [jax.ShapeDtypeStruct((m, d), F32)]
    if norm_out:
        in_specs.append(gain_spec)
        args.append(next_gain)
        out_specs.append(rows())
        out_shape.append(jax.ShapeDtypeStruct((m, d), wg.dtype))
    outs = pl.pallas_call(
        functools.partial(_ffn_kernel, nf=nf, per_step=per_step, n_in=len(xs), n_out=2 if split_out else 1,
                          n_first=n_first, norm_out=norm_out),
        grid=(m // tm, nsteps),
        in_specs=in_specs,
        out_specs=out_specs,
        out_shape=out_shape,
        scratch_shapes=[pltpu.VMEM((tm, d), wg.dtype), pltpu.VMEM((tm, d), F32)],
        compiler_params=_params(("arbitrary", "arbitrary"), est),
        name="swiglu_half_step",
    )(*args)
    return outs if len(outs) > 1 else outs[0]


SEG_RQ, SEG_RK, SEG_RV, SEG_RG, SEG_DQ, SEG_DK, SEG_DV, SEG_GATES, N_SEG = 0, 1, 2, 3, 4, 5, 6, 7, 11


def _rotate_half_pairs(a, cos2, sin2):
    return a * cos2 + pltpu.roll(a, HALF, 1) * sin2


def _component_rms_norm(a, gain2):
    lo = lax.broadcasted_iota(jnp.int32, a.shape, 1) < HALF
    sq = a * a
    s_all = jnp.sum(sq, axis=-1, keepdims=True)
    s_lo = jnp.sum(jnp.where(lo, sq, 0.0), axis=-1, keepdims=True)
    ms = jnp.where(lo, s_lo, s_all - s_lo) * (1.0 / HALF)
    return a * lax.rsqrt(ms + EPS) * gain2


def _keep_f32(fp_ref, fs_ref, prev_refs, n_first, value):
    i = pl.program_id(1)

    @pl.when(i < n_first)
    def _():
        if prev_refs:
            for l, prev in enumerate(prev_refs):
                fp_ref[l] = prev[...]
            fp_ref[len(prev_refs)] = value
        else:
            fp_ref[...] = value

    @pl.when(i >= n_first)
    def _():
        fs_ref[...] = value


def _segment_kernel(h_ref, w_ref, *refs, kind, scale, nheads, n_first, n_prev):
    *refs, wc_ref = refs

    @pl.when(pl.program_id(1) == 0)
    def _():
        wc_ref[...] = w_ref[...].astype(wc_ref.dtype)

    acc = jnp.dot(h_ref[...], wc_ref[...], preferred_element_type=F32)
    heads = [slice(h * HEAD, (h + 1) * HEAD) for h in range(nheads)]
    if kind == "rotary":
        cos_ref, sin_ref, o_ref = refs
        for sl in heads:
            r = _rotate_half_pairs(acc[:, sl], cos_ref[...], sin_ref[...])
            o_ref[:, sl] = (r if scale == 1.0 else r * scale).astype(o_ref.dtype)
    elif kind == "cast":
        (o_ref,) = refs
        o_ref[...] = acc.astype(o_ref.dtype)
    elif kind == "silu":
        (o_ref,) = refs
        o_ref[...] = (acc * jax.nn.sigmoid(acc)).astype(o_ref.dtype)
    elif kind == "sigmoid":
        (o_ref,) = refs
        o_ref[...] = jax.nn.sigmoid(acc).astype(o_ref.dtype)
    elif kind == "norm":
        gain_ref, o_ref = refs
        for sl in heads:
            o_ref[:, sl] = (_component_rms_norm(acc[:, sl], gain_ref[...]) * scale).astype(o_ref.dtype)
    elif kind == "norm_keep":
        gain_ref, *prev_refs, o_ref, fp_ref, fs_ref = refs
        normed = jnp.concatenate([_component_rms_norm(acc[:, sl], gain_ref[...]) for sl in heads], axis=1)
        _keep_f32(fp_ref, fs_ref, prev_refs, n_first, normed)
        o_ref[...] = normed.astype(o_ref.dtype)
    elif kind == "keep":
        *prev_refs, o_ref, fp_ref, fs_ref = refs
        _keep_f32(fp_ref, fs_ref, prev_refs, n_first, acc)
        o_ref[...] = acc.astype(o_ref.dtype)
    else:
        raise ValueError(kind)
    assert kind not in ("norm_keep", "keep") or len(prev_refs) == n_prev


def _segment(h, w_in, layer, seg0, nseg, tm, kind, *, scale=1.0, tables=(), gain=None, group_rows=None,
             prev_prompt=()):
    m, d = h.shape
    u = d // 2
    nheads = u // HEAD
    n_first = group_rows[0] // tm if group_rows else 0
    n_prev = len(prev_prompt)
    in_specs = [pl.BlockSpec((tm, d), lambda s, i: (i, 0)),
                pl.BlockSpec((None, d, u), lambda s, i: (layer, 0, seg0 + s))]
    args = [h, w_in]
    for tab in tables:
        in_specs.append(pl.BlockSpec((tm, HEAD), lambda s, i: (i, 0)))
        args.append(tab)
    if gain is not None:
        in_specs.append(pl.BlockSpec((None, 1, HEAD), lambda s, i: (layer, 0, 0)))
        args.append(gain)
    out_specs = [pl.BlockSpec((tm, u), lambda s, i: (i, s))]
    out_shape = [jax.ShapeDtypeStruct((m, nseg * u), h.dtype)]
    if group_rows:
        first, second = _group_specs(tm, u, n_first, row_axis=1)
        in_specs += [first] * n_prev
        args += list(prev_prompt)
        if n_prev:
            first = pl.BlockSpec((n_prev + 1, tm, u), lambda s, i: (0, jnp.minimum(i, n_first - 1), 0))
        out_specs += [first, second]
        out_shape += [jax.ShapeDtypeStruct(((n_prev + 1, group_rows[0], u) if n_prev else (group_rows[0], u)), F32),
                      jax.ShapeDtypeStruct((group_rows[1], u), F32)]
    hb = jnp.dtype(h.dtype).itemsize
    est = (2 * (tm * d * hb + d * u * 4 + tm * u * hb + (2 * n_prev + 2) * tm * u * 4 + 2 * tm * HEAD * 4)
           + d * u * hb + 4 * tm * u * 4)
    outs = pl.pallas_call(
        functools.partial(_segment_kernel, kind=kind, scale=scale, nheads=nheads, n_first=n_first, n_prev=n_prev),
        grid=(nseg, m // tm),
        in_specs=in_specs,
        out_specs=out_specs,
        out_shape=out_shape,
        scratch_shapes=[pltpu.VMEM((d, u), h.dtype)],
        compiler_params=_params(("arbitrary", "arbitrary"), est),
        name="input_projection_" + kind,
    )(*args)
    return outs if len(outs) > 1 else outs[0]


def _retention_kernel(q_ref, k_ref, v_ref, g_ref, s0_ref, d_ref, wq_ref, we_ref, dec_ref, o_ref, sout_ref, st_ref,
                      *, nheads, nblk):
    t = pl.program_id(1)

    @pl.when(t == 0)
    def _():
        st_ref[...] = s0_ref[...]

    lanes = [slice(h * HEAD, (h + 1) * HEAD) for h in range(nheads)]
    scores, inter = [], []
    for h, sl in enumerate(lanes):
        q, k, v = q_ref[:, sl], k_ref[:, sl], v_ref[:, sl]
        state = st_ref[h]
        scores.append(lax.dot_general(q, k, (((1,), (1,)), ((), ())), preferred_element_type=F32))
        inter.append(jnp.dot(q, state.astype(q.dtype), preferred_element_type=F32))
        kw = (k.astype(F32) * we_ref[:, sl]).astype(k.dtype)
        kv = lax.dot_general(kw, v, (((0,), (0,)), ((), ())), preferred_element_type=F32)
        st_ref[h] = state * dec_ref[h:h + 1, :] + kv
    for h, sl in enumerate(lanes):
        v = v_ref[:, sl]
        s = scores[h] * d_ref[h]
        o = jnp.dot(s.astype(v.dtype), v, preferred_element_type=F32) + wq_ref[:, sl] * inter[h]
        r = o * _rms_scale(o)
        o_ref[:, sl] = (r * g_ref[:, sl].astype(F32)).astype(o_ref.dtype)

    @pl.when(t == nblk - 1)
    def _():
        sout_ref[...] = st_ref[...]


def _retention_tables(t, nheads):
    log_g = jnp.log(1.0 - 2.0 ** (-5.0 - jnp.arange(nheads, dtype=F32)))
    idx = jnp.arange(t, dtype=F32)
    dist = jnp.abs(idx[:, None] - idx[None, :])
    ci = np.arange(t) // CHUNK
    visible = jnp.asarray(ci[None, :] <= ci[:, None])
    dmat = jnp.where(visible[None], jnp.exp(log_g[:, None, None] * dist[None]), 0.0)
    wq = jnp.exp(log_g[None, :] * (idx + 1.0)[:, None])
    we = jnp.exp(log_g[None, :] * (t - 1.0 - idx)[:, None])
    dec = jnp.exp(log_g * t)
    expand = lambda a: jnp.repeat(a, HEAD, axis=1)
    return dmat, expand(wq), expand(we), jnp.broadcast_to(dec[:, None], (nheads, HEAD))


def _retention(q, k, v, g, s0, t, nbatch, nblk, row_block0):
    nheads = s0.shape[1]
    u = nheads * HEAD
    dmat, wq, we, dec = _retention_tables(t, nheads)
    rows = pl.BlockSpec((t, u), lambda b, i: (row_block0 + b * nblk + i, 0))
    whole = lambda a: pl.BlockSpec(a.shape, lambda b, i: (0,) * a.ndim)
    state_spec = pl.BlockSpec((None, nheads, HEAD, HEAD), lambda b, i: (b, 0, 0, 0))
    pbytes = jnp.dtype(q.dtype).itemsize
    est = (2 * (5 * t * u * pbytes + 2 * nheads * HEAD * HEAD * 4 + nheads * t * t * 4 + 2 * t * u * 4)
           + nheads * HEAD * HEAD * 4 + 6 * t * max(t, HEAD) * 4)
    return pl.pallas_call(
        functools.partial(_retention_kernel, nheads=nheads, nblk=nblk),
        grid=(nbatch, nblk),
        in_specs=[rows, rows, rows, rows, state_spec, whole(dmat), whole(wq), whole(we), whole(dec)],
        out_specs=[
            pl.BlockSpec((t, u), lambda b, i: (b * nblk + i, 0)),
            state_spec,
        ],
        out_shape=[
            jax.ShapeDtypeStruct((nbatch * nblk * t, u), q.dtype),
            jax.ShapeDtypeStruct((nbatch, nheads, HEAD, HEAD), F32),
        ],
        scratch_shapes=[pltpu.VMEM((nheads, HEAD, HEAD), F32)],
        compiler_params=_params(("parallel", "arbitrary"), est),
        name="retention",
    )(q, k, v, g, s0, dmat, wq, we, dec)


def _bucket_thresholds():
    nb = N_BUCKETS // 2
    me = nb // 2
    out = []
    for k in range(1, nb - me):
        n = me
        while n ** (nb - me) * me ** k < me ** (nb - me) * MAX_DISTANCE ** k:
            n += 1
        out.append(n)
    return out


def _t5_bucket_np(rel):
    nb = N_BUCKETS // 2
    me = nb // 2
    n = np.abs(rel)
    large = np.full(rel.shape, me, np.int64)
    for thr in _bucket_thresholds():
        large += (n >= thr)
    large = np.minimum(large, nb - 1)
    return (np.where(rel > 0, nb, 0) + np.where(n < me, n, large)).astype(np.int32)


def _bias_kernel(rb_ref, idx_ref, mask_ref, o_ref):
    h = pl.program_id(0)
    idx = idx_ref[...]
    out = mask_ref[...]
    for b in range(N_BUCKETS):
        out = out + jnp.where(idx == b, rb_ref[b, h], 0.0)
    o_ref[...] = out


def _bias_table(rel_bias, qpos, kpos):
    nheads = rel_bias.shape[1]
    rel = kpos[None, :] - qpos[:, None]
    idx = jnp.asarray(_t5_bucket_np(rel))
    mask = jnp.asarray(np.where((kpos[None, :] // CHUNK) <= (qpos[:, None] // CHUNK), 0.0, MASK_VALUE)
                       .astype(np.float32))
    nq, nk = rel.shape
    return pl.pallas_call(
        _bias_kernel,
        grid=(nheads,),
        in_specs=[
            pl.BlockSpec(memory_space=pltpu.SMEM),
            pl.BlockSpec((nq, nk), lambda h: (0, 0)),
            pl.BlockSpec((nq, nk), lambda h: (0, 0)),
        ],
        out_specs=pl.BlockSpec((None, nq, nk), lambda h: (h, 0, 0)),
        out_shape=jax.ShapeDtypeStruct((nheads, nq, nk), F32),
        compiler_params=_params(("arbitrary",), 6 * nq * nk * 4),
        name="relative_bias_table",
    )(rel_bias, idx, mask)


def _stack_components(q):
    lo = lax.broadcasted_iota(jnp.int32, q.shape, 1) < HALF
    zero = jnp.zeros_like(q)
    return jnp.concatenate([jnp.where(lo, q, zero), jnp.where(lo, zero, q)], axis=0)


def _biased_scores(qs, k, bias):
    s = lax.dot_general(qs, k, (((1,), (1,)), ((), ())), preferred_element_type=F32)
    if bias.ndim == 2:
        t = bias.shape[0]
        return jnp.concatenate([s[:t] + bias, s[t:] + bias], axis=0), 0.0
    return s, bias


def _softmax_step(qs, k, v, bias, stats, g):
    m_ref, l_ref, acc_ref = stats
    s, c = _biased_scores(qs, k, bias)
    m_prev = m_ref[g]
    m_new = jnp.maximum(m_prev, jnp.max(s, axis=-1, keepdims=True) + c)
    alpha = jnp.exp(m_prev - m_new)
    p = jnp.exp(s - (m_new - c))
    l_ref[g] = alpha * l_ref[g] + jnp.sum(p, axis=-1, keepdims=True)
    acc_ref[g] = alpha * acc_ref[g] + jnp.dot(p.astype(v.dtype), v, preferred_element_type=F32)
    m_ref[g] = m_new


def _softmax_init(stats):
    m_ref, l_ref, acc_ref = stats
    m_ref[...] = jnp.full_like(m_ref, MASK_VALUE)
    l_ref[...] = jnp.zeros_like(l_ref)
    acc_ref[...] = jnp.zeros_like(acc_ref)


def _softmax_scratch(groups, t):
    return [pltpu.VMEM((groups, 2 * t, 1), F32), pltpu.VMEM((groups, 2 * t, 1), F32),
            pltpu.VMEM((groups, 2 * t, HEAD), F32)]


def _lambda_value(lam_ref, lam_init):
    a = lam_ref[...]
    e1 = jnp.exp(jnp.sum(a[0:1] * a[1:2], axis=-1, keepdims=True))
    e2 = jnp.exp(jnp.sum(a[2:3] * a[3:4], axis=-1, keepdims=True))
    return e1 - e2 + lam_init


def _diff_finish(t, lam, lam_init, subln, stats, g):
    _, l_ref, acc_ref = stats
    acc = acc_ref[g]
    l = l_ref[g]
    o = acc[:t] / l[:t] - lam * (acc[t:] / l[t:])
    return o * _rms_scale(o) * subln * (1.0 - lam_init)


def _attn_prompt_kernel(rb_ref, q_ref, k_ref, v_ref, bias_ref, lam_ref, sub_ref, o_ref, *stats,
                        t, groups, lam_init, far_bucket):
    hp = pl.program_id(1)
    qi = pl.program_id(2)
    m_ref, l_ref, acc_ref = stats
    lanes = [slice(g * HEAD, (g + 1) * HEAD) for g in range(groups)]
    qs = [_stack_components(q_ref[:, sl]) for sl in lanes]
    far_bias = [rb_ref[far_bucket, hp * groups + g] for g in range(groups)]
    _softmax_init(stats)

    def sweep(step):
        def run(j, bias_of):
            step(pl.ds(pl.multiple_of(j * t, t), t), [bias_of(g) for g in range(groups)])

        def far_step(j, carry):
            run(j, lambda g: far_bias[g])
            return carry

        lax.fori_loop(0, jnp.maximum(qi - 1, 0), far_step, 0)

        @pl.when(qi > 0)
        def _():
            run(qi - 1, lambda g: bias_ref[g, :, :t])

        run(qi, lambda g: bias_ref[g, :, t:])

    def max_step(rows, biases):
        scores = [_biased_scores(qs[g], k_ref[rows, lanes[g]], biases[g]) for g in range(groups)]
        for g, (s, c) in enumerate(scores):
            m_ref[g] = jnp.maximum(m_ref[g], jnp.max(s, axis=-1, keepdims=True) + c)

    def acc_step(rows, biases):
        scores = [_biased_scores(qs[g], k_ref[rows, lanes[g]], biases[g]) for g in range(groups)]
        probs = [jnp.exp(s - (m_ref[g] - c)) for g, (s, c) in enumerate(scores)]
        for g, p in enumerate(probs):
            l_ref[g] += jnp.sum(p, axis=-1, keepdims=True)
            v = v_ref[rows, lanes[g]]
            acc_ref[g] += jnp.dot(p.astype(v.dtype), v, preferred_element_type=F32)

    sweep(max_step)
    sweep(acc_step)

    lam = _lambda_value(lam_ref, lam_init)
    for g, sl in enumerate(lanes):
        o_ref[:, sl] = _diff_finish(t, lam, lam_init, sub_ref[...], stats, g).astype(o_ref.dtype)


def _attn_prompt(q, k, v, rel_bias, lam_params, subln2, layer, nbatch, seq, nheads, lam_init):
    u = nheads * HEAD
    t = _pick(seq, ATTN_BLOCKS)
    groups = _pick(nheads, HEADS_PER_STEP)
    assert t % CHUNK == 0 and t + 1 >= _bucket_thresholds()[-1]
    nq = seq // t
    w = groups * HEAD
    r = np.arange(t)
    bias = _bias_table(rel_bias, r + t, np.arange(2 * t))
    pbytes = jnp.dtype(q.dtype).itemsize
    est = (2 * (2 * t * w * pbytes + 2 * seq * w * pbytes + groups * 2 * t * t * 4)
           + groups * (2 * t * (HEAD + 2 * 128) * 4 + 6 * 2 * t * t * 4))
    return pl.pallas_call(
        functools.partial(_attn_prompt_kernel, t=t, groups=groups, lam_init=lam_init,
                          far_bucket=N_BUCKETS // 2 - 1),
        grid=(nbatch, nheads // groups, nq),
        in_specs=[
            pl.BlockSpec(memory_space=pltpu.SMEM),
            pl.BlockSpec((t, w), lambda b, h, i: (b * nq + i, h)),
            pl.BlockSpec((seq, w), lambda b, h, i: (b, h)),
            pl.BlockSpec((seq, w), lambda b, h, i: (b, h)),
            pl.BlockSpec((groups, t, 2 * t), lambda b, h, i: (h, 0, 0)),
            pl.BlockSpec((None, 4, HALF), lambda b, h, i: (layer, 0, 0)),
            pl.BlockSpec((None, 1, HEAD), lambda b, h, i: (layer, 0, 0)),
        ],
        out_specs=pl.BlockSpec((t, w), lambda b, h, i: (b * nq + i, h)),
        out_shape=jax.ShapeDtypeStruct((nbatch * seq, u), q.dtype),
        scratch_shapes=_softmax_scratch(groups, t),
        compiler_params=_params(("parallel", "parallel", "arbitrary"), est),
        name="diff_attention_prompt",
    )(rel_bias, q, k, v, bias, lam_params, subln2)


def _attn_sample_kernel(q_ref, kn_ref, vn_ref, kc_ref, vc_ref, bc_ref, bn_ref, lam_ref, sub_ref,
                        o_ref, *stats, t, tk, ncache, nheads, lam_init):
    j = pl.program_id(1)
    lanes = [slice(h * HEAD, (h + 1) * HEAD) for h in range(nheads)]

    @pl.when(j == 0)
    def _():
        _softmax_init(stats)

    qs = [_stack_components(q_ref[:, sl]) for sl in lanes]
    head_rows = lambda ref, h: ref[pl.ds(h, tk, stride=nheads), :].astype(qs[0].dtype)
    for h in range(nheads):
        _softmax_step(qs[h], head_rows(kc_ref, h), head_rows(vc_ref, h), bc_ref[h], stats, h)

    @pl.when(j == ncache - 1)
    def _():
        lam = _lambda_value(lam_ref, lam_init)
        for h, sl in enumerate(lanes):
            _softmax_step(qs[h], kn_ref[:, sl], vn_ref[:, sl], bn_ref[h], stats, h)
            o_ref[:, sl] = _diff_finish(t, lam, lam_init, sub_ref[...], stats, h).astype(o_ref.dtype)


def _attn_sample(q, k, v, cache_k, cache_v, rel_bias, lam_params, subln2, layer, nbatch, t, past,
                 row0, nheads, lam_init):
    u = nheads * HEAD
    assert row0 % t == 0
    rb0 = row0 // t
    tk = _pick(past, CACHE_TILES)
    qpos = past + np.arange(t)
    ncache = past // tk
    bias_c = _bias_table(rel_bias, qpos, np.arange(past))
    bias_c = bias_c.reshape(nheads, t, ncache, tk).transpose(2, 0, 1, 3)
    bias_n = _bias_table(rel_bias, qpos, qpos)
    pbytes = jnp.dtype(q.dtype).itemsize
    est = (2 * (4 * t * u * pbytes + 2 * tk * nheads * HEAD * 4 + nheads * t * (tk + t) * 4)
           + nheads * (2 * t * (HEAD + 2 * 128) * 4 + 6 * 2 * t * tk * 4))
    rows = pl.BlockSpec((t, u), lambda b, j: (rb0 + b, 0))
    cache = pl.BlockSpec((None, tk * nheads, HEAD), lambda b, j: (layer, b * ncache + j, 0))
    return pl.pallas_call(
        functools.partial(_attn_sample_kernel, t=t, tk=tk, ncache=ncache, nheads=nheads, lam_init=lam_init),
        grid=(nbatch, ncache),
        in_specs=[
            rows, rows, rows, cache, cache,
            pl.BlockSpec((None, nheads, t, tk), lambda b, j: (j, 0, 0, 0)),
            pl.BlockSpec((nheads, t, t), lambda b, j: (0, 0, 0)),
            pl.BlockSpec((None, 4, HALF), lambda b, j: (layer, 0, 0)),
            pl.BlockSpec((None, 1, HEAD), lambda b, j: (layer, 0, 0)),
        ],
        out_specs=pl.BlockSpec((t, u), lambda b, j: (b, 0)),
        out_shape=jax.ShapeDtypeStruct((nbatch * t, u), q.dtype),
        scratch_shapes=_softmax_scratch(nheads, t),
        compiler_params=_params(("parallel", "arbitrary"), est),
        name="diff_attention_sample",
    )(q, k, v, cache_k, cache_v, bias_c, bias_n, lam_params, subln2)


def _merge_kernel(ap_ref, as_ref, bp_ref, bs_ref, ga0_ref, ga1_ref, gb0_ref, gb1_ref, wa_ref, wb_ref, o_ref,
                  *, u, n_first):
    a = _group_tile((ap_ref, as_ref), n_first)
    b = _group_tile((bp_ref, bs_ref), n_first)
    ya = jnp.dot(a, wa_ref[...], preferred_element_type=F32)
    yb = jnp.dot(b, wb_ref[...], preferred_element_type=F32)
    for c, (ga, gb) in enumerate(((ga0_ref, gb0_ref), (ga1_ref, gb1_ref))):
        sl = slice(c * u, (c + 1) * u)
        o_ref[:, sl] = (ga[...].astype(F32) * ya[:, sl] + gb[...].astype(F32) * yb[:, sl]).astype(o_ref.dtype)


def _merge(ret_outs, dif_outs, gates, w_ret_up, w_dif_up, layer, tm):
    m = gates.shape[0]
    u = ret_outs[0].shape[1]
    d = 2 * u
    n_first = ret_outs[0].shape[0] // tm
    gate = lambda c: pl.BlockSpec((tm, u), lambda i: (i, c))
    rows = _group_specs(tm, u, n_first)
    wspec = pl.BlockSpec((None, u, d), lambda i: (layer, 0, 0))
    pbytes = jnp.dtype(gates.dtype).itemsize
    est = 2 * (8 * tm * u * pbytes + 2 * u * d * pbytes + tm * d * pbytes) + 3 * tm * d * 4
    return pl.pallas_call(
        functools.partial(_merge_kernel, u=u, n_first=n_first),
        grid=(m // tm,),
        in_specs=rows + rows + [gate(0), gate(1), gate(2), gate(3), wspec, wspec],
        out_specs=pl.BlockSpec((tm, d), lambda i: (i, 0)),
        out_shape=jax.ShapeDtypeStruct((m, d), gates.dtype),
        compiler_params=_params(("arbitrary",), est),
        name="gated_merge",
    )(*ret_outs, *dif_outs, gates, gates, gates, gates, w_ret_up, w_dif_up)


def _out_proj_kernel(x_ref, a_ref, w_ref, o_ref):
    o_ref[...] = x_ref[...] + jnp.dot(a_ref[...], w_ref[...], preferred_element_type=F32)


def _out_proj(x, merged, w_out, layer, tm):
    m, d = x.shape
    wbytes = jnp.dtype(w_out.dtype).itemsize
    est = 2 * (2 * tm * d * 4 + tm * d * wbytes + d * d * wbytes) + tm * d * 4
    return pl.pallas_call(
        _out_proj_kernel,
        grid=(m // tm,),
        in_specs=[
            pl.BlockSpec((tm, d), lambda i: (i, 0)),
            pl.BlockSpec((tm, d), lambda i: (i, 0)),
            pl.BlockSpec((None, d, d), lambda i: (layer, 0, 0)),
        ],
        out_specs=pl.BlockSpec((tm, d), lambda i: (i, 0)),
        out_shape=jax.ShapeDtypeStruct((m, d), F32),
        compiler_params=_params(("parallel",), est),
        name="output_projection",
    )(x, merged, w_out)


def _rotary_tables(pos):
    inv = ROPE_BASE ** (-jnp.arange(HALF, dtype=F32) / HALF)
    ang = pos.astype(F32)[:, None] * inv[None, :]
    cos, sin = jnp.cos(ang), jnp.sin(ang)
    return jnp.concatenate([cos, cos], axis=-1), jnp.concatenate([-sin, sin], axis=-1)


def kernel(x_prompt, x_sample, cache_diff_k, cache_diff_v, state_ret, ffn1_norm, ffn1_gate, ffn1_up, ffn1_down, mix_norm, w_in, q_norm, k_norm, lambda_q1, lambda_k1, lambda_q2, lambda_k2, subln, w_ret_up, w_dif_up, w_out, ffn2_norm, ffn2_gate, ffn2_up, ffn2_down, rel_bias):
    nb, seq, d = x_prompt.shape
    db, dseq, _ = x_sample.shape
    depth, _, past, nh_d, _ = cache_diff_k.shape
    nh_r = state_ret.shape[2]
    u = d // 2
    assert nh_r * HEAD == u and nh_d * HEAD == u and dseq == CHUNK and seq % CHUNK == 0
    assert w_in.shape[-1] == N_SEG * u
    mp, ms = nb * seq, db * dseq

    cast = lambda w: w.astype(MXU_DTYPE)
    wg1, wu1, wd1 = cast(ffn1_gate), cast(ffn1_up), cast(ffn1_down)
    wg2, wu2, wd2 = cast(ffn2_gate), cast(ffn2_up), cast(ffn2_down)
    w_ret_c, w_dif_c, w_out_c = cast(w_ret_up), cast(w_dif_up), cast(w_out)
    row3 = lambda g: g.reshape(depth, 1, g.shape[-1])
    n1, nmix, n2 = row3(ffn1_norm), row3(mix_norm), row3(ffn2_norm)
    qn2 = row3(jnp.concatenate([q_norm, q_norm], axis=-1))
    kn2 = row3(jnp.concatenate([k_norm, k_norm], axis=-1))
    subln2 = row3(subln)
    lam_params = jnp.stack([lambda_q1, lambda_k1, lambda_q2, lambda_k2], axis=1)
    cache_k = cache_diff_k.reshape(depth, db * past * nh_d, HEAD)
    cache_v = cache_diff_v.reshape(depth, db * past * nh_d, HEAD)

    pos = jnp.concatenate([jnp.tile(jnp.arange(seq, dtype=jnp.int32), nb),
                           past + jnp.tile(jnp.arange(dseq, dtype=jnp.int32), db)])
    rot = _rotary_tables(pos)

    tm = _pick(math.gcd(mp, ms), ROW_TILES)
    tm_proj = _pick(mp + ms, PROJ_ROW_TILES)
    groups = (mp, ms)
    zero_state = jnp.zeros((nb, nh_r, HEAD, HEAD), F32)
    t_ret = _pick(seq, RET_BLOCKS)

    x = (x_prompt.reshape(mp, d), x_sample.reshape(ms, d))
    kps, kss, vps, vss, states_p, states_s = [], [], [], [], [], []
    for l in range(depth):
        lam_init = 0.8 - 0.6 * math.exp(-0.3 * l)
        x, h = _ffn(x, n1, wg1, wu1, wd1, l, tm, next_gain=nmix)

        last = l == depth - 1
        seg = functools.partial(_segment, h, w_in, l)
        rq = seg(SEG_RQ, 1, tm_proj, "rotary", tables=rot)
        rk = seg(SEG_RK, 1, tm_proj, "rotary", tables=rot, scale=HEAD ** -0.5)
        rv = seg(SEG_RV, 1, tm_proj, "cast")
        rg = seg(SEG_RG, 1, tm_proj, "silu")
        dq = seg(SEG_DQ, 1, tm_proj, "norm", gain=qn2, scale=HALF ** -0.5)
        dk, k_p, k_s = seg(SEG_DK, 1, tm, "norm_keep", gain=kn2, group_rows=groups,
                           prev_prompt=kps if last else ())
        dv, v_p, v_s = seg(SEG_DV, 1, tm, "keep", group_rows=groups, prev_prompt=vps if last else ())
        gates = seg(SEG_GATES, N_SEG - SEG_GATES, tm_proj, "sigmoid")

        ret_p, st_p = _retention(rq, rk, rv, rg, zero_state, t_ret, nb, seq // t_ret, 0)
        ret_s, st_s = _retention(rq, rk, rv, rg, state_ret[l].astype(F32), dseq, db, 1, mp // dseq)

        dif_p = _attn_prompt(dq, dk, dv, rel_bias, lam_params, subln2, l, nb, seq, nh_d, lam_init)
        dif_s = _attn_sample(dq, dk, dv, cache_k, cache_v, rel_bias, lam_params, subln2, l, db, dseq, past,
                             mp, nh_d, lam_init)

        merged = _merge((ret_p, ret_s), (dif_p, dif_s), gates, w_ret_c, w_dif_c, l, tm)
        x = _out_proj(x, merged, w_out_c, l, tm)
        x = _ffn(x, n2, wg2, wu2, wd2, l, tm, split_out=groups if l == depth - 1 else None)

        for acc, val in ((kps, k_p), (kss, k_s), (vps, v_p), (vss, v_s), (states_p, st_p), (states_s, st_s)):
            acc.append(val)

    y_p, y_s = x
    kv_p = lambda parts: parts[-1].reshape(depth, nb, seq, nh_d, HEAD)
    kv_s = lambda parts: jnp.stack(parts).reshape(depth, db, dseq, nh_d, HEAD)
    return (y_p.reshape(nb, seq, d), y_s.reshape(db, dseq, d),
            kv_p(kps).astype(cache_diff_k.dtype), kv_p(vps).astype(cache_diff_v.dtype),
            jnp.stack(states_p).astype(state_ret.dtype),
            kv_s(kss).astype(cache_diff_k.dtype), kv_s(vss).astype(cache_diff_v.dtype),
            jnp.stack(states_s).astype(state_ret.dtype))
```

```python
import functools
import math

import numpy as np
import jax
import jax.numpy as jnp
from jax import lax
from jax.experimental import pallas as pl
from jax.experimental.pallas import tpu as pltpu

F32 = jnp.float32
MXU_DTYPE = jnp.bfloat16

CHUNK = 64
HEAD = 128
HALF = HEAD // 2
ROPE_BASE = 10000.0
N_BUCKETS = 32
MAX_DISTANCE = 128
EPS = 1e-6
MASK_VALUE = -1e30

VMEM_LIMIT_CAP = 60 * 1024 * 1024
MIB = 1024 * 1024

ROW_TILES = (512, 256, 128, 64)
PROJ_ROW_TILES = (1536, 1024, 768, 512, 256, 128, 64)
FF_TILES = (512, 256, 128)
FF_CHUNKS_PER_STEP = 2
ATTN_BLOCKS = (512, 256, 128)
HEADS_PER_STEP = (4, 2, 1)
RET_BLOCKS = (256, 128, 64)
CACHE_TILES = (2048, 1024, 512, 256, 128, 64)


def _pick(n, prefs):
    for p in prefs:
        if n % p == 0:
            return p
    raise ValueError(f"no tile in {prefs} divides {n}")


def _params(semantics, est_bytes):
    limit = int(min(max(est_bytes + 8 * MIB, 32 * MIB), VMEM_LIMIT_CAP))
    return pltpu.CompilerParams(dimension_semantics=semantics, vmem_limit_bytes=limit)


def _rms_scale(x):
    return lax.rsqrt(jnp.mean(x * x, axis=-1, keepdims=True) + EPS)


def _group_specs(tm, width, n_first, row_axis=0):
    first = pl.BlockSpec((tm, width), lambda *g: (jnp.minimum(g[row_axis], n_first - 1), 0))
    second = pl.BlockSpec((tm, width), lambda *g: (jnp.maximum(g[row_axis] - n_first, 0), 0),
                          pipeline_mode=pl.Buffered(1))
    return [first, second]


def _group_tile(refs, n_first, row_axis=0):
    if len(refs) == 1:
        return refs[0][...]
    return jnp.where(pl.program_id(row_axis) < n_first, refs[0][...], refs[1][...])


def _store_group_tile(refs, n_first, value, row_axis=0):
    if len(refs) == 1:
        refs[0][...] = value
        return
    i = pl.program_id(row_axis)

    @pl.when(i < n_first)
    def _():
        refs[0][...] = value

    @pl.when(i >= n_first)
    def _():
        refs[1][...] = value


def _ffn_kernel(*refs, nf, per_step, n_in, n_out, n_first, norm_out):
    x_refs = refs[:n_in]
    g_ref = refs[n_in]
    pos = n_in + 1
    w_refs = [refs[pos + 3 * c:pos + 3 * c + 3] for c in range(per_step)]
    pos += 3 * per_step
    g2_ref = refs[pos] if norm_out else None
    pos += int(norm_out)
    o_refs = refs[pos:pos + n_out]
    pos += n_out
    hn_ref = refs[pos] if norm_out else None
    pos += int(norm_out)
    h_ref, acc_ref = refs[pos:]
    f = pl.program_id(1)
    nsteps = -(-nf // per_step)

    @pl.when(f == 0)
    def _():
        x = _group_tile(x_refs, n_first)
        h_ref[...] = (x * _rms_scale(x) * g_ref[...]).astype(h_ref.dtype)
        acc_ref[...] = jnp.zeros_like(acc_ref)

    def chunks(slots):
        h = h_ref[...]
        pre = [(jnp.dot(h, wg[...], preferred_element_type=F32), jnp.dot(h, wu[...], preferred_element_type=F32))
               for wg, wu, _ in slots]
        total = acc_ref[...]
        for (gate, up), (_, _, wd) in zip(pre, slots):
            act = (gate * jax.nn.sigmoid(gate) * up).astype(wd.dtype)
            total = total + jnp.dot(act, wd[...], preferred_element_type=F32)
        acc_ref[...] = total

    head = nf - (nsteps - 1) * per_step
    if head == per_step:
        chunks(w_refs)
    else:
        @pl.when(f == 0)
        def _():
            chunks(w_refs[:head])

        @pl.when(f > 0)
        def _():
            chunks(w_refs)

    @pl.when(f == nsteps - 1)
    def _():
        y = _group_tile(x_refs, n_first) + 0.5 * acc_ref[...]
        _store_group_tile(o_refs, n_first, y)
        if norm_out:
            hn_ref[...] = (y * _rms_scale(y) * g2_ref[...]).astype(hn_ref.dtype)


def _ffn(xs, gain, wg, wu, wd, layer, tm, split_out=None, next_gain=None):
    xs = tuple(xs) if isinstance(xs, (tuple, list)) else (xs,)
    m = sum(x.shape[0] for x in xs)
    d = xs[0].shape[1]
    n_first = (xs[0].shape[0] if len(xs) == 2 else split_out[0] if split_out else m) // tm
    nf, tf = wg.shape[1], wg.shape[3]
    norm_out = next_gain is not None
    wbytes = jnp.dtype(wg.dtype).itemsize
    row_buffers = (3 if len(xs) == 2 else 2) + (3 if split_out else 2)
    vmem_need = lambda chunks: (
        row_buffers * tm * d * 4 + 2 * (3 * chunks * d * tf * wbytes + int(norm_out) * tm * d * wbytes)
        + tm * d * (4 + wbytes) + 2 * chunks * tm * tf * 4)
    per_step = min(FF_CHUNKS_PER_STEP, nf)
    while per_step > 1 and vmem_need(per_step) > VMEM_LIMIT_CAP:
        per_step -= 1
    nsteps = -(-nf // per_step)
    est = vmem_need(per_step)
    rows = lambda: pl.BlockSpec((tm, d), lambda i, f: (i, 0))
    gain_spec = pl.BlockSpec((None, 1, d), lambda i, f: (layer, 0, 0))
    in_specs = (_group_specs(tm, d, n_first) if len(xs) == 2 else [rows()]) + [gain_spec]
    args = [*xs, gain]
    head = nf - (nsteps - 1) * per_step
    for c in range(per_step):
        chunk = lambda f, c=c: jnp.where(f == 0, min(c, head - 1), head + (f - 1) * per_step + c)
        in_specs += [pl.BlockSpec((None, None, d, tf), lambda i, f, chunk=chunk: (layer, chunk(f), 0, 0)),
                     pl.BlockSpec((None, None, d, tf), lambda i, f, chunk=chunk: (layer, chunk(f), 0, 0)),
                     pl.BlockSpec((None, tf, d), lambda i, f, chunk=chunk: (layer, chunk(f), 0))]
        args += [wg, wu, wd]
    if split_out:
        out_specs = _group_specs(tm, d, n_first)
        out_shape = [jax.ShapeDtypeStruct((r, d), F32) for r in split_out]
    else:
        out_specs = [rows()]
        out_shape = [jax.ShapeDtypeStruct((m, d), F32)]
    if norm_out:
        in_specs.append(gain_spec)
        args.append(next_gain)
        out_specs.append(rows())
        out_shape.append(jax.ShapeDtypeStruct((m, d), wg.dtype))
    outs = pl.pallas_call(
        functools.partial(_ffn_kernel, nf=nf, per_step=per_step, n_in=len(xs), n_out=2 if split_out else 1,
                          n_first=n_first, norm_out=norm_out),
        grid=(m // tm, nsteps),
        in_specs=in_specs,
        out_specs=out_specs,
        out_shape=out_shape,
        scratch_shapes=[pltpu.VMEM((tm, d), wg.dtype), pltpu.VMEM((tm, d), F32)],
        compiler_params=_params(("arbitrary", "arbitrary"), est),
        name="swiglu_half_step",
    )(*args)
    return outs if len(outs) > 1 else outs[0]


SEG_RQ, SEG_RK, SEG_RV, SEG_RG, SEG_DQ, SEG_DK, SEG_DV, SEG_GATES, N_SEG = 0, 1, 2, 3, 4, 5, 6, 7, 11


def _rotate_half_pairs(a, cos2, sin2):
    return a * cos2 + pltpu.roll(a, HALF, 1) * sin2


def _component_rms_norm(a, gain2):
    lo = lax.broadcasted_iota(jnp.int32, a.shape, 1) < HALF
    sq = a * a
    s_all = jnp.sum(sq, axis=-1, keepdims=True)
    s_lo = jnp.sum(jnp.where(lo, sq, 0.0), axis=-1, keepdims=True)
    ms = jnp.where(lo, s_lo, s_all - s_lo) * (1.0 / HALF)
    return a * lax.rsqrt(ms + EPS) * gain2


def _keep_f32(fp_ref, fs_ref, prev_refs, n_first, value):
    i = pl.program_id(1)

    @pl.when(i < n_first)
    def _():
        if prev_refs:
            for l, prev in enumerate(prev_refs):
                fp_ref[l] = prev[...]
            fp_ref[len(prev_refs)] = value
        else:
            fp_ref[...] = value

    @pl.when(i >= n_first)
    def _():
        fs_ref[...] = value


def _segment_kernel(h_ref, w_ref, *refs, kind, scale, nheads, n_first, n_prev):
    *refs, wc_ref = refs

    @pl.when(pl.program_id(1) == 0)
    def _():
        wc_ref[...] = w_ref[...].astype(wc_ref.dtype)

    acc = jnp.dot(h_ref[...], wc_ref[...], preferred_element_type=F32)
    heads = [slice(h * HEAD, (h + 1) * HEAD) for h in range(nheads)]
    if kind == "rotary":
        cos_ref, sin_ref, o_ref = refs
        for sl in heads:
            r = _rotate_half_pairs(acc[:, sl], cos_ref[...], sin_ref[...])
            o_ref[:, sl] = (r if scale == 1.0 else r * scale).astype(o_ref.dtype)
    elif kind == "cast":
        (o_ref,) = refs
        o_ref[...] = acc.astype(o_ref.dtype)
    elif kind == "silu":
        (o_ref,) = refs
        o_ref[...] = (acc * jax.nn.sigmoid(acc)).astype(o_ref.dtype)
    elif kind == "sigmoid":
        (o_ref,) = refs
        o_ref[...] = jax.nn.sigmoid(acc).astype(o_ref.dtype)
    elif kind == "norm":
        gain_ref, o_ref = refs
        for sl in heads:
            o_ref[:, sl] = (_component_rms_norm(acc[:, sl], gain_ref[...]) * scale).astype(o_ref.dtype)
    elif kind == "norm_keep":
        gain_ref, *prev_refs, o_ref, fp_ref, fs_ref = refs
        normed = jnp.concatenate([_component_rms_norm(acc[:, sl], gain_ref[...]) for sl in heads], axis=1)
        _keep_f32(fp_ref, fs_ref, prev_refs, n_first, normed)
        o_ref[...] = normed.astype(o_ref.dtype)
    elif kind == "keep":
        *prev_refs, o_ref, fp_ref, fs_ref = refs
        _keep_f32(fp_ref, fs_ref, prev_refs, n_first, acc)
        o_ref[...] = acc.astype(o_ref.dtype)
    else:
        raise ValueError(kind)
    assert kind not in ("norm_keep", "keep") or len(prev_refs) == n_prev


def _segment(h, w_in, layer, seg0, nseg, tm, kind, *, scale=1.0, tables=(), gain=None, group_rows=None,
             prev_prompt=()):
    m, d = h.shape
    u = d // 2
    nheads = u // HEAD
    n_first = group_rows[0] // tm if group_rows else 0
    n_prev = len(prev_prompt)
    in_specs = [pl.BlockSpec((tm, d), lambda s, i: (i, 0)),
                pl.BlockSpec((None, d, u), lambda s, i: (layer, 0, seg0 + s))]
    args = [h, w_in]
    for tab in tables:
        in_specs.append(pl.BlockSpec((tm, HEAD), lambda s, i: (i, 0)))
        args.append(tab)
    if gain is not None:
        in_specs.append(pl.BlockSpec((None, 1, HEAD), lambda s, i: (layer, 0, 0)))
        args.append(gain)
    out_specs = [pl.BlockSpec((tm, u), lambda s, i: (i, s))]
    out_shape = [jax.ShapeDtypeStruct((m, nseg * u), h.dtype)]
    if group_rows:
        first, second = _group_specs(tm, u, n_first, row_axis=1)
        in_specs += [first] * n_prev
        args += list(prev_prompt)
        if n_prev:
            first = pl.BlockSpec((n_prev + 1, tm, u), lambda s, i: (0, jnp.minimum(i, n_first - 1), 0))
        out_specs += [first, second]
        out_shape += [jax.ShapeDtypeStruct(((n_prev + 1, group_rows[0], u) if n_prev else (group_rows[0], u)), F32),
                      jax.ShapeDtypeStruct((group_rows[1], u), F32)]
    hb = jnp.dtype(h.dtype).itemsize
    est = (2 * (tm * d * hb + d * u * 4 + tm * u * hb + (2 * n_prev + 2) * tm * u * 4 + 2 * tm * HEAD * 4)
           + d * u * hb + 4 * tm * u * 4)
    outs = pl.pallas_call(
        functools.partial(_segment_kernel, kind=kind, scale=scale, nheads=nheads, n_first=n_first, n_prev=n_prev),
        grid=(nseg, m // tm),
        in_specs=in_specs,
        out_specs=out_specs,
        out_shape=out_shape,
        scratch_shapes=[pltpu.VMEM((d, u), h.dtype)],
        compiler_params=_params(("arbitrary", "arbitrary"), est),
        name="input_projection_" + kind,
    )(*args)
    return outs if len(outs) > 1 else outs[0]


def _retention_kernel(q_ref, k_ref, v_ref, g_ref, s0_ref, d_ref, wq_ref, we_ref, dec_ref, o_ref, sout_ref, st_ref,
                      *, nheads, nblk):
    t = pl.program_id(1)

    @pl.when(t == 0)
    def _():
        st_ref[...] = s0_ref[...]

    lanes = [slice(h * HEAD, (h + 1) * HEAD) for h in range(nheads)]
    scores, inter = [], []
    for h, sl in enumerate(lanes):
        q, k, v = q_ref[:, sl], k_ref[:, sl], v_ref[:, sl]
        state = st_ref[h]
        scores.append(lax.dot_general(q, k, (((1,), (1,)), ((), ())), preferred_element_type=F32))
        inter.append(jnp.dot(q, state.astype(q.dtype), preferred_element_type=F32))
        kw = (k.astype(F32) * we_ref[:, sl]).astype(k.dtype)
        kv = lax.dot_general(kw, v, (((0,), (0,)), ((), ())), preferred_element_type=F32)
        st_ref[h] = state * dec_ref[h:h + 1, :] + kv
    for h, sl in enumerate(lanes):
        v = v_ref[:, sl]
        s = scores[h] * d_ref[h]
        o = jnp.dot(s.astype(v.dtype), v, preferred_element_type=F32) + wq_ref[:, sl] * inter[h]
        r = o * _rms_scale(o)
        o_ref[:, sl] = (r * g_ref[:, sl].astype(F32)).astype(o_ref.dtype)

    @pl.when(t == nblk - 1)
    def _():
        sout_ref[...] = st_ref[...]


def _retention_tables(t, nheads):
    log_g = jnp.log(1.0 - 2.0 ** (-5.0 - jnp.arange(nheads, dtype=F32)))
    idx = jnp.arange(t, dtype=F32)
    dist = jnp.abs(idx[:, None] - idx[None, :])
    ci = np.arange(t) // CHUNK
    visible = jnp.asarray(ci[None, :] <= ci[:, None])
    dmat = jnp.where(visible[None], jnp.exp(log_g[:, None, None] * dist[None]), 0.0)
    wq = jnp.exp(log_g[None, :] * (idx + 1.0)[:, None])
    we = jnp.exp(log_g[None, :] * (t - 1.0 - idx)[:, None])
    dec = jnp.exp(log_g * t)
    expand = lambda a: jnp.repeat(a, HEAD, axis=1)
    return dmat, expand(wq), expand(we), jnp.broadcast_to(dec[:, None], (nheads, HEAD))


def _retention(q, k, v, g, s0, t, nbatch, nblk, row_block0):
    nheads = s0.shape[1]
    u = nheads * HEAD
    dmat, wq, we, dec = _retention_tables(t, nheads)
    rows = pl.BlockSpec((t, u), lambda b, i: (row_block0 + b * nblk + i, 0))
    whole = lambda a: pl.BlockSpec(a.shape, lambda b, i: (0,) * a.ndim)
    state_spec = pl.BlockSpec((None, nheads, HEAD, HEAD), lambda b, i: (b, 0, 0, 0))
    pbytes = jnp.dtype(q.dtype).itemsize
    est = (2 * (5 * t * u * pbytes + 2 * nheads * HEAD * HEAD * 4 + nheads * t * t * 4 + 2 * t * u * 4)
           + nheads * HEAD * HEAD * 4 + 6 * t * max(t, HEAD) * 4)
    return pl.pallas_call(
        functools.partial(_retention_kernel, nheads=nheads, nblk=nblk),
        grid=(nbatch, nblk),
        in_specs=[rows, rows, rows, rows, state_spec, whole(dmat), whole(wq), whole(we), whole(dec)],
        out_specs=[
            pl.BlockSpec((t, u), lambda b, i: (b * nblk + i, 0)),
            state_spec,
        ],
        out_shape=[
            jax.ShapeDtypeStruct((nbatch * nblk * t, u), q.dtype),
            jax.ShapeDtypeStruct((nbatch, nheads, HEAD, HEAD), F32),
        ],
        scratch_shapes=[pltpu.VMEM((nheads, HEAD, HEAD), F32)],
        compiler_params=_params(("parallel", "arbitrary"), est),
        name="retention",
    )(q, k, v, g, s0, dmat, wq, we, dec)


def _bucket_thresholds():
    nb = N_BUCKETS // 2
    me = nb // 2
    out = []
    for k in range(1, nb - me):
        n = me
        while n ** (nb - me) * me ** k < me ** (nb - me) * MAX_DISTANCE ** k:
            n += 1
        out.append(n)
    return out


def _t5_bucket_np(rel):
    nb = N_BUCKETS // 2
    me = nb // 2
    n = np.abs(rel)
    large = np.full(rel.shape, me, np.int64)
    for thr in _bucket_thresholds():
        large += (n >= thr)
    large = np.minimum(large, nb - 1)
    return (np.where(rel > 0, nb, 0) + np.where(n < me, n, large)).astype(np.int32)


def _bias_kernel(rb_ref, idx_ref, mask_ref, o_ref):
    h = pl.program_id(0)
    idx = idx_ref[...]
    out = mask_ref[...]
    for b in range(N_BUCKETS):
        out = out + jnp.where(idx == b, rb_ref[b, h], 0.0)
    o_ref[...] = out


def _bias_table(rel_bias, qpos, kpos):
    nheads = rel_bias.shape[1]
    rel = kpos[None, :] - qpos[:, None]
    idx = jnp.asarray(_t5_bucket_np(rel))
    mask = jnp.asarray(np.where((kpos[None, :] // CHUNK) <= (qpos[:, None] // CHUNK), 0.0, MASK_VALUE)
                       .astype(np.float32))
    nq, nk = rel.shape
    return pl.pallas_call(
        _bias_kernel,
        grid=(nheads,),
        in_specs=[
            pl.BlockSpec(memory_space=pltpu.SMEM),
            pl.BlockSpec((nq, nk), lambda h: (0, 0)),
            pl.BlockSpec((nq, nk), lambda h: (0, 0)),
        ],
        out_specs=pl.BlockSpec((None, nq, nk), lambda h: (h, 0, 0)),
        out_shape=jax.ShapeDtypeStruct((nheads, nq, nk), F32),
        compiler_params=_params(("arbitrary",), 6 * nq * nk * 4),
        name="relative_bias_table",
    )(rel_bias, idx, mask)


def _stack_components(q):
    lo = lax.broadcasted_iota(jnp.int32, q.shape, 1) < HALF
    zero = jnp.zeros_like(q)
    return jnp.concatenate([jnp.where(lo, q, zero), jnp.where(lo, zero, q)], axis=0)


def _biased_scores(qs, k, bias):
    s = lax.dot_general(qs, k, (((1,), (1,)), ((), ())), preferred_element_type=F32)
    if bias.ndim == 2:
        t = bias.shape[0]
        return jnp.concatenate([s[:t] + bias, s[t:] + bias], axis=0), 0.0
    return s, bias


def _softmax_step(qs, k, v, bias, stats, g):
    m_ref, l_ref, acc_ref = stats
    s, c = _biased_scores(qs, k, bias)
    m_prev = m_ref[g]
    m_new = jnp.maximum(m_prev, jnp.max(s, axis=-1, keepdims=True) + c)
    alpha = jnp.exp(m_prev - m_new)
    p = jnp.exp(s - (m_new - c))
    l_ref[g] = alpha * l_ref[g] + jnp.sum(p, axis=-1, keepdims=True)
    acc_ref[g] = alpha * acc_ref[g] + jnp.dot(p.astype(v.dtype), v, preferred_element_type=F32)
    m_ref[g] = m_new


def _softmax_init(stats):
    m_ref, l_ref, acc_ref = stats
    m_ref[...] = jnp.full_like(m_ref, MASK_VALUE)
    l_ref[...] = jnp.zeros_like(l_ref)
    acc_ref[...] = jnp.zeros_like(acc_ref)


def _softmax_scratch(groups, t):
    return [pltpu.VMEM((groups, 2 * t, 1), F32), pltpu.VMEM((groups, 2 * t, 1), F32),
            pltpu.VMEM((groups, 2 * t, HEAD), F32)]


def _lambda_value(lam_ref, lam_init):
    a = lam_ref[...]
    e1 = jnp.exp(jnp.sum(a[0:1] * a[1:2], axis=-1, keepdims=True))
    e2 = jnp.exp(jnp.sum(a[2:3] * a[3:4], axis=-1, keepdims=True))
    return e1 - e2 + lam_init


def _diff_finish(t, lam, lam_init, subln, stats, g):
    _, l_ref, acc_ref = stats
    acc = acc_ref[g]
    l = l_ref[g]
    o = acc[:t] / l[:t] - lam * (acc[t:] / l[t:])
    return o * _rms_scale(o) * subln * (1.0 - lam_init)


def _attn_prompt_kernel(rb_ref, q_ref, k_ref, v_ref, bias_ref, lam_ref, sub_ref, o_ref, *stats,
                        t, groups, lam_init, far_bucket):
    hp = pl.program_id(1)
    qi = pl.program_id(2)
    m_ref, l_ref, acc_ref = stats
    lanes = [slice(g * HEAD, (g + 1) * HEAD) for g in range(groups)]
    qs = [_stack_components(q_ref[:, sl]) for sl in lanes]
    far_bias = [rb_ref[far_bucket, hp * groups + g] for g in range(groups)]
    _softmax_init(stats)

    def sweep(step):
        def run(j, bias_of):
            step(pl.ds(pl.multiple_of(j * t, t), t), [bias_of(g) for g in range(groups)])

        def far_step(j, carry):
            run(j, lambda g: far_bias[g])
            return carry

        lax.fori_loop(0, jnp.maximum(qi - 1, 0), far_step, 0)

        @pl.when(qi > 0)
        def _():
            run(qi - 1, lambda g: bias_ref[g, :, :t])

        run(qi, lambda g: bias_ref[g, :, t:])

    def max_step(rows, biases):
        scores = [_biased_scores(qs[g], k_ref[rows, lanes[g]], biases[g]) for g in range(groups)]
        for g, (s, c) in enumerate(scores):
            m_ref[g] = jnp.maximum(m_ref[g], jnp.max(s, axis=-1, keepdims=True) + c)

    def acc_step(rows, biases):
        scores = [_biased_scores(qs[g], k_ref[rows, lanes[g]], biases[g]) for g in range(groups)]
        probs = [jnp.exp(s - (m_ref[g] - c)) for g, (s, c) in enumerate(scores)]
        for g, p in enumerate(probs):
            l_ref[g] += jnp.sum(p, axis=-1, keepdims=True)
            v = v_ref[rows, lanes[g]]
            acc_ref[g] += jnp.dot(p.astype(v.dtype), v, preferred_element_type=F32)

    sweep(max_step)
    sweep(acc_step)

    lam = _lambda_value(lam_ref, lam_init)
    for g, sl in enumerate(lanes):
        o_ref[:, sl] = _diff_finish(t, lam, lam_init, sub_ref[...], stats, g).astype(o_ref.dtype)


def _attn_prompt(q, k, v, rel_bias, lam_params, subln2, layer, nbatch, seq, nheads, lam_init):
    u = nheads * HEAD
    t = _pick(seq, ATTN_BLOCKS)
    groups = _pick(nheads, HEADS_PER_STEP)
    assert t % CHUNK == 0 and t + 1 >= _bucket_thresholds()[-1]
    nq = seq // t
    w = groups * HEAD
    r = np.arange(t)
    bias = _bias_table(rel_bias, r + t, np.arange(2 * t))
    pbytes = jnp.dtype(q.dtype).itemsize
    est = (2 * (2 * t * w * pbytes + 2 * seq * w * pbytes + groups * 2 * t * t * 4)
           + groups * (2 * t * (HEAD + 2 * 128) * 4 + 6 * 2 * t * t * 4))
    return pl.pallas_call(
        functools.partial(_attn_prompt_kernel, t=t, groups=groups, lam_init=lam_init,
                          far_bucket=N_BUCKETS // 2 - 1),
        grid=(nbatch, nheads // groups, nq),
        in_specs=[
            pl.BlockSpec(memory_space=pltpu.SMEM),
            pl.BlockSpec((t, w), lambda b, h, i: (b * nq + i, h)),
            pl.BlockSpec((seq, w), lambda b, h, i: (b, h)),
            pl.BlockSpec((seq, w), lambda b, h, i: (b, h)),
            pl.BlockSpec((groups, t, 2 * t), lambda b, h, i: (h, 0, 0)),
            pl.BlockSpec((None, 4, HALF), lambda b, h, i: (layer, 0, 0)),
            pl.BlockSpec((None, 1, HEAD), lambda b, h, i: (layer, 0, 0)),
        ],
        out_specs=pl.BlockSpec((t, w), lambda b, h, i: (b * nq + i, h)),
        out_shape=jax.ShapeDtypeStruct((nbatch * seq, u), q.dtype),
        scratch_shapes=_softmax_scratch(groups, t),
        compiler_params=_params(("parallel", "parallel", "arbitrary"), est),
        name="diff_attention_prompt",
    )(rel_bias, q, k, v, bias, lam_params, subln2)


def _attn_sample_kernel(q_ref, kn_ref, vn_ref, kc_ref, vc_ref, bc_ref, bn_ref, lam_ref, sub_ref,
                        o_ref, *stats, t, tk, ncache, nheads, lam_init):
    j = pl.program_id(1)
    lanes = [slice(h * HEAD, (h + 1) * HEAD) for h in range(nheads)]

    @pl.when(j == 0)
    def _():
        _softmax_init(stats)

    qs = [_stack_components(q_ref[:, sl]) for sl in lanes]
    head_rows = lambda ref, h: ref[pl.ds(h, tk, stride=nheads), :].astype(qs[0].dtype)
    for h in range(nheads):
        _softmax_step(qs[h], head_rows(kc_ref, h), head_rows(vc_ref, h), bc_ref[h], stats, h)

    @pl.when(j == ncache - 1)
    def _():
        lam = _lambda_value(lam_ref, lam_init)
        for h, sl in enumerate(lanes):
            _softmax_step(qs[h], kn_ref[:, sl], vn_ref[:, sl], bn_ref[h], stats, h)
            o_ref[:, sl] = _diff_finish(t, lam, lam_init, sub_ref[...], stats, h).astype(o_ref.dtype)


def _attn_sample(q, k, v, cache_k, cache_v, rel_bias, lam_params, subln2, layer, nbatch, t, past,
                 row0, nheads, lam_init):
    u = nheads * HEAD
    assert row0 % t == 0
    rb0 = row0 // t
    tk = _pick(past, CACHE_TILES)
    qpos = past + np.arange(t)
    ncache = past // tk
    bias_c = _bias_table(rel_bias, qpos, np.arange(past))
    bias_c = bias_c.reshape(nheads, t, ncache, tk).transpose(2, 0, 1, 3)
    bias_n = _bias_table(rel_bias, qpos, qpos)
    pbytes = jnp.dtype(q.dtype).itemsize
    est = (2 * (4 * t * u * pbytes + 2 * tk * nheads * HEAD * 4 + nheads * t * (tk + t) * 4)
           + nheads * (2 * t * (HEAD + 2 * 128) * 4 + 6 * 2 * t * tk * 4))
    rows = pl.BlockSpec((t, u), lambda b, j: (rb0 + b, 0))
    cache = pl.BlockSpec((None, tk * nheads, HEAD), lambda b, j: (layer, b * ncache + j, 0))
    return pl.pallas_call(
        functools.partial(_attn_sample_kernel, t=t, tk=tk, ncache=ncache, nheads=nheads, lam_init=lam_init),
        grid=(nbatch, ncache),
        in_specs=[
            rows, rows, rows, cache, cache,
            pl.BlockSpec((None, nheads, t, tk), lambda b, j: (j, 0, 0, 0)),
            pl.BlockSpec((nheads, t, t), lambda b, j: (0, 0, 0)),
            pl.BlockSpec((None, 4, HALF), lambda b, j: (layer, 0, 0)),
            pl.BlockSpec((None, 1, HEAD), lambda b, j: (layer, 0, 0)),
        ],
        out_specs=pl.BlockSpec((t, u), lambda b, j: (b, 0)),
        out_shape=jax.ShapeDtypeStruct((nbatch * t, u), q.dtype),
        scratch_shapes=_softmax_scratch(nheads, t),
        compiler_params=_params(("parallel", "arbitrary"), est),
        name="diff_attention_sample",
    )(q, k, v, cache_k, cache_v, bias_c, bias_n, lam_params, subln2)


def _merge_kernel(ap_ref, as_ref, bp_ref, bs_ref, ga0_ref, ga1_ref, gb0_ref, gb1_ref, wa_ref, wb_ref, o_ref,
                  *, u, n_first):
    a = _group_tile((ap_ref, as_ref), n_first)
    b = _group_tile((bp_ref, bs_ref), n_first)
    ya = jnp.dot(a, wa_ref[...], preferred_element_type=F32)
    yb = jnp.dot(b, wb_ref[...], preferred_element_type=F32)
    for c, (ga, gb) in enumerate(((ga0_ref, gb0_ref), (ga1_ref, gb1_ref))):
        sl = slice(c * u, (c + 1) * u)
        o_ref[:, sl] = (ga[...].astype(F32) * ya[:, sl] + gb[...].astype(F32) * yb[:, sl]).astype(o_ref.dtype)


def _merge(ret_outs, dif_outs, gates, w_ret_up, w_dif_up, layer, tm):
    m = gates.shape[0]
    u = ret_outs[0].shape[1]
    d = 2 * u
    n_first = ret_outs[0].shape[0] // tm
    gate = lambda c: pl.BlockSpec((tm, u), lambda i: (i, c))
    rows = _group_specs(tm, u, n_first)
    wspec = pl.BlockSpec((None, u, d), lambda i: (layer, 0, 0))
    pbytes = jnp.dtype(gates.dtype).itemsize
    est = 2 * (8 * tm * u * pbytes + 2 * u * d * pbytes + tm * d * pbytes) + 3 * tm * d * 4
    return pl.pallas_call(
        functools.partial(_merge_kernel, u=u, n_first=n_first),
        grid=(m // tm,),
        in_specs=rows + rows + [gate(0), gate(1), gate(2), gate(3), wspec, wspec],
        out_specs=pl.BlockSpec((tm, d), lambda i: (i, 0)),
        out_shape=jax.ShapeDtypeStruct((m, d), gates.dtype),
        compiler_params=_params(("arbitrary",), est),
        name="gated_merge",
    )(*ret_outs, *dif_outs, gates, gates, gates, gates, w_ret_up, w_dif_up)


def _out_proj_kernel(x_ref, a_ref, w_ref, o_ref):
    o_ref[...] = x_ref[...] + jnp.dot(a_ref[...], w_ref[...], preferred_element_type=F32)


def _out_proj(x, merged, w_out, layer, tm):
    m, d = x.shape
    wbytes = jnp.dtype(w_out.dtype).itemsize
    est = 2 * (2 * tm * d * 4 + tm * d * wbytes + d * d * wbytes) + tm * d * 4
    return pl.pallas_call(
        _out_proj_kernel,
        grid=(m // tm,),
        in_specs=[
            pl.BlockSpec((tm, d), lambda i: (i, 0)),
            pl.BlockSpec((tm, d), lambda i: (i, 0)),
            pl.BlockSpec((None, d, d), lambda i: (layer, 0, 0)),
        ],
        out_specs=pl.BlockSpec((tm, d), lambda i: (i, 0)),
        out_shape=jax.ShapeDtypeStruct((m, d), F32),
        compiler_params=_params(("parallel",), est),
        name="output_projection",
    )(x, merged, w_out)


def _rotary_tables(pos):
    inv = ROPE_BASE ** (-jnp.arange(HALF, dtype=F32) / HALF)
    ang = pos.astype(F32)[:, None] * inv[None, :]
    cos, sin = jnp.cos(ang), jnp.sin(ang)
    return jnp.concatenate([cos, cos], axis=-1), jnp.concatenate([-sin, sin], axis=-1)


def kernel(x_prompt, x_sample, cache_diff_k, cache_diff_v, state_ret, ffn1_norm, ffn1_gate, ffn1_up, ffn1_down, mix_norm, w_in, q_norm, k_norm, lambda_q1, lambda_k1, lambda_q2, lambda_k2, subln, w_ret_up, w_dif_up, w_out, ffn2_norm, ffn2_gate, ffn2_up, ffn2_down, rel_bias):
    nb, seq, d = x_prompt.shape
    db, dseq, _ = x_sample.shape
    depth, _, past, nh_d, _ = cache_diff_k.shape
    nh_r = state_ret.shape[2]
    u = d // 2
    assert nh_r * HEAD == u and nh_d * HEAD == u and dseq == CHUNK and seq % CHUNK == 0
    assert w_in.shape[-1] == N_SEG * u
    mp, ms = nb * seq, db * dseq

    cast = lambda w: w.astype(MXU_DTYPE)
    tf = _pick(ffn1_gate.shape[-1], FF_TILES)

    def chunked(w):
        dep, din, ff = w.shape
        return cast(w.reshape(dep, din, ff // tf, tf).transpose(0, 2, 1, 3))

    wg1, wu1, wd1 = chunked(ffn1_gate), chunked(ffn1_up), cast(ffn1_down)
    wg2, wu2, wd2 = chunked(ffn2_gate), chunked(ffn2_up), cast(ffn2_down)
    w_ret_c, w_dif_c, w_out_c = cast(w_ret_up), cast(w_dif_up), cast(w_out)
    row3 = lambda g: g.reshape(depth, 1, g.shape[-1])
    n1, nmix, n2 = row3(ffn1_norm), row3(mix_norm), row3(ffn2_norm)
    qn2 = row3(jnp.concatenate([q_norm, q_norm], axis=-1))
    kn2 = row3(jnp.concatenate([k_norm, k_norm], axis=-1))
    subln2 = row3(subln)
    lam_params = jnp.stack([lambda_q1, lambda_k1, lambda_q2, lambda_k2], axis=1)
    cache_k = cache_diff_k.reshape(depth, db * past * nh_d, HEAD)
    cache_v = cache_diff_v.reshape(depth, db * past * nh_d, HEAD)

    pos = jnp.concatenate([jnp.tile(jnp.arange(seq, dtype=jnp.int32), nb),
                           past + jnp.tile(jnp.arange(dseq, dtype=jnp.int32), db)])
    rot = _rotary_tables(pos)

    tm = _pick(math.gcd(mp, ms), ROW_TILES)
    tm_proj = _pick(mp + ms, PROJ_ROW_TILES)
    groups = (mp, ms)
    zero_state = jnp.zeros((nb, nh_r, HEAD, HEAD), F32)
    t_ret = _pick(seq, RET_BLOCKS)

    x = (x_prompt.reshape(mp, d), x_sample.reshape(ms, d))
    kps, kss, vps, vss, states_p, states_s = [], [], [], [], [], []
    for l in range(depth):
        lam_init = 0.8 - 0.6 * math.exp(-0.3 * l)
        x, h = _ffn(x, n1, wg1, wu1, wd1, l, tm, next_gain=nmix)

        last = l == depth - 1
        seg = functools.partial(_segment, h, w_in, l)
        rq = seg(SEG_RQ, 1, tm_proj, "rotary", tables=rot)
        rk = seg(SEG_RK, 1, tm_proj, "rotary", tables=rot, scale=HEAD ** -0.5)
        rv = seg(SEG_RV, 1, tm_proj, "cast")
        rg = seg(SEG_RG, 1, tm_proj, "silu")
        dq = seg(SEG_DQ, 1, tm_proj, "norm", gain=qn2, scale=HALF ** -0.5)
        dk, k_p, k_s = seg(SEG_DK, 1, tm, "norm_keep", gain=kn2, group_rows=groups,
                           prev_prompt=kps if last else ())
        dv, v_p, v_s = seg(SEG_DV, 1, tm, "keep", group_rows=groups, prev_prompt=vps if last else ())
        gates = seg(SEG_GATES, N_SEG - SEG_GATES, tm_proj, "sigmoid")

        ret_p, st_p = _retention(rq, rk, rv, rg, zero_state, t_ret, nb, seq // t_ret, 0)
        ret_s, st_s = _retention(rq, rk, rv, rg, state_ret[l].astype(F32), dseq, db, 1, mp // dseq)

        dif_p = _attn_prompt(dq, dk, dv, rel_bias, lam_params, subln2, l, nb, seq, nh_d, lam_init)
        dif_s = _attn_sample(dq, dk, dv, cache_k, cache_v, rel_bias, lam_params, subln2, l, db, dseq, past,
                             mp, nh_d, lam_init)

        merged = _merge((ret_p, ret_s), (dif_p, dif_s), gates, w_ret_c, w_dif_c, l, tm)
        x = _out_proj(x, merged, w_out_c, l, tm)
        x = _ffn(x, n2, wg2, wu2, wd2, l, tm, split_out=groups if l == depth - 1 else None)

        for acc, val in ((kps, k_p), (kss, k_s), (vps, v_p), (vss, v_s), (states_p, st_p), (states_s, st_s)):
            acc.append(val)

    y_p, y_s = x
    kv_p = lambda parts: parts[-1].reshape(depth, nb, seq, nh_d, HEAD)
    kv_s = lambda parts: jnp.stack(parts).reshape(depth, db, dseq, nh_d, HEAD)
    return (y_p.reshape(nb, seq, d), y_s.reshape(db, dseq, d),
            kv_p(kps).astype(cache_diff_k.dtype), kv_p(vps).astype(cache_diff_v.dtype),
            jnp.stack(states_p).astype(state_ret.dtype),
            kv_s(kss).astype(cache_diff_k.dtype), kv_s(vss).astype(cache_diff_v.dtype),
            jnp.stack(states_s).astype(state_ret.dtype))
```

```python
import functools
import math

import numpy as np
import jax
import jax.numpy as jnp
from jax import lax
from jax.experimental import pallas as pl
from jax.experimental.pallas import tpu as pltpu

F32 = jnp.float32
MXU_DTYPE = jnp.bfloat16

CHUNK = 64
HEAD = 128
HALF = HEAD // 2
ROPE_BASE = 10000.0
N_BUCKETS = 32
MAX_DISTANCE = 128
EPS = 1e-6
MASK_VALUE = -1e30
SCORE_BOUND_MARGIN = 1.03
MAX_SHIFT_BRACKET = 100.0

VMEM_LIMIT_CAP = 60 * 1024 * 1024
MIB = 1024 * 1024

ROW_TILES = (512, 256, 128, 64)
PROJ_ROW_TILES = (1536, 1024, 768, 512, 256, 128, 64)
FF_TILES = (512, 256, 128)
FF_CHUNKS_PER_STEP = 2
ATTN_BLOCKS = (512, 256, 128)
HEADS_PER_STEP = (4, 2, 1)
RET_BLOCKS = (256, 128, 64)
CACHE_TILES = (2048, 1024, 512, 256, 128, 64)


def _pick(n, prefs):
    for p in prefs:
        if n % p == 0:
            return p
    raise ValueError(f"no tile in {prefs} divides {n}")


def _params(semantics, est_bytes):
    limit = int(min(max(est_bytes + 8 * MIB, 32 * MIB), VMEM_LIMIT_CAP))
    return pltpu.CompilerParams(dimension_semantics=semantics, vmem_limit_bytes=limit)


def _rms_scale(x):
    return lax.rsqrt(jnp.mean(x * x, axis=-1, keepdims=True) + EPS)


def _group_specs(tm, width, n_first, row_axis=0):
    first = pl.BlockSpec((tm, width), lambda *g: (jnp.minimum(g[row_axis], n_first - 1), 0))
    second = pl.BlockSpec((tm, width), lambda *g: (jnp.maximum(g[row_axis] - n_first, 0), 0),
                          pipeline_mode=pl.Buffered(1))
    return [first, second]


def _group_tile(refs, n_first, row_axis=0):
    if len(refs) == 1:
        return refs[0][...]
    return jnp.where(pl.program_id(row_axis) < n_first, refs[0][...], refs[1][...])


def _store_group_tile(refs, n_first, value, row_axis=0):
    if len(refs) == 1:
        refs[0][...] = value
        return
    i = pl.program_id(row_axis)

    @pl.when(i < n_first)
    def _():
        refs[0][...] = value

    @pl.when(i >= n_first)
    def _():
        refs[1][...] = value


def _ffn_kernel(*refs, nf, per_step, n_in, n_out, n_first, norm_out):
    x_refs = refs[:n_in]
    g_ref = refs[n_in]
    pos = n_in + 1
    w_refs = [refs[pos + 3 * c:pos + 3 * c + 3] for c in range(per_step)]
    pos += 3 * per_step
    g2_ref = refs[pos] if norm_out else None
    pos += int(norm_out)
    o_refs = refs[pos:pos + n_out]
    pos += n_out
    hn_ref = refs[pos] if norm_out else None
    pos += int(norm_out)
    h_ref, acc_ref = refs[pos:]
    f = pl.program_id(1)
    nsteps = -(-nf // per_step)

    @pl.when(f == 0)
    def _():
        x = _group_tile(x_refs, n_first)
        h_ref[...] = (x * _rms_scale(x) * g_ref[...]).astype(h_ref.dtype)
        acc_ref[...] = jnp.zeros_like(acc_ref)

    def chunks(slots):
        h = h_ref[...]
        pre = [(jnp.dot(h, wg[...], preferred_element_type=F32), jnp.dot(h, wu[...], preferred_element_type=F32))
               for wg, wu, _ in slots]
        total = acc_ref[...]
        for (gate, up), (_, _, wd) in zip(pre, slots):
            act = (gate * jax.nn.sigmoid(gate) * up).astype(wd.dtype)
            total = total + jnp.dot(act, wd[...], preferred_element_type=F32)
        acc_ref[...] = total

    head = nf - (nsteps - 1) * per_step
    if head == per_step:
        chunks(w_refs)
    else:
        @pl.when(f == 0)
        def _():
            chunks(w_refs[:head])

        @pl.when(f > 0)
        def _():
            chunks(w_refs)

    @pl.when(f == nsteps - 1)
    def _():
        y = _group_tile(x_refs, n_first) + 0.5 * acc_ref[...]
        _store_group_tile(o_refs, n_first, y)
        if norm_out:
            hn_ref[...] = (y * _rms_scale(y) * g2_ref[...]).astype(hn_ref.dtype)


def _ffn(xs, gain, wg, wu, wd, layer, tm, split_out=None, next_gain=None):
    xs = tuple(xs) if isinstance(xs, (tuple, list)) else (xs,)
    m = sum(x.shape[0] for x in xs)
    d = xs[0].shape[1]
    n_first = (xs[0].shape[0] if len(xs) == 2 else split_out[0] if split_out else m) // tm
    ff = wg.shape[-1]
    tf = _pick(ff, FF_TILES)
    nf = ff // tf
    norm_out = next_gain is not None
    wbytes = jnp.dtype(wg.dtype).itemsize
    row_buffers = (3 if len(xs) == 2 else 2) + (3 if split_out else 2)
    vmem_need = lambda chunks: (
        row_buffers * tm * d * 4 + 2 * (3 * chunks * d * tf * wbytes + int(norm_out) * tm * d * wbytes)
        + tm * d * (4 + wbytes) + 2 * chunks * tm * tf * 4)
    per_step = min(FF_CHUNKS_PER_STEP, nf)
    while per_step > 1 and vmem_need(per_step) > VMEM_LIMIT_CAP:
        per_step -= 1
    nsteps = -(-nf // per_step)
    est = vmem_need(per_step)
    rows = lambda: pl.BlockSpec((tm, d), lambda i, f: (i, 0))
    gain_spec = pl.BlockSpec((None, 1, d), lambda i, f: (layer, 0, 0))
    in_specs = (_group_specs(tm, d, n_first) if len(xs) == 2 else [rows()]) + [gain_spec]
    args = [*xs, gain]
    head = nf - (nsteps - 1) * per_step
    for c in range(per_step):
        chunk = lambda f, c=c: jnp.where(f == 0, min(c, head - 1), head + (f - 1) * per_step + c)
        in_specs += [pl.BlockSpec((None, d, tf), lambda i, f, chunk=chunk: (layer, 0, chunk(f))),
                     pl.BlockSpec((None, d, tf), lambda i, f, chunk=chunk: (layer, 0, chunk(f))),
                     pl.BlockSpec((None, tf, d), lambda i, f, chunk=chunk: (layer, chunk(f), 0))]
        args += [wg, wu, wd]
    if split_out:
        out_specs = _group_specs(tm, d, n_first)
        out_shape = [jax.ShapeDtypeStruct((r, d), F32) for r in split_out]
    else:
        out_specs = [rows()]
        out_shape = [jax.ShapeDtypeStruct((m, d), F32)]
    if norm_out:
        in_specs.append(gain_spec)
        args.append(next_gain)
        out_specs.append(rows())
        out_shape.append(jax.ShapeDtypeStruct((m, d), wg.dtype))
    outs = pl.pallas_call(
        functools.partial(_ffn_kernel, nf=nf, per_step=per_step, n_in=len(xs), n_out=2 if split_out else 1,
                          n_first=n_first, norm_out=norm_out),
        grid=(m // tm, nsteps),
        in_specs=in_specs,
        out_specs=out_specs,
        out_shape=out_shape,
        scratch_shapes=[pltpu.VMEM((tm, d), wg.dtype), pltpu.VMEM((tm, d), F32)],
        compiler_params=_params(("arbitrary", "arbitrary"), est),
        name="swiglu_half_step",
    )(*args)
    return outs if len(outs) > 1 else outs[0]


SEG_RQ, SEG_RK, SEG_RV, SEG_RG, SEG_DQ, SEG_DK, SEG_DV, SEG_GATES, N_SEG = 0, 1, 2, 3, 4, 5, 6, 7, 11


def _rotate_half_pairs(a, cos2, sin2):
    return a * cos2 + pltpu.roll(a, HALF, 1) * sin2


def _component_rms_norm(a, gain2):
    lo = lax.broadcasted_iota(jnp.int32, a.shape, 1) < HALF
    sq = a * a
    s_all = jnp.sum(sq, axis=-1, keepdims=True)
    s_lo = jnp.sum(jnp.where(lo, sq, 0.0), axis=-1, keepdims=True)
    ms = jnp.where(lo, s_lo, s_all - s_lo) * (1.0 / HALF)
    return a * lax.rsqrt(ms + EPS) * gain2


def _keep_f32(fp_ref, fs_ref, prev_refs, n_first, value):
    i = pl.program_id(1)

    @pl.when(i < n_first)
    def _():
        if prev_refs:
            for l, prev in enumerate(prev_refs):
                fp_ref[l] = prev[...]
            fp_ref[len(prev_refs)] = value
        else:
            fp_ref[...] = value

    @pl.when(i >= n_first)
    def _():
        fs_ref[...] = value


def _segment_kernel(h_ref, w_ref, *refs, kind, scale, nheads, n_first, n_prev):
    *refs, wc_ref = refs

    @pl.when(pl.program_id(1) == 0)
    def _():
        wc_ref[...] = w_ref[...].astype(wc_ref.dtype)

    acc = jnp.dot(h_ref[...], wc_ref[...], preferred_element_type=F32)
    heads = [slice(h * HEAD, (h + 1) * HEAD) for h in range(nheads)]
    if kind == "rotary":
        cos_ref, sin_ref, o_ref = refs
        for sl in heads:
            r = _rotate_half_pairs(acc[:, sl], cos_ref[...], sin_ref[...])
            o_ref[:, sl] = (r if scale == 1.0 else r * scale).astype(o_ref.dtype)
    elif kind == "cast":
        (o_ref,) = refs
        o_ref[...] = acc.astype(o_ref.dtype)
    elif kind == "silu":
        (o_ref,) = refs
        o_ref[...] = (acc * jax.nn.sigmoid(acc)).astype(o_ref.dtype)
    elif kind == "sigmoid":
        (o_ref,) = refs
        o_ref[...] = jax.nn.sigmoid(acc).astype(o_ref.dtype)
    elif kind == "norm":
        gain_ref, o_ref = refs
        for sl in heads:
            o_ref[:, sl] = (_component_rms_norm(acc[:, sl], gain_ref[...]) * scale).astype(o_ref.dtype)
    elif kind == "norm_keep":
        gain_ref, *prev_refs, o_ref, fp_ref, fs_ref = refs
        normed = jnp.concatenate([_component_rms_norm(acc[:, sl], gain_ref[...]) for sl in heads], axis=1)
        _keep_f32(fp_ref, fs_ref, prev_refs, n_first, normed)
        o_ref[...] = normed.astype(o_ref.dtype)
    elif kind == "keep":
        *prev_refs, o_ref, fp_ref, fs_ref = refs
        _keep_f32(fp_ref, fs_ref, prev_refs, n_first, acc)
        o_ref[...] = acc.astype(o_ref.dtype)
    else:
        raise ValueError(kind)
    assert kind not in ("norm_keep", "keep") or len(prev_refs) == n_prev


def _segment(h, w_in, layer, seg0, nseg, tm, kind, *, scale=1.0, tables=(), gain=None, group_rows=None,
             prev_prompt=()):
    m, d = h.shape
    u = d // 2
    nheads = u // HEAD
    n_first = group_rows[0] // tm if group_rows else 0
    n_prev = len(prev_prompt)
    in_specs = [pl.BlockSpec((tm, d), lambda s, i: (i, 0)),
                pl.BlockSpec((None, d, u), lambda s, i: (layer, 0, seg0 + s))]
    args = [h, w_in]
    for tab in tables:
        in_specs.append(pl.BlockSpec((tm, HEAD), lambda s, i: (i, 0)))
        args.append(tab)
    if gain is not None:
        in_specs.append(pl.BlockSpec((None, 1, HEAD), lambda s, i: (layer, 0, 0)))
        args.append(gain)
    out_specs = [pl.BlockSpec((tm, u), lambda s, i: (i, s))]
    out_shape = [jax.ShapeDtypeStruct((m, nseg * u), h.dtype)]
    if group_rows:
        first, second = _group_specs(tm, u, n_first, row_axis=1)
        in_specs += [first] * n_prev
        args += list(prev_prompt)
        if n_prev:
            first = pl.BlockSpec((n_prev + 1, tm, u), lambda s, i: (0, jnp.minimum(i, n_first - 1), 0))
        out_specs += [first, second]
        out_shape += [jax.ShapeDtypeStruct(((n_prev + 1, group_rows[0], u) if n_prev else (group_rows[0], u)), F32),
                      jax.ShapeDtypeStruct((group_rows[1], u), F32)]
    hb = jnp.dtype(h.dtype).itemsize
    est = (2 * (tm * d * hb + d * u * 4 + tm * u * hb + (2 * n_prev + 2) * tm * u * 4 + 2 * tm * HEAD * 4)
           + d * u * hb + 4 * tm * u * 4)
    outs = pl.pallas_call(
        functools.partial(_segment_kernel, kind=kind, scale=scale, nheads=nheads, n_first=n_first, n_prev=n_prev),
        grid=(nseg, m // tm),
        in_specs=in_specs,
        out_specs=out_specs,
        out_shape=out_shape,
        scratch_shapes=[pltpu.VMEM((d, u), h.dtype)],
        compiler_params=_params(("arbitrary", "arbitrary"), est),
        name="input_projection_" + kind,
    )(*args)
    return outs if len(outs) > 1 else outs[0]


def _retention_kernel(q_ref, k_ref, v_ref, g_ref, s0_ref, d_ref, wq_ref, we_ref, dec_ref, o_ref, sout_ref, st_ref,
                      *, nheads, nblk):
    t = pl.program_id(1)

    @pl.when(t == 0)
    def _():
        st_ref[...] = s0_ref[...]

    lanes = [slice(h * HEAD, (h + 1) * HEAD) for h in range(nheads)]
    scores, inter = [], []
    for h, sl in enumerate(lanes):
        q, k, v = q_ref[:, sl], k_ref[:, sl], v_ref[:, sl]
        state = st_ref[h]
        scores.append(lax.dot_general(q, k, (((1,), (1,)), ((), ())), preferred_element_type=F32))
        inter.append(jnp.dot(q, state.astype(q.dtype), preferred_element_type=F32))
        kw = (k.astype(F32) * we_ref[:, sl]).astype(k.dtype)
        kv = lax.dot_general(kw, v, (((0,), (0,)), ((), ())), preferred_element_type=F32)
        st_ref[h] = state * dec_ref[h:h + 1, :] + kv
    for h, sl in enumerate(lanes):
        v = v_ref[:, sl]
        s = scores[h] * d_ref[h]
        o = jnp.dot(s.astype(v.dtype), v, preferred_element_type=F32) + wq_ref[:, sl] * inter[h]
        r = o * _rms_scale(o)
        o_ref[:, sl] = (r * g_ref[:, sl].astype(F32)).astype(o_ref.dtype)

    @pl.when(t == nblk - 1)
    def _():
        sout_ref[...] = st_ref[...]


def _retention_tables(t, nheads):
    log_g = jnp.log(1.0 - 2.0 ** (-5.0 - jnp.arange(nheads, dtype=F32)))
    idx = jnp.arange(t, dtype=F32)
    dist = jnp.abs(idx[:, None] - idx[None, :])
    ci = np.arange(t) // CHUNK
    visible = jnp.asarray(ci[None, :] <= ci[:, None])
    dmat = jnp.where(visible[None], jnp.exp(log_g[:, None, None] * dist[None]), 0.0)
    wq = jnp.exp(log_g[None, :] * (idx + 1.0)[:, None])
    we = jnp.exp(log_g[None, :] * (t - 1.0 - idx)[:, None])
    dec = jnp.exp(log_g * t)
    expand = lambda a: jnp.repeat(a, HEAD, axis=1)
    return dmat, expand(wq), expand(we), jnp.broadcast_to(dec[:, None], (nheads, HEAD))


def _retention(q, k, v, g, s0, t, nbatch, nblk, row_block0):
    nheads = s0.shape[1]
    u = nheads * HEAD
    dmat, wq, we, dec = _retention_tables(t, nheads)
    rows = pl.BlockSpec((t, u), lambda b, i: (row_block0 + b * nblk + i, 0))
    whole = lambda a: pl.BlockSpec(a.shape, lambda b, i: (0,) * a.ndim)
    state_spec = pl.BlockSpec((None, nheads, HEAD, HEAD), lambda b, i: (b, 0, 0, 0))
    pbytes = jnp.dtype(q.dtype).itemsize
    est = (2 * (5 * t * u * pbytes + 2 * nheads * HEAD * HEAD * 4 + nheads * t * t * 4 + 2 * t * u * 4)
           + nheads * HEAD * HEAD * 4 + 6 * t * max(t, HEAD) * 4)
    return pl.pallas_call(
        functools.partial(_retention_kernel, nheads=nheads, nblk=nblk),
        grid=(nbatch, nblk),
        in_specs=[rows, rows, rows, rows, state_spec, whole(dmat), whole(wq), whole(we), whole(dec)],
        out_specs=[
            pl.BlockSpec((t, u), lambda b, i: (b * nblk + i, 0)),
            state_spec,
        ],
        out_shape=[
            jax.ShapeDtypeStruct((nbatch * nblk * t, u), q.dtype),
            jax.ShapeDtypeStruct((nbatch, nheads, HEAD, HEAD), F32),
        ],
        scratch_shapes=[pltpu.VMEM((nheads, HEAD, HEAD), F32)],
        compiler_params=_params(("parallel", "arbitrary"), est),
        name="retention",
    )(q, k, v, g, s0, dmat, wq, we, dec)


def _bucket_thresholds():
    nb = N_BUCKETS // 2
    me = nb // 2
    out = []
    for k in range(1, nb - me):
        n = me
        while n ** (nb - me) * me ** k < me ** (nb - me) * MAX_DISTANCE ** k:
            n += 1
        out.append(n)
    return out


def _t5_bucket_np(rel):
    nb = N_BUCKETS // 2
    me = nb // 2
    n = np.abs(rel)
    large = np.full(rel.shape, me, np.int64)
    for thr in _bucket_thresholds():
        large += (n >= thr)
    large = np.minimum(large, nb - 1)
    return (np.where(rel > 0, nb, 0) + np.where(n < me, n, large)).astype(np.int32)


def _bias_kernel(rb_ref, idx_ref, mask_ref, o_ref):
    h = pl.program_id(0)
    idx = idx_ref[...]
    out = mask_ref[...]
    for b in range(N_BUCKETS):
        out = out + jnp.where(idx == b, rb_ref[b, h], 0.0)
    o_ref[...] = out


def _bias_table(rel_bias, qpos, kpos):
    nheads = rel_bias.shape[1]
    rel = kpos[None, :] - qpos[:, None]
    idx = jnp.asarray(_t5_bucket_np(rel))
    mask = jnp.asarray(np.where((kpos[None, :] // CHUNK) <= (qpos[:, None] // CHUNK), 0.0, MASK_VALUE)
                       .astype(np.float32))
    nq, nk = rel.shape
    return pl.pallas_call(
        _bias_kernel,
        grid=(nheads,),
        in_specs=[
            pl.BlockSpec(memory_space=pltpu.SMEM),
            pl.BlockSpec((nq, nk), lambda h: (0, 0)),
            pl.BlockSpec((nq, nk), lambda h: (0, 0)),
        ],
        out_specs=pl.BlockSpec((None, nq, nk), lambda h: (h, 0, 0)),
        out_shape=jax.ShapeDtypeStruct((nheads, nq, nk), F32),
        compiler_params=_params(("arbitrary",), 6 * nq * nk * 4),
        name="relative_bias_table",
    )(rel_bias, idx, mask)


def _stack_components(q):
    lo = lax.broadcasted_iota(jnp.int32, q.shape, 1) < HALF
    zero = jnp.zeros_like(q)
    return jnp.concatenate([jnp.where(lo, q, zero), jnp.where(lo, zero, q)], axis=0)


def _biased_scores(qs, k, bias):
    s = lax.dot_general(qs, k, (((1,), (1,)), ((), ())), preferred_element_type=F32)
    if bias.ndim == 2:
        t = bias.shape[0]
        return jnp.concatenate([s[:t] + bias, s[t:] + bias], axis=0), 0.0
    return s, bias


def _softmax_step(qs, k, v, bias, stats, g):
    m_ref, l_ref, acc_ref = stats
    s, c = _biased_scores(qs, k, bias)
    m_prev = m_ref[g]
    m_new = jnp.maximum(m_prev, jnp.max(s, axis=-1, keepdims=True) + c)
    alpha = jnp.exp(m_prev - m_new)
    p = jnp.exp(s - (m_new - c))
    l_ref[g] = alpha * l_ref[g] + jnp.sum(p, axis=-1, keepdims=True)
    acc_ref[g] = alpha * acc_ref[g] + jnp.dot(p.astype(v.dtype), v, preferred_element_type=F32)
    m_ref[g] = m_new


def _softmax_init(stats):
    m_ref, l_ref, acc_ref = stats
    m_ref[...] = jnp.full_like(m_ref, MASK_VALUE)
    l_ref[...] = jnp.zeros_like(l_ref)
    acc_ref[...] = jnp.zeros_like(acc_ref)


def _softmax_scratch(groups, t):
    return [pltpu.VMEM((groups, 2 * t, 1), F32), pltpu.VMEM((groups, 2 * t, 1), F32),
            pltpu.VMEM((groups, 2 * t, HEAD), F32)]


def _lambda_value(lam_ref, lam_init):
    a = lam_ref[...]
    e1 = jnp.exp(jnp.sum(a[0:1] * a[1:2], axis=-1, keepdims=True))
    e2 = jnp.exp(jnp.sum(a[2:3] * a[3:4], axis=-1, keepdims=True))
    return e1 - e2 + lam_init


def _diff_finish(t, lam, lam_init, subln, stats, g):
    _, l_ref, acc_ref = stats
    acc = acc_ref[g]
    l = l_ref[g]
    o = acc[:t] / l[:t] - lam * (acc[t:] / l[t:])
    return o * _rms_scale(o) * subln * (1.0 - lam_init)


def _attn_prompt_kernel(rb_ref, q_ref, k_ref, v_ref, bias_ref, lam_ref, sub_ref, qg_ref, kg_ref, o_ref, *stats,
                        t, groups, lam_init, far_bucket):
    hp = pl.program_id(1)
    qi = pl.program_id(2)
    m_ref, l_ref, acc_ref = stats
    lanes = [slice(g * HEAD, (g + 1) * HEAD) for g in range(groups)]
    qs = [_stack_components(q_ref[:, sl]) for sl in lanes]
    far_bias = [rb_ref[far_bucket, hp * groups + g] for g in range(groups)]
    _softmax_init(stats)

    def sweep(step):
        def run(j, bias_of):
            step(pl.ds(pl.multiple_of(j * t, t), t), [bias_of(g) for g in range(groups)])

        def far_step(j, carry):
            run(j, lambda g: far_bias[g])
            return carry

        lax.fori_loop(0, jnp.maximum(qi - 1, 0), far_step, 0)

        @pl.when(qi > 0)
        def _():
            run(qi - 1, lambda g: bias_ref[g, :, :t])

        run(qi, lambda g: bias_ref[g, :, t:])

    def max_step(rows, biases):
        scores = [_biased_scores(qs[g], k_ref[rows, lanes[g]], biases[g]) for g in range(groups)]
        for g, (s, c) in enumerate(scores):
            m_ref[g] = jnp.maximum(m_ref[g], jnp.max(s, axis=-1, keepdims=True) + c)

    def acc_step(rows, biases):
        scores = [_biased_scores(qs[g], k_ref[rows, lanes[g]], biases[g]) for g in range(groups)]
        probs = [jnp.exp(s - (m_ref[g] - c)) for g, (s, c) in enumerate(scores)]
        for g, p in enumerate(probs):
            l_ref[g] += jnp.sum(p, axis=-1, keepdims=True)
            v = v_ref[rows, lanes[g]]
            acc_ref[g] += jnp.dot(p.astype(v.dtype), v, preferred_element_type=F32)

    gain_bound = (SCORE_BOUND_MARGIN * HALF ** 0.5 * jnp.max(jnp.abs(qg_ref[...]), axis=-1, keepdims=True)
                  * jnp.max(jnp.abs(kg_ref[...]), axis=-1, keepdims=True))
    own_rows = pl.ds(pl.multiple_of(qi * t, t), t)
    widest = jnp.zeros((1, 1), F32)
    for g in range(groups):
        head = hp * groups + g
        bias_max = rb_ref[0, head]
        for b in range(1, N_BUCKETS):
            bias_max = jnp.maximum(bias_max, rb_ref[b, head])
        upper = gain_bound + bias_max
        own_k = k_ref[own_rows, lanes[g]].astype(F32)
        own = jnp.sum(qs[g].astype(F32) * jnp.concatenate([own_k, own_k], axis=0), axis=-1, keepdims=True)
        lower = own + rb_ref[0, head]
        m_ref[g] = 0.5 * (upper + lower)
        widest = jnp.maximum(widest, jnp.max(upper - lower, axis=0, keepdims=True))

    @pl.when(widest[0, 0] > MAX_SHIFT_BRACKET)
    def _():
        m_ref[...] = jnp.full_like(m_ref, MASK_VALUE)
        sweep(max_step)

    sweep(acc_step)

    lam = _lambda_value(lam_ref, lam_init)
    for g, sl in enumerate(lanes):
        o_ref[:, sl] = _diff_finish(t, lam, lam_init, sub_ref[...], stats, g).astype(o_ref.dtype)


def _attn_prompt(q, k, v, rel_bias, lam_params, subln2, qn2, kn2, layer, nbatch, seq, nheads, lam_init):
    u = nheads * HEAD
    t = _pick(seq, ATTN_BLOCKS)
    groups = _pick(nheads, HEADS_PER_STEP)
    assert t % CHUNK == 0 and t + 1 >= _bucket_thresholds()[-1]
    nq = seq // t
    w = groups * HEAD
    r = np.arange(t)
    bias = _bias_table(rel_bias, r + t, np.arange(2 * t))
    pbytes = jnp.dtype(q.dtype).itemsize
    est = (2 * (2 * t * w * pbytes + 2 * seq * w * pbytes + groups * 2 * t * t * 4)
           + groups * (2 * t * (HEAD + 2 * 128) * 4 + 6 * 2 * t * t * 4))
    return pl.pallas_call(
        functools.partial(_attn_prompt_kernel, t=t, groups=groups, lam_init=lam_init,
                          far_bucket=N_BUCKETS // 2 - 1),
        grid=(nbatch, nheads // groups, nq),
        in_specs=[
            pl.BlockSpec(memory_space=pltpu.SMEM),
            pl.BlockSpec((t, w), lambda b, h, i: (b * nq + i, h)),
            pl.BlockSpec((seq, w), lambda b, h, i: (b, h)),
            pl.BlockSpec((seq, w), lambda b, h, i: (b, h)),
            pl.BlockSpec((groups, t, 2 * t), lambda b, h, i: (h, 0, 0)),
            pl.BlockSpec((None, 4, HALF), lambda b, h, i: (layer, 0, 0)),
            pl.BlockSpec((None, 1, HEAD), lambda b, h, i: (layer, 0, 0)),
            pl.BlockSpec((None, 1, HEAD), lambda b, h, i: (layer, 0, 0)),
            pl.BlockSpec((None, 1, HEAD), lambda b, h, i: (layer, 0, 0)),
        ],
        out_specs=pl.BlockSpec((t, w), lambda b, h, i: (b * nq + i, h)),
        out_shape=jax.ShapeDtypeStruct((nbatch * seq, u), q.dtype),
        scratch_shapes=_softmax_scratch(groups, t),
        compiler_params=_params(("parallel", "parallel", "arbitrary"), est),
        name="diff_attention_prompt",
    )(rel_bias, q, k, v, bias, lam_params, subln2, qn2, kn2)


def _attn_sample_kernel(q_ref, kn_ref, vn_ref, kc_ref, vc_ref, bc_ref, bn_ref, lam_ref, sub_ref,
                        o_ref, *stats, t, tk, ncache, nheads, lam_init):
    j = pl.program_id(1)
    lanes = [slice(h * HEAD, (h + 1) * HEAD) for h in range(nheads)]

    @pl.when(j == 0)
    def _():
        _softmax_init(stats)

    qs = [_stack_components(q_ref[:, sl]) for sl in lanes]
    head_rows = lambda ref, h: ref[pl.ds(h, tk, stride=nheads), :].astype(qs[0].dtype)
    for h in range(nheads):
        _softmax_step(qs[h], head_rows(kc_ref, h), head_rows(vc_ref, h), bc_ref[h], stats, h)

    @pl.when(j == ncache - 1)
    def _():
        lam = _lambda_value(lam_ref, lam_init)
        for h, sl in enumerate(lanes):
            _softmax_step(qs[h], kn_ref[:, sl], vn_ref[:, sl], bn_ref[h], stats, h)
            o_ref[:, sl] = _diff_finish(t, lam, lam_init, sub_ref[...], stats, h).astype(o_ref.dtype)


def _attn_sample(q, k, v, cache_k, cache_v, rel_bias, lam_params, subln2, layer, nbatch, t, past,
                 row0, nheads, lam_init):
    u = nheads * HEAD
    assert row0 % t == 0
    rb0 = row0 // t
    tk = _pick(past, CACHE_TILES)
    qpos = past + np.arange(t)
    ncache = past // tk
    bias_c = _bias_table(rel_bias, qpos, np.arange(past))
    bias_c = bias_c.reshape(nheads, t, ncache, tk).transpose(2, 0, 1, 3)
    bias_n = _bias_table(rel_bias, qpos, qpos)
    pbytes = jnp.dtype(q.dtype).itemsize
    est = (2 * (4 * t * u * pbytes + 2 * tk * nheads * HEAD * 4 + nheads * t * (tk + t) * 4)
           + nheads * (2 * t * (HEAD + 2 * 128) * 4 + 6 * 2 * t * tk * 4))
    rows = pl.BlockSpec((t, u), lambda b, j: (rb0 + b, 0))
    cache = pl.BlockSpec((None, tk * nheads, HEAD), lambda b, j: (layer, b * ncache + j, 0))
    return pl.pallas_call(
        functools.partial(_attn_sample_kernel, t=t, tk=tk, ncache=ncache, nheads=nheads, lam_init=lam_init),
        grid=(nbatch, ncache),
        in_specs=[
            rows, rows, rows, cache, cache,
            pl.BlockSpec((None, nheads, t, tk), lambda b, j: (j, 0, 0, 0)),
            pl.BlockSpec((nheads, t, t), lambda b, j: (0, 0, 0)),
            pl.BlockSpec((None, 4, HALF), lambda b, j: (layer, 0, 0)),
            pl.BlockSpec((None, 1, HEAD), lambda b, j: (layer, 0, 0)),
        ],
        out_specs=pl.BlockSpec((t, u), lambda b, j: (b, 0)),
        out_shape=jax.ShapeDtypeStruct((nbatch * t, u), q.dtype),
        scratch_shapes=_softmax_scratch(nheads, t),
        compiler_params=_params(("parallel", "arbitrary"), est),
        name="diff_attention_sample",
    )(q, k, v, cache_k, cache_v, bias_c, bias_n, lam_params, subln2)


def _merge_kernel(ap_ref, as_ref, bp_ref, bs_ref, ga0_ref, ga1_ref, gb0_ref, gb1_ref, wa_ref, wb_ref, o_ref,
                  *, u, n_first):
    a = _group_tile((ap_ref, as_ref), n_first)
    b = _group_tile((bp_ref, bs_ref), n_first)
    ya = jnp.dot(a, wa_ref[...], preferred_element_type=F32)
    yb = jnp.dot(b, wb_ref[...], preferred_element_type=F32)
    for c, (ga, gb) in enumerate(((ga0_ref, gb0_ref), (ga1_ref, gb1_ref))):
        sl = slice(c * u, (c + 1) * u)
        o_ref[:, sl] = (ga[...].astype(F32) * ya[:, sl] + gb[...].astype(F32) * yb[:, sl]).astype(o_ref.dtype)


def _merge(ret_outs, dif_outs, gates, w_ret_up, w_dif_up, layer, tm):
    m = gates.shape[0]
    u = ret_outs[0].shape[1]
    d = 2 * u
    n_first = ret_outs[0].shape[0] // tm
    gate = lambda c: pl.BlockSpec((tm, u), lambda i: (i, c))
    rows = _group_specs(tm, u, n_first)
    wspec = pl.BlockSpec((None, u, d), lambda i: (layer, 0, 0))
    pbytes = jnp.dtype(gates.dtype).itemsize
    est = 2 * (8 * tm * u * pbytes + 2 * u * d * pbytes + tm * d * pbytes) + 3 * tm * d * 4
    return pl.pallas_call(
        functools.partial(_merge_kernel, u=u, n_first=n_first),
        grid=(m // tm,),
        in_specs=rows + rows + [gate(0), gate(1), gate(2), gate(3), wspec, wspec],
        out_specs=pl.BlockSpec((tm, d), lambda i: (i, 0)),
        out_shape=jax.ShapeDtypeStruct((m, d), gates.dtype),
        compiler_params=_params(("arbitrary",), est),
        name="gated_merge",
    )(*ret_outs, *dif_outs, gates, gates, gates, gates, w_ret_up, w_dif_up)


def _out_proj_kernel(x_ref, a_ref, w_ref, o_ref):
    o_ref[...] = x_ref[...] + jnp.dot(a_ref[...], w_ref[...], preferred_element_type=F32)


def _out_proj(x, merged, w_out, layer, tm):
    m, d = x.shape
    wbytes = jnp.dtype(w_out.dtype).itemsize
    est = 2 * (2 * tm * d * 4 + tm * d * wbytes + d * d * wbytes) + tm * d * 4
    return pl.pallas_call(
        _out_proj_kernel,
        grid=(m // tm,),
        in_specs=[
            pl.BlockSpec((tm, d), lambda i: (i, 0)),
            pl.BlockSpec((tm, d), lambda i: (i, 0)),
            pl.BlockSpec((None, d, d), lambda i: (layer, 0, 0)),
        ],
        out_specs=pl.BlockSpec((tm, d), lambda i: (i, 0)),
        out_shape=jax.ShapeDtypeStruct((m, d), F32),
        compiler_params=_params(("parallel",), est),
        name="output_projection",
    )(x, merged, w_out)


def _rotary_tables(pos):
    inv = ROPE_BASE ** (-jnp.arange(HALF, dtype=F32) / HALF)
    ang = pos.astype(F32)[:, None] * inv[None, :]
    cos, sin = jnp.cos(ang), jnp.sin(ang)
    return jnp.concatenate([cos, cos], axis=-1), jnp.concatenate([-sin, sin], axis=-1)


def kernel(x_prompt, x_sample, cache_diff_k, cache_diff_v, state_ret, ffn1_norm, ffn1_gate, ffn1_up, ffn1_down, mix_norm, w_in, q_norm, k_norm, lambda_q1, lambda_k1, lambda_q2, lambda_k2, subln, w_ret_up, w_dif_up, w_out, ffn2_norm, ffn2_gate, ffn2_up, ffn2_down, rel_bias):
    nb, seq, d = x_prompt.shape
    db, dseq, _ = x_sample.shape
    depth, _, past, nh_d, _ = cache_diff_k.shape
    nh_r = state_ret.shape[2]
    u = d // 2
    assert nh_r * HEAD == u and nh_d * HEAD == u and dseq == CHUNK and seq % CHUNK == 0
    assert w_in.shape[-1] == N_SEG * u
    mp, ms = nb * seq, db * dseq

    cast = lambda w: w.astype(MXU_DTYPE)
    wg1, wu1, wd1 = cast(ffn1_gate), cast(ffn1_up), cast(ffn1_down)
    wg2, wu2, wd2 = cast(ffn2_gate), cast(ffn2_up), cast(ffn2_down)
    w_ret_c, w_dif_c, w_out_c = cast(w_ret_up), cast(w_dif_up), cast(w_out)
    row3 = lambda g: g.reshape(depth, 1, g.shape[-1])
    n1, nmix, n2 = row3(ffn1_norm), row3(mix_norm), row3(ffn2_norm)
    qn2 = row3(jnp.concatenate([q_norm, q_norm], axis=-1))
    kn2 = row3(jnp.concatenate([k_norm, k_norm], axis=-1))
    subln2 = row3(subln)
    lam_params = jnp.stack([lambda_q1, lambda_k1, lambda_q2, lambda_k2], axis=1)
    cache_k = cache_diff_k.reshape(depth, db * past * nh_d, HEAD)
    cache_v = cache_diff_v.reshape(depth, db * past * nh_d, HEAD)

    pos = jnp.concatenate([jnp.tile(jnp.arange(seq, dtype=jnp.int32), nb),
                           past + jnp.tile(jnp.arange(dseq, dtype=jnp.int32), db)])
    rot = _rotary_tables(pos)

    tm = _pick(math.gcd(mp, ms), ROW_TILES)
    tm_proj = _pick(mp + ms, PROJ_ROW_TILES)
    groups = (mp, ms)
    zero_state = jnp.zeros((nb, nh_r, HEAD, HEAD), F32)
    t_ret = _pick(seq, RET_BLOCKS)

    x = (x_prompt.reshape(mp, d), x_sample.reshape(ms, d))
    kps, kss, vps, vss, states_p, states_s = [], [], [], [], [], []
    for l in range(depth):
        lam_init = 0.8 - 0.6 * math.exp(-0.3 * l)
        x, h = _ffn(x, n1, wg1, wu1, wd1, l, tm, next_gain=nmix)

        last = l == depth - 1
        seg = functools.partial(_segment, h, w_in, l)
        rq = seg(SEG_RQ, 1, tm_proj, "rotary", tables=rot)
        rk = seg(SEG_RK, 1, tm_proj, "rotary", tables=rot, scale=HEAD ** -0.5)
        rv = seg(SEG_RV, 1, tm_proj, "cast")
        rg = seg(SEG_RG, 1, tm_proj, "silu")
        dq = seg(SEG_DQ, 1, tm_proj, "norm", gain=qn2, scale=HALF ** -0.5)
        dk, k_p, k_s = seg(SEG_DK, 1, tm, "norm_keep", gain=kn2, group_rows=groups,
                           prev_prompt=kps if last else ())
        dv, v_p, v_s = seg(SEG_DV, 1, tm, "keep", group_rows=groups, prev_prompt=vps if last else ())
        gates = seg(SEG_GATES, N_SEG - SEG_GATES, tm_proj, "sigmoid")

        ret_p, st_p = _retention(rq, rk, rv, rg, zero_state, t_ret, nb, seq // t_ret, 0)
        ret_s, st_s = _retention(rq, rk, rv, rg, state_ret[l].astype(F32), dseq, db, 1, mp // dseq)

        dif_p = _attn_prompt(dq, dk, dv, rel_bias, lam_params, subln2, qn2, kn2, l, nb, seq, nh_d, lam_init)
        dif_s = _attn_sample(dq, dk, dv, cache_k, cache_v, rel_bias, lam_params, subln2, l, db, dseq, past,
                             mp, nh_d, lam_init)

        merged = _merge((ret_p, ret_s), (dif_p, dif_s), gates, w_ret_c, w_dif_c, l, tm)
        x = _out_proj(x, merged, w_out_c, l, tm)
        x = _ffn(x, n2, wg2, wu2, wd2, l, tm, split_out=groups if l == depth - 1 else None)

        for acc, val in ((kps, k_p), (kss, k_s), (vps, v_p), (vss, v_s), (states_p, st_p), (states_s, st_s)):
            acc.append(val)

    y_p, y_s = x
    kv_p = lambda parts: parts[-1].reshape(depth, nb, seq, nh_d, HEAD)
    kv_s = lambda parts: jnp.stack(parts).reshape(depth, db, dseq, nh_d, HEAD)
    return (y_p.reshape(nb, seq, d), y_s.reshape(db, dseq, d),
            kv_p(kps).astype(cache_diff_k.dtype), kv_p(vps).astype(cache_diff_v.dtype),
            jnp.stack(states_p).astype(state_ret.dtype),
            kv_s(kss).astype(cache_diff_k.dtype), kv_s(vss).astype(cache_diff_v.dtype),
            jnp.stack(states_s).astype(state_ret.dtype))
```

```python
import functools
import math

import numpy as np
import jax
import jax.numpy as jnp
from jax import lax
from jax.experimental import pallas as pl
from jax.experimental.pallas import tpu as pltpu

F32 = jnp.float32
MXU_DTYPE = jnp.bfloat16

CHUNK = 64
HEAD = 128
HALF = HEAD // 2
ROPE_BASE = 10000.0
N_BUCKETS = 32
MAX_DISTANCE = 128
EPS = 1e-6
MASK_VALUE = -1e30
SCORE_BOUND_MARGIN = 1.03
MAX_SHIFT_BRACKET = 100.0

VMEM_LIMIT_CAP = 60 * 1024 * 1024
MIB = 1024 * 1024

ROW_TILES = (512, 256, 128, 64)
PROJ_ROW_TILES = (1536, 1024, 768, 512, 256, 128, 64)
FF_TILES = (512, 256, 128)
FF_CHUNKS_PER_STEP = 2
ATTN_BLOCKS = (512, 256, 128)
HEADS_PER_STEP = (4, 2, 1)
RET_BLOCKS = (256, 128, 64)
CACHE_TILES = (2048, 1024, 512, 256, 128, 64)


def _pick(n, prefs):
    for p in prefs:
        if n % p == 0:
            return p
    raise ValueError(f"no tile in {prefs} divides {n}")


def _params(semantics, est_bytes):
    limit = int(min(max(est_bytes + 8 * MIB, 32 * MIB), VMEM_LIMIT_CAP))
    return pltpu.CompilerParams(dimension_semantics=semantics, vmem_limit_bytes=limit)


def _rms_scale(x):
    return lax.rsqrt(jnp.mean(x * x, axis=-1, keepdims=True) + EPS)


def _group_specs(tm, width, n_first, row_axis=0):
    first = pl.BlockSpec((tm, width), lambda *g: (jnp.minimum(g[row_axis], n_first - 1), 0))
    second = pl.BlockSpec((tm, width), lambda *g: (jnp.maximum(g[row_axis] - n_first, 0), 0),
                          pipeline_mode=pl.Buffered(1))
    return [first, second]


def _group_tile(refs, n_first, row_axis=0):
    if len(refs) == 1:
        return refs[0][...]
    return jnp.where(pl.program_id(row_axis) < n_first, refs[0][...], refs[1][...])


def _store_group_tile(refs, n_first, value, row_axis=0):
    if len(refs) == 1:
        refs[0][...] = value
        return
    i = pl.program_id(row_axis)

    @pl.when(i < n_first)
    def _():
        refs[0][...] = value

    @pl.when(i >= n_first)
    def _():
        refs[1][...] = value


def _ffn_kernel(*refs, nf, per_step, n_in, n_out, n_first, norm_out):
    x_refs = refs[:n_in]
    g_ref = refs[n_in]
    pos = n_in + 1
    w_refs = [refs[pos + 3 * c:pos + 3 * c + 3] for c in range(per_step)]
    pos += 3 * per_step
    g2_ref = refs[pos] if norm_out else None
    pos += int(norm_out)
    o_refs = refs[pos:pos + n_out]
    pos += n_out
    hn_ref = refs[pos] if norm_out else None
    pos += int(norm_out)
    h_ref, acc_ref = refs[pos:]
    f = pl.program_id(1)
    nsteps = -(-nf // per_step)

    @pl.when(f == 0)
    def _():
        x = _group_tile(x_refs, n_first)
        h_ref[...] = (x * _rms_scale(x) * g_ref[...]).astype(h_ref.dtype)
        acc_ref[...] = jnp.zeros_like(acc_ref)

    def chunks(slots):
        h = h_ref[...]
        pre = [(jnp.dot(h, wg[...], preferred_element_type=F32), jnp.dot(h, wu[...], preferred_element_type=F32))
               for wg, wu, _ in slots]
        total = acc_ref[...]
        for (gate, up), (_, _, wd) in zip(pre, slots):
            act = (gate * jax.nn.sigmoid(gate) * up).astype(wd.dtype)
            total = total + jnp.dot(act, wd[...], preferred_element_type=F32)
        acc_ref[...] = total

    head = nf - (nsteps - 1) * per_step
    if head == per_step:
        chunks(w_refs)
    else:
        @pl.when(f == 0)
        def _():
            chunks(w_refs[:head])

        @pl.when(f > 0)
        def _():
            chunks(w_refs)

    @pl.when(f == nsteps - 1)
    def _():
        y = _group_tile(x_refs, n_first) + 0.5 * acc_ref[...]
        _store_group_tile(o_refs, n_first, y)
        if norm_out:
            hn_ref[...] = (y * _rms_scale(y) * g2_ref[...]).astype(hn_ref.dtype)


def _ffn(xs, gain, wg, wu, wd, layer, tm, split_out=None, next_gain=None):
    xs = tuple(xs) if isinstance(xs, (tuple, list)) else (xs,)
    m = sum(x.shape[0] for x in xs)
    d = xs[0].shape[1]
    n_first = (xs[0].shape[0] if len(xs) == 2 else split_out[0] if split_out else m) // tm
    ff = wg.shape[-1]
    tf = _pick(ff, FF_TILES)
    nf = ff // tf
    norm_out = next_gain is not None
    wbytes = jnp.dtype(wg.dtype).itemsize
    row_buffers = (3 if len(xs) == 2 else 2) + (3 if split_out else 2)
    vmem_need = lambda chunks: (
        row_buffers * tm * d * 4 + 2 * (3 * chunks * d * tf * wbytes + int(norm_out) * tm * d * wbytes)
        + tm * d * (4 + wbytes) + 2 * chunks * tm * tf * 4)
    per_step = min(FF_CHUNKS_PER_STEP, nf)
    while per_step > 1 and vmem_need(per_step) > VMEM_LIMIT_CAP:
        per_step -= 1
    nsteps = -(-nf // per_step)
    est = vmem_need(per_step)
    rows = lambda: pl.BlockSpec((tm, d), lambda i, f: (i, 0))
    gain_spec = pl.BlockSpec((None, 1, d), lambda i, f: (layer, 0, 0))
    in_specs = (_group_specs(tm, d, n_first) if len(xs) == 2 else [rows()]) + [gain_spec]
    args = [*xs, gain]
    head = nf - (nsteps - 1) * per_step
    for c in range(per_step):
        chunk = lambda f, c=c: jnp.where(f == 0, min(c, head - 1), head + (f - 1) * per_step + c)
        in_specs += [pl.BlockSpec((None, d, tf), lambda i, f, chunk=chunk: (layer, 0, chunk(f))),
                     pl.BlockSpec((None, d, tf), lambda i, f, chunk=chunk: (layer, 0, chunk(f))),
                     pl.BlockSpec((None, tf, d), lambda i, f, chunk=chunk: (layer, chunk(f), 0))]
        args += [wg, wu, wd]
    if split_out:
        out_specs = _group_specs(tm, d, n_first)
        out_shape = [jax.ShapeDtypeStruct((r, d), F32) for r in split_out]
    else:
        out_specs = [rows()]
        out_shape = [jax.ShapeDtypeStruct((m, d), F32)]
    if norm_out:
        in_specs.append(gain_spec)
        args.append(next_gain)
        out_specs.append(rows())
        out_shape.append(jax.ShapeDtypeStruct((m, d), wg.dtype))
    outs = pl.pallas_call(
        functools.partial(_ffn_kernel, nf=nf, per_step=per_step, n_in=len(xs), n_out=2 if split_out else 1,
                          n_first=n_first, norm_out=norm_out),
        grid=(m // tm, nsteps),
        in_specs=in_specs,
        out_specs=out_specs,
        out_shape=out_shape,
        scratch_shapes=[pltpu.VMEM((tm, d), wg.dtype), pltpu.VMEM((tm, d), F32)],
        compiler_params=_params(("arbitrary", "arbitrary"), est),
        name="swiglu_half_step",
    )(*args)
    return outs if len(outs) > 1 else outs[0]


SEG_RQ, SEG_RK, SEG_RV, SEG_RG, SEG_DQ, SEG_DK, SEG_DV, SEG_GATES, N_SEG = 0, 1, 2, 3, 4, 5, 6, 7, 11


def _rotate_half_pairs(a, cos2, sin2):
    return a * cos2 + pltpu.roll(a, HALF, 1) * sin2


def _component_rms_norm(a, gain2):
    lo = lax.broadcasted_iota(jnp.int32, a.shape, 1) < HALF
    sq = a * a
    s_all = jnp.sum(sq, axis=-1, keepdims=True)
    s_lo = jnp.sum(jnp.where(lo, sq, 0.0), axis=-1, keepdims=True)
    ms = jnp.where(lo, s_lo, s_all - s_lo) * (1.0 / HALF)
    return a * lax.rsqrt(ms + EPS) * gain2


def _keep_f32(fp_ref, fs_ref, prev_refs, n_first, value):
    i = pl.program_id(1)

    @pl.when(i < n_first)
    def _():
        if prev_refs:
            for l, prev in enumerate(prev_refs):
                fp_ref[l] = prev[...]
            fp_ref[len(prev_refs)] = value
        else:
            fp_ref[...] = value

    @pl.when(i >= n_first)
    def _():
        fs_ref[...] = value


def _segment_kernel(h_ref, w_ref, *refs, kind, scale, nheads, n_first, n_prev):
    *refs, wc_ref = refs

    @pl.when(pl.program_id(1) == 0)
    def _():
        wc_ref[...] = w_ref[...].astype(wc_ref.dtype)

    acc = jnp.dot(h_ref[...], wc_ref[...], preferred_element_type=F32)
    heads = [slice(h * HEAD, (h + 1) * HEAD) for h in range(nheads)]
    if kind == "rotary":
        cos_ref, sin_ref, o_ref = refs
        for sl in heads:
            r = _rotate_half_pairs(acc[:, sl], cos_ref[...], sin_ref[...])
            o_ref[:, sl] = (r if scale == 1.0 else r * scale).astype(o_ref.dtype)
    elif kind == "cast":
        (o_ref,) = refs
        o_ref[...] = acc.astype(o_ref.dtype)
    elif kind == "silu":
        (o_ref,) = refs
        o_ref[...] = (acc * jax.nn.sigmoid(acc)).astype(o_ref.dtype)
    elif kind == "sigmoid":
        (o_ref,) = refs
        o_ref[...] = jax.nn.sigmoid(acc).astype(o_ref.dtype)
    elif kind == "norm":
        gain_ref, o_ref = refs
        for sl in heads:
            o_ref[:, sl] = (_component_rms_norm(acc[:, sl], gain_ref[...]) * scale).astype(o_ref.dtype)
    elif kind == "norm_keep":
        gain_ref, *prev_refs, o_ref, fp_ref, fs_ref = refs
        normed = jnp.concatenate([_component_rms_norm(acc[:, sl], gain_ref[...]) for sl in heads], axis=1)
        _keep_f32(fp_ref, fs_ref, prev_refs, n_first, normed)
        o_ref[...] = normed.astype(o_ref.dtype)
    elif kind == "keep":
        *prev_refs, o_ref, fp_ref, fs_ref = refs
        _keep_f32(fp_ref, fs_ref, prev_refs, n_first, acc)
        o_ref[...] = acc.astype(o_ref.dtype)
    else:
        raise ValueError(kind)
    assert kind not in ("norm_keep", "keep") or len(prev_refs) == n_prev


def _segment(h, w_in, layer, seg0, nseg, tm, kind, *, scale=1.0, tables=(), gain=None, group_rows=None,
             prev_prompt=()):
    m, d = h.shape
    u = d // 2
    nheads = u // HEAD
    n_first = group_rows[0] // tm if group_rows else 0
    n_prev = len(prev_prompt)
    in_specs = [pl.BlockSpec((tm, d), lambda s, i: (i, 0)),
                pl.BlockSpec((None, d, u), lambda s, i: (layer, 0, seg0 + s))]
    args = [h, w_in]
    for tab in tables:
        in_specs.append(pl.BlockSpec((tm, HEAD), lambda s, i: (i, 0)))
        args.append(tab)
    if gain is not None:
        in_specs.append(pl.BlockSpec((None, 1, HEAD), lambda s, i: (layer, 0, 0)))
        args.append(gain)
    out_specs = [pl.BlockSpec((tm, u), lambda s, i: (i, s))]
    out_shape = [jax.ShapeDtypeStruct((m, nseg * u), h.dtype)]
    if group_rows:
        first, second = _group_specs(tm, u, n_first, row_axis=1)
        in_specs += [first] * n_prev
        args += list(prev_prompt)
        if n_prev:
            first = pl.BlockSpec((n_prev + 1, tm, u), lambda s, i: (0, jnp.minimum(i, n_first - 1), 0))
        out_specs += [first, second]
        out_shape += [jax.ShapeDtypeStruct(((n_prev + 1, group_rows[0], u) if n_prev else (group_rows[0], u)), F32),
                      jax.ShapeDtypeStruct((group_rows[1], u), F32)]
    hb = jnp.dtype(h.dtype).itemsize
    est = (2 * (tm * d * hb + d * u * 4 + tm * u * hb + (2 * n_prev + 2) * tm * u * 4 + 2 * tm * HEAD * 4)
           + d * u * hb + 4 * tm * u * 4)
    outs = pl.pallas_call(
        functools.partial(_segment_kernel, kind=kind, scale=scale, nheads=nheads, n_first=n_first, n_prev=n_prev),
        grid=(nseg, m // tm),
        in_specs=in_specs,
        out_specs=out_specs,
        out_shape=out_shape,
        scratch_shapes=[pltpu.VMEM((d, u), h.dtype)],
        compiler_params=_params(("arbitrary", "arbitrary"), est),
        name="input_projection_" + kind,
    )(*args)
    return outs if len(outs) > 1 else outs[0]


def _retention_kernel(q_ref, k_ref, v_ref, g_ref, s0_ref, d_ref, wq_ref, we_ref, dec_ref, o_ref, sout_ref, st_ref,
                      *, nheads, nblk):
    t = pl.program_id(1)

    @pl.when(t == 0)
    def _():
        st_ref[...] = s0_ref[...]

    lanes = [slice(h * HEAD, (h + 1) * HEAD) for h in range(nheads)]
    scores, inter = [], []
    for h, sl in enumerate(lanes):
        q, k, v = q_ref[:, sl], k_ref[:, sl], v_ref[:, sl]
        state = st_ref[h]
        scores.append(lax.dot_general(q, k, (((1,), (1,)), ((), ())), preferred_element_type=F32))
        inter.append(jnp.dot(q, state.astype(q.dtype), preferred_element_type=F32))
        kw = (k.astype(F32) * we_ref[:, sl]).astype(k.dtype)
        kv = lax.dot_general(kw, v, (((0,), (0,)), ((), ())), preferred_element_type=F32)
        st_ref[h] = state * dec_ref[h:h + 1, :] + kv
    for h, sl in enumerate(lanes):
        v = v_ref[:, sl]
        s = scores[h] * d_ref[h]
        o = jnp.dot(s.astype(v.dtype), v, preferred_element_type=F32) + wq_ref[:, sl] * inter[h]
        r = o * _rms_scale(o)
        o_ref[:, sl] = (r * g_ref[:, sl].astype(F32)).astype(o_ref.dtype)

    @pl.when(t == nblk - 1)
    def _():
        sout_ref[...] = st_ref[...]


def _retention_tables(t, nheads):
    log_g = jnp.log(1.0 - 2.0 ** (-5.0 - jnp.arange(nheads, dtype=F32)))
    idx = jnp.arange(t, dtype=F32)
    dist = jnp.abs(idx[:, None] - idx[None, :])
    ci = np.arange(t) // CHUNK
    visible = jnp.asarray(ci[None, :] <= ci[:, None])
    dmat = jnp.where(visible[None], jnp.exp(log_g[:, None, None] * dist[None]), 0.0)
    wq = jnp.exp(log_g[None, :] * (idx + 1.0)[:, None])
    we = jnp.exp(log_g[None, :] * (t - 1.0 - idx)[:, None])
    dec = jnp.exp(log_g * t)
    expand = lambda a: jnp.repeat(a, HEAD, axis=1)
    return dmat, expand(wq), expand(we), jnp.broadcast_to(dec[:, None], (nheads, HEAD))


def _retention(q, k, v, g, s0, t, nbatch, nblk, row_block0):
    nheads = s0.shape[1]
    u = nheads * HEAD
    dmat, wq, we, dec = _retention_tables(t, nheads)
    rows = pl.BlockSpec((t, u), lambda b, i: (row_block0 + b * nblk + i, 0))
    whole = lambda a: pl.BlockSpec(a.shape, lambda b, i: (0,) * a.ndim)
    state_spec = pl.BlockSpec((None, nheads, HEAD, HEAD), lambda b, i: (b, 0, 0, 0))
    pbytes = jnp.dtype(q.dtype).itemsize
    est = (2 * (5 * t * u * pbytes + 2 * nheads * HEAD * HEAD * 4 + nheads * t * t * 4 + 2 * t * u * 4)
           + nheads * HEAD * HEAD * 4 + 6 * t * max(t, HEAD) * 4)
    return pl.pallas_call(
        functools.partial(_retention_kernel, nheads=nheads, nblk=nblk),
        grid=(nbatch, nblk),
        in_specs=[rows, rows, rows, rows, state_spec, whole(dmat), whole(wq), whole(we), whole(dec)],
        out_specs=[
            pl.BlockSpec((t, u), lambda b, i: (b * nblk + i, 0)),
            state_spec,
        ],
        out_shape=[
            jax.ShapeDtypeStruct((nbatch * nblk * t, u), q.dtype),
            jax.ShapeDtypeStruct((nbatch, nheads, HEAD, HEAD), F32),
        ],
        scratch_shapes=[pltpu.VMEM((nheads, HEAD, HEAD), F32)],
        compiler_params=_params(("parallel", "arbitrary"), est),
        name="retention",
    )(q, k, v, g, s0, dmat, wq, we, dec)


def _bucket_thresholds():
    nb = N_BUCKETS // 2
    me = nb // 2
    out = []
    for k in range(1, nb - me):
        n = me
        while n ** (nb - me) * me ** k < me ** (nb - me) * MAX_DISTANCE ** k:
            n += 1
        out.append(n)
    return out


def _t5_bucket_np(rel):
    nb = N_BUCKETS // 2
    me = nb // 2
    n = np.abs(rel)
    large = np.full(rel.shape, me, np.int64)
    for thr in _bucket_thresholds():
        large += (n >= thr)
    large = np.minimum(large, nb - 1)
    return (np.where(rel > 0, nb, 0) + np.where(n < me, n, large)).astype(np.int32)


def _bias_kernel(rb_ref, idx_ref, mask_ref, o_ref):
    h = pl.program_id(0)
    idx = idx_ref[...]
    out = mask_ref[...]
    for b in range(N_BUCKETS):
        out = out + jnp.where(idx == b, rb_ref[b, h], 0.0)
    o_ref[...] = out


def _bias_table(rel_bias, qpos, kpos):
    nheads = rel_bias.shape[1]
    rel = kpos[None, :] - qpos[:, None]
    idx = jnp.asarray(_t5_bucket_np(rel))
    mask = jnp.asarray(np.where((kpos[None, :] // CHUNK) <= (qpos[:, None] // CHUNK), 0.0, MASK_VALUE)
                       .astype(np.float32))
    nq, nk = rel.shape
    return pl.pallas_call(
        _bias_kernel,
        grid=(nheads,),
        in_specs=[
            pl.BlockSpec(memory_space=pltpu.SMEM),
            pl.BlockSpec((nq, nk), lambda h: (0, 0)),
            pl.BlockSpec((nq, nk), lambda h: (0, 0)),
        ],
        out_specs=pl.BlockSpec((None, nq, nk), lambda h: (h, 0, 0)),
        out_shape=jax.ShapeDtypeStruct((nheads, nq, nk), F32),
        compiler_params=_params(("arbitrary",), 6 * nq * nk * 4),
        name="relative_bias_table",
    )(rel_bias, idx, mask)


def _stack_components(q):
    lo = lax.broadcasted_iota(jnp.int32, q.shape, 1) < HALF
    zero = jnp.zeros_like(q)
    return jnp.concatenate([jnp.where(lo, q, zero), jnp.where(lo, zero, q)], axis=0)


def _biased_scores(qs, k, bias):
    s = lax.dot_general(qs, k, (((1,), (1,)), ((), ())), preferred_element_type=F32)
    if bias.ndim == 2:
        t = bias.shape[0]
        return jnp.concatenate([s[:t] + bias, s[t:] + bias], axis=0), 0.0
    return s, bias


def _softmax_step(qs, k, v, bias, stats, g):
    m_ref, l_ref, acc_ref = stats
    s, c = _biased_scores(qs, k, bias)
    m_prev = m_ref[g]
    m_new = jnp.maximum(m_prev, jnp.max(s, axis=-1, keepdims=True) + c)
    alpha = jnp.exp(m_prev - m_new)
    p = jnp.exp(s - (m_new - c))
    l_ref[g] = alpha * l_ref[g] + jnp.sum(p, axis=-1, keepdims=True)
    acc_ref[g] = alpha * acc_ref[g] + jnp.dot(p.astype(v.dtype), v, preferred_element_type=F32)
    m_ref[g] = m_new


def _softmax_init(stats):
    m_ref, l_ref, acc_ref = stats
    m_ref[...] = jnp.full_like(m_ref, MASK_VALUE)
    l_ref[...] = jnp.zeros_like(l_ref)
    acc_ref[...] = jnp.zeros_like(acc_ref)


def _softmax_scratch(groups, t):
    return [pltpu.VMEM((groups, 2 * t, 1), F32), pltpu.VMEM((groups, 2 * t, 1), F32),
            pltpu.VMEM((groups, 2 * t, HEAD), F32)]


def _lambda_value(lam_ref, lam_init):
    a = lam_ref[...]
    e1 = jnp.exp(jnp.sum(a[0:1] * a[1:2], axis=-1, keepdims=True))
    e2 = jnp.exp(jnp.sum(a[2:3] * a[3:4], axis=-1, keepdims=True))
    return e1 - e2 + lam_init


def _diff_finish(t, lam, lam_init, subln, stats, g):
    _, l_ref, acc_ref = stats
    acc = acc_ref[g]
    l = l_ref[g]
    o = acc[:t] / l[:t] - lam * (acc[t:] / l[t:])
    return o * _rms_scale(o) * subln * (1.0 - lam_init)


def _shift_brackets(rb_ref, qg_ref, kg_ref, qs, own_ks, head0, m_ref):
    gain_bound = (SCORE_BOUND_MARGIN * HALF ** 0.5 * jnp.max(jnp.abs(qg_ref[...]), axis=-1, keepdims=True)
                  * jnp.max(jnp.abs(kg_ref[...]), axis=-1, keepdims=True))
    widest = jnp.zeros((1, 1), F32)
    for g, (q, own_k) in enumerate(zip(qs, own_ks)):
        head = head0 + g
        bias_max = rb_ref[0, head]
        for b in range(1, N_BUCKETS):
            bias_max = jnp.maximum(bias_max, rb_ref[b, head])
        upper = gain_bound + bias_max
        own_k = own_k.astype(F32)
        own = jnp.sum(q.astype(F32) * jnp.concatenate([own_k, own_k], axis=0), axis=-1, keepdims=True)
        lower = own + rb_ref[0, head]
        m_ref[g] = 0.5 * (upper + lower)
        widest = jnp.maximum(widest, jnp.max(upper - lower, axis=0, keepdims=True))
    return widest[0, 0]


def _attn_prompt_kernel(rb_ref, q_ref, k_ref, v_ref, bias_ref, lam_ref, sub_ref, qg_ref, kg_ref, o_ref, *stats,
                        t, groups, lam_init, far_bucket):
    hp = pl.program_id(1)
    qi = pl.program_id(2)
    m_ref, l_ref, acc_ref = stats
    lanes = [slice(g * HEAD, (g + 1) * HEAD) for g in range(groups)]
    qs = [_stack_components(q_ref[:, sl]) for sl in lanes]
    far_bias = [rb_ref[far_bucket, hp * groups + g] for g in range(groups)]
    _softmax_init(stats)

    def sweep(step):
        def run(j, bias_of):
            step(pl.ds(pl.multiple_of(j * t, t), t), [bias_of(g) for g in range(groups)])

        def far_step(j, carry):
            run(j, lambda g: far_bias[g])
            return carry

        lax.fori_loop(0, jnp.maximum(qi - 1, 0), far_step, 0)

        @pl.when(qi > 0)
        def _():
            run(qi - 1, lambda g: bias_ref[g, :, :t])

        run(qi, lambda g: bias_ref[g, :, t:])

    def max_step(rows, biases):
        scores = [_biased_scores(qs[g], k_ref[rows, lanes[g]], biases[g]) for g in range(groups)]
        for g, (s, c) in enumerate(scores):
            m_ref[g] = jnp.maximum(m_ref[g], jnp.max(s, axis=-1, keepdims=True) + c)

    def acc_step(rows, biases):
        scores = [_biased_scores(qs[g], k_ref[rows, lanes[g]], biases[g]) for g in range(groups)]
        probs = [jnp.exp(s - (m_ref[g] - c)) for g, (s, c) in enumerate(scores)]
        for g, p in enumerate(probs):
            l_ref[g] += jnp.sum(p, axis=-1, keepdims=True)
            v = v_ref[rows, lanes[g]]
            acc_ref[g] += jnp.dot(p.astype(v.dtype), v, preferred_element_type=F32)

    own_rows = pl.ds(pl.multiple_of(qi * t, t), t)
    widest = _shift_brackets(rb_ref, qg_ref, kg_ref, qs, [k_ref[own_rows, sl] for sl in lanes],
                             hp * groups, m_ref)

    @pl.when(widest > MAX_SHIFT_BRACKET)
    def _():
        m_ref[...] = jnp.full_like(m_ref, MASK_VALUE)
        sweep(max_step)

    sweep(acc_step)

    lam = _lambda_value(lam_ref, lam_init)
    for g, sl in enumerate(lanes):
        o_ref[:, sl] = _diff_finish(t, lam, lam_init, sub_ref[...], stats, g).astype(o_ref.dtype)


def _attn_prompt(q, k, v, rel_bias, lam_params, subln2, qn2, kn2, layer, nbatch, seq, nheads, lam_init):
    u = nheads * HEAD
    t = _pick(seq, ATTN_BLOCKS)
    groups = _pick(nheads, HEADS_PER_STEP)
    assert t % CHUNK == 0 and t + 1 >= _bucket_thresholds()[-1]
    nq = seq // t
    w = groups * HEAD
    r = np.arange(t)
    bias = _bias_table(rel_bias, r + t, np.arange(2 * t))
    pbytes = jnp.dtype(q.dtype).itemsize
    est = (2 * (2 * t * w * pbytes + 2 * seq * w * pbytes + groups * 2 * t * t * 4)
           + groups * (2 * t * (HEAD + 2 * 128) * 4 + 6 * 2 * t * t * 4))
    return pl.pallas_call(
        functools.partial(_attn_prompt_kernel, t=t, groups=groups, lam_init=lam_init,
                          far_bucket=N_BUCKETS // 2 - 1),
        grid=(nbatch, nheads // groups, nq),
        in_specs=[
            pl.BlockSpec(memory_space=pltpu.SMEM),
            pl.BlockSpec((t, w), lambda b, h, i: (b * nq + i, h)),
            pl.BlockSpec((seq, w), lambda b, h, i: (b, h)),
            pl.BlockSpec((seq, w), lambda b, h, i: (b, h)),
            pl.BlockSpec((groups, t, 2 * t), lambda b, h, i: (h, 0, 0)),
            pl.BlockSpec((None, 4, HALF), lambda b, h, i: (layer, 0, 0)),
            pl.BlockSpec((None, 1, HEAD), lambda b, h, i: (layer, 0, 0)),
            pl.BlockSpec((None, 1, HEAD), lambda b, h, i: (layer, 0, 0)),
            pl.BlockSpec((None, 1, HEAD), lambda b, h, i: (layer, 0, 0)),
        ],
        out_specs=pl.BlockSpec((t, w), lambda b, h, i: (b * nq + i, h)),
        out_shape=jax.ShapeDtypeStruct((nbatch * seq, u), q.dtype),
        scratch_shapes=_softmax_scratch(groups, t),
        compiler_params=_params(("parallel", "parallel", "arbitrary"), est),
        name="diff_attention_prompt",
    )(rel_bias, q, k, v, bias, lam_params, subln2, qn2, kn2)


def _attn_sample_kernel(rb_ref, q_ref, kn_ref, vn_ref, kc_ref, vc_ref, bc_ref, bn_ref, lam_ref, sub_ref,
                        qg_ref, kg_ref, o_ref, *stats, t, tk, ncache, nheads, lam_init):
    *stats, wide_ref = stats
    m_ref, l_ref, acc_ref = stats
    j = pl.program_id(1)
    lanes = [slice(h * HEAD, (h + 1) * HEAD) for h in range(nheads)]
    heads = range(nheads)
    qs = [_stack_components(q_ref[:, sl]) for sl in lanes]

    @pl.when(j == 0)
    def _():
        _softmax_init(stats)
        widest = _shift_brackets(rb_ref, qg_ref, kg_ref, qs, [kn_ref[:, sl] for sl in lanes], 0, m_ref)
        wide_ref[0] = (widest > MAX_SHIFT_BRACKET).astype(jnp.int32)

        @pl.when(widest > MAX_SHIFT_BRACKET)
        def _():
            m_ref[...] = jnp.full_like(m_ref, MASK_VALUE)

    def fixed_shift_steps(ks, vs, biases):
        scores = [_biased_scores(qs[h], ks[h], biases[h])[0] for h in heads]
        probs = [jnp.exp(scores[h] - m_ref[h]) for h in heads]
        for h in heads:
            l_ref[h] += jnp.sum(probs[h], axis=-1, keepdims=True)
            acc_ref[h] += jnp.dot(probs[h].astype(vs[h].dtype), vs[h], preferred_element_type=F32)

    def online_steps(ks, vs, biases):
        for h in heads:
            _softmax_step(qs[h], ks[h], vs[h], biases[h], stats, h)

    def both(ks, vs, biases):
        @pl.when(wide_ref[0] == 0)
        def _():
            fixed_shift_steps(ks(), vs(), biases())

        @pl.when(wide_ref[0] != 0)
        def _():
            online_steps(ks(), vs(), biases())

    head_rows = lambda ref: [ref[pl.ds(h, tk, stride=nheads), :].astype(qs[0].dtype) for h in heads]
    both(lambda: head_rows(kc_ref), lambda: head_rows(vc_ref), lambda: [bc_ref[h] for h in heads])

    @pl.when(j == ncache - 1)
    def _():
        both(lambda: [kn_ref[:, sl] for sl in lanes], lambda: [vn_ref[:, sl] for sl in lanes],
             lambda: [bn_ref[h] for h in heads])
        lam = _lambda_value(lam_ref, lam_init)
        for h, sl in enumerate(lanes):
            o_ref[:, sl] = _diff_finish(t, lam, lam_init, sub_ref[...], stats, h).astype(o_ref.dtype)


def _attn_sample(q, k, v, cache_k, cache_v, rel_bias, lam_params, subln2, qn2, kn2, layer, nbatch, t, past,
                 row0, nheads, lam_init):
    u = nheads * HEAD
    assert row0 % t == 0
    rb0 = row0 // t
    tk = _pick(past, CACHE_TILES)
    qpos = past + np.arange(t)
    ncache = past // tk
    bias_c = _bias_table(rel_bias, qpos, np.arange(past))
    bias_c = bias_c.reshape(nheads, t, ncache, tk).transpose(2, 0, 1, 3)
    bias_n = _bias_table(rel_bias, qpos, qpos)
    pbytes = jnp.dtype(q.dtype).itemsize
    est = (2 * (4 * t * u * pbytes + 2 * tk * nheads * HEAD * 4 + nheads * t * (tk + t) * 4)
           + nheads * (2 * t * (HEAD + 2 * 128) * 4 + 6 * 2 * t * tk * 4))
    rows = pl.BlockSpec((t, u), lambda b, j: (rb0 + b, 0))
    cache = pl.BlockSpec((None, tk * nheads, HEAD), lambda b, j: (layer, b * ncache + j, 0))
    return pl.pallas_call(
        functools.partial(_attn_sample_kernel, t=t, tk=tk, ncache=ncache, nheads=nheads, lam_init=lam_init),
        grid=(nbatch, ncache),
        in_specs=[
            pl.BlockSpec(memory_space=pltpu.SMEM),
            rows, rows, rows, cache, cache,
            pl.BlockSpec((None, nheads, t, tk), lambda b, j: (j, 0, 0, 0)),
            pl.BlockSpec((nheads, t, t), lambda b, j: (0, 0, 0)),
            pl.BlockSpec((None, 4, HALF), lambda b, j: (layer, 0, 0)),
            pl.BlockSpec((None, 1, HEAD), lambda b, j: (layer, 0, 0)),
            pl.BlockSpec((None, 1, HEAD), lambda b, j: (layer, 0, 0)),
            pl.BlockSpec((None, 1, HEAD), lambda b, j: (layer, 0, 0)),
        ],
        out_specs=pl.BlockSpec((t, u), lambda b, j: (b, 0)),
        out_shape=jax.ShapeDtypeStruct((nbatch * t, u), q.dtype),
        scratch_shapes=_softmax_scratch(nheads, t) + [pltpu.SMEM((1,), jnp.int32)],
        compiler_params=_params(("parallel", "arbitrary"), est),
        name="diff_attention_sample",
    )(rel_bias, q, k, v, cache_k, cache_v, bias_c, bias_n, lam_params, subln2, qn2, kn2)


def _merge_kernel(ap_ref, as_ref, bp_ref, bs_ref, ga0_ref, ga1_ref, gb0_ref, gb1_ref, wa_ref, wb_ref, o_ref,
                  *, u, n_first):
    a = _group_tile((ap_ref, as_ref), n_first)
    b = _group_tile((bp_ref, bs_ref), n_first)
    ya = jnp.dot(a, wa_ref[...], preferred_element_type=F32)
    yb = jnp.dot(b, wb_ref[...], preferred_element_type=F32)
    for c, (ga, gb) in enumerate(((ga0_ref, gb0_ref), (ga1_ref, gb1_ref))):
        sl = slice(c * u, (c + 1) * u)
        o_ref[:, sl] = (ga[...].astype(F32) * ya[:, sl] + gb[...].astype(F32) * yb[:, sl]).astype(o_ref.dtype)


def _merge(ret_outs, dif_outs, gates, w_ret_up, w_dif_up, layer, tm):
    m = gates.shape[0]
    u = ret_outs[0].shape[1]
    d = 2 * u
    n_first = ret_outs[0].shape[0] // tm
    gate = lambda c: pl.BlockSpec((tm, u), lambda i: (i, c))
    rows = _group_specs(tm, u, n_first)
    wspec = pl.BlockSpec((None, u, d), lambda i: (layer, 0, 0))
    pbytes = jnp.dtype(gates.dtype).itemsize
    est = 2 * (8 * tm * u * pbytes + 2 * u * d * pbytes + tm * d * pbytes) + 3 * tm * d * 4
    return pl.pallas_call(
        functools.partial(_merge_kernel, u=u, n_first=n_first),
        grid=(m // tm,),
        in_specs=rows + rows + [gate(0), gate(1), gate(2), gate(3), wspec, wspec],
        out_specs=pl.BlockSpec((tm, d), lambda i: (i, 0)),
        out_shape=jax.ShapeDtypeStruct((m, d), gates.dtype),
        compiler_params=_params(("arbitrary",), est),
        name="gated_merge",
    )(*ret_outs, *dif_outs, gates, gates, gates, gates, w_ret_up, w_dif_up)


def _out_proj_kernel(x_ref, a_ref, w_ref, o_ref):
    o_ref[...] = x_ref[...] + jnp.dot(a_ref[...], w_ref[...], preferred_element_type=F32)


def _out_proj(x, merged, w_out, layer, tm):
    m, d = x.shape
    wbytes = jnp.dtype(w_out.dtype).itemsize
    est = 2 * (2 * tm * d * 4 + tm * d * wbytes + d * d * wbytes) + tm * d * 4
    return pl.pallas_call(
        _out_proj_kernel,
        grid=(m // tm,),
        in_specs=[
            pl.BlockSpec((tm, d), lambda i: (i, 0)),
            pl.BlockSpec((tm, d), lambda i: (i, 0)),
            pl.BlockSpec((None, d, d), lambda i: (layer, 0, 0)),
        ],
        out_specs=pl.BlockSpec((tm, d), lambda i: (i, 0)),
        out_shape=jax.ShapeDtypeStruct((m, d), F32),
        compiler_params=_params(("parallel",), est),
        name="output_projection",
    )(x, merged, w_out)


def _rotary_tables(pos):
    inv = ROPE_BASE ** (-jnp.arange(HALF, dtype=F32) / HALF)
    ang = pos.astype(F32)[:, None] * inv[None, :]
    cos, sin = jnp.cos(ang), jnp.sin(ang)
    return jnp.concatenate([cos, cos], axis=-1), jnp.concatenate([-sin, sin], axis=-1)


def kernel(x_prompt, x_sample, cache_diff_k, cache_diff_v, state_ret, ffn1_norm, ffn1_gate, ffn1_up, ffn1_down, mix_norm, w_in, q_norm, k_norm, lambda_q1, lambda_k1, lambda_q2, lambda_k2, subln, w_ret_up, w_dif_up, w_out, ffn2_norm, ffn2_gate, ffn2_up, ffn2_down, rel_bias):
    nb, seq, d = x_prompt.shape
    db, dseq, _ = x_sample.shape
    depth, _, past, nh_d, _ = cache_diff_k.shape
    nh_r = state_ret.shape[2]
    u = d // 2
    assert nh_r * HEAD == u and nh_d * HEAD == u and dseq == CHUNK and seq % CHUNK == 0
    assert w_in.shape[-1] == N_SEG * u
    mp, ms = nb * seq, db * dseq

    cast = lambda w: w.astype(MXU_DTYPE)
    wg1, wu1, wd1 = cast(ffn1_gate), cast(ffn1_up), cast(ffn1_down)
    wg2, wu2, wd2 = cast(ffn2_gate), cast(ffn2_up), cast(ffn2_down)
    w_ret_c, w_dif_c, w_out_c = cast(w_ret_up), cast(w_dif_up), cast(w_out)
    row3 = lambda g: g.reshape(depth, 1, g.shape[-1])
    n1, nmix, n2 = row3(ffn1_norm), row3(mix_norm), row3(ffn2_norm)
    qn2 = row3(jnp.concatenate([q_norm, q_norm], axis=-1))
    kn2 = row3(jnp.concatenate([k_norm, k_norm], axis=-1))
    subln2 = row3(subln)
    lam_params = jnp.stack([lambda_q1, lambda_k1, lambda_q2, lambda_k2], axis=1)
    cache_k = cache_diff_k.reshape(depth, db * past * nh_d, HEAD)
    cache_v = cache_diff_v.reshape(depth, db * past * nh_d, HEAD)

    pos = jnp.concatenate([jnp.tile(jnp.arange(seq, dtype=jnp.int32), nb),
                           past + jnp.tile(jnp.arange(dseq, dtype=jnp.int32), db)])
    rot = _rotary_tables(pos)

    tm = _pick(math.gcd(mp, ms), ROW_TILES)
    tm_proj = _pick(mp + ms, PROJ_ROW_TILES)
    groups = (mp, ms)
    zero_state = jnp.zeros((nb, nh_r, HEAD, HEAD), F32)
    t_ret = _pick(seq, RET_BLOCKS)

    x = (x_prompt.reshape(mp, d), x_sample.reshape(ms, d))
    kps, kss, vps, vss, states_p, states_s = [], [], [], [], [], []
    for l in range(depth):
        lam_init = 0.8 - 0.6 * math.exp(-0.3 * l)
        x, h = _ffn(x, n1, wg1, wu1, wd1, l, tm, next_gain=nmix)

        last = l == depth - 1
        seg = functools.partial(_segment, h, w_in, l)
        rq = seg(SEG_RQ, 1, tm_proj, "rotary", tables=rot)
        rk = seg(SEG_RK, 1, tm_proj, "rotary", tables=rot, scale=HEAD ** -0.5)
        rv = seg(SEG_RV, 1, tm_proj, "cast")
        rg = seg(SEG_RG, 1, tm_proj, "silu")
        dq = seg(SEG_DQ, 1, tm_proj, "norm", gain=qn2, scale=HALF ** -0.5)
        dk, k_p, k_s = seg(SEG_DK, 1, tm, "norm_keep", gain=kn2, group_rows=groups,
                           prev_prompt=kps if last else ())
        dv, v_p, v_s = seg(SEG_DV, 1, tm, "keep", group_rows=groups, prev_prompt=vps if last else ())
        gates = seg(SEG_GATES, N_SEG - SEG_GATES, tm_proj, "sigmoid")

        ret_p, st_p = _retention(rq, rk, rv, rg, zero_state, t_ret, nb, seq // t_ret, 0)
        ret_s, st_s = _retention(rq, rk, rv, rg, state_ret[l].astype(F32), dseq, db, 1, mp // dseq)

        dif_p = _attn_prompt(dq, dk, dv, rel_bias, lam_params, subln2, qn2, kn2, l, nb, seq, nh_d, lam_init)
        dif_s = _attn_sample(dq, dk, dv, cache_k, cache_v, rel_bias, lam_params, subln2, qn2, kn2, l, db, dseq,
                             past, mp, nh_d, lam_init)

        merged = _merge((ret_p, ret_s), (dif_p, dif_s), gates, w_ret_c, w_dif_c, l, tm)
        x = _out_proj(x, merged, w_out_c, l, tm)
        x = _ffn(x, n2, wg2, wu2, wd2, l, tm, split_out=groups if l == depth - 1 else None)

        for acc, val in ((kps, k_p), (kss, k_s), (vps, v_p), (vss, v_s), (states_p, st_p), (states_s, st_s)):
            acc.append(val)

    y_p, y_s = x
    kv_p = lambda parts: parts[-1].reshape(depth, nb, seq, nh_d, HEAD)
    kv_s = lambda parts: jnp.stack(parts).reshape(depth, db, dseq, nh_d, HEAD)
    return (y_p.reshape(nb, seq, d), y_s.reshape(db, dseq, d),
            kv_p(kps).astype(cache_diff_k.dtype), kv_p(vps).astype(cache_diff_v.dtype),
            jnp.stack(states_p).astype(state_ret.dtype),
            kv_s(kss).astype(cache_diff_k.dtype), kv_s(vss).astype(cache_diff_v.dtype),
            jnp.stack(states_s).astype(state_ret.dtype))
```

```python
import functools
import math

import numpy as np
import jax
import jax.numpy as jnp
from jax import lax
from jax.experimental import pallas as pl
from jax.experimental.pallas import tpu as pltpu

F32 = jnp.float32
MXU_DTYPE = jnp.bfloat16

CHUNK = 64
HEAD = 128
HALF = HEAD // 2
ROPE_BASE = 10000.0
N_BUCKETS = 32
MAX_DISTANCE = 128
EPS = 1e-6
MASK_VALUE = -1e30
SCORE_BOUND_MARGIN = 1.03
MAX_SHIFT_BRACKET = 100.0

VMEM_LIMIT_CAP = 60 * 1024 * 1024
MIB = 1024 * 1024

ROW_TILES = (512, 256, 128, 64)
PROJ_ROW_TILES = (1536, 1024, 768, 512, 256, 128, 64)
FF_TILES = (512, 256, 128)
ATTN_BLOCKS = (512, 256, 128)
HEADS_PER_STEP = (4, 2, 1)
RET_BLOCKS = (256, 128, 64)
CACHE_TILES = (2048, 1024, 512, 256, 128, 64)


def _pick(n, prefs):
    for p in prefs:
        if n % p == 0:
            return p
    raise ValueError(f"no tile in {prefs} divides {n}")


def _params(semantics, est_bytes):
    limit = int(min(max(est_bytes + 8 * MIB, 32 * MIB), VMEM_LIMIT_CAP))
    return pltpu.CompilerParams(dimension_semantics=semantics, vmem_limit_bytes=limit)


def _rms_scale(x):
    return lax.rsqrt(jnp.mean(x * x, axis=-1, keepdims=True) + EPS)


def _group_specs(tm, width, n_first, row_axis=0):
    first = pl.BlockSpec((tm, width), lambda *g: (jnp.minimum(g[row_axis], n_first - 1), 0))
    second = pl.BlockSpec((tm, width), lambda *g: (jnp.maximum(g[row_axis] - n_first, 0), 0),
                          pipeline_mode=pl.Buffered(1))
    return [first, second]


def _group_tile(refs, n_first, row_axis=0):
    if len(refs) == 1:
        return refs[0][...]
    return jnp.where(pl.program_id(row_axis) < n_first, refs[0][...], refs[1][...])


def _store_group_tile(refs, n_first, value, row_axis=0):
    if len(refs) == 1:
        refs[0][...] = value
        return
    i = pl.program_id(row_axis)

    @pl.when(i < n_first)
    def _():
        refs[0][...] = value

    @pl.when(i >= n_first)
    def _():
        refs[1][...] = value


def _ffn_kernel(*refs, nf, n_in, n_out, n_first, norm_out):
    x_refs = refs[:n_in]
    g_ref, wg_ref, wu_ref, wd_ref = refs[n_in:n_in + 4]
    pos = n_in + 4
    g2_ref = refs[pos] if norm_out else None
    pos += int(norm_out)
    o_refs = refs[pos:pos + n_out]
    pos += n_out
    hn_ref = refs[pos] if norm_out else None
    pos += int(norm_out)
    h_ref, acc_ref = refs[pos:]
    f = pl.program_id(1)

    @pl.when(f == 0)
    def _():
        x = _group_tile(x_refs, n_first)
        h_ref[...] = (x * _rms_scale(x) * g_ref[...]).astype(h_ref.dtype)
        acc_ref[...] = jnp.zeros_like(acc_ref)

    h = h_ref[...]
    gate = jnp.dot(h, wg_ref[...], preferred_element_type=F32)
    up = jnp.dot(h, wu_ref[...], preferred_element_type=F32)
    act = (gate * jax.nn.sigmoid(gate) * up).astype(wd_ref.dtype)
    acc_ref[...] += jnp.dot(act, wd_ref[...], preferred_element_type=F32)

    @pl.when(f == nf - 1)
    def _():
        y = _group_tile(x_refs, n_first) + 0.5 * acc_ref[...]
        _store_group_tile(o_refs, n_first, y)
        if norm_out:
            hn_ref[...] = (y * _rms_scale(y) * g2_ref[...]).astype(hn_ref.dtype)


def _ffn(xs, gain, wg, wu, wd, layer, tm, split_out=None, next_gain=None):
    xs = tuple(xs) if isinstance(xs, (tuple, list)) else (xs,)
    m = sum(x.shape[0] for x in xs)
    d = xs[0].shape[1]
    n_first = (xs[0].shape[0] if len(xs) == 2 else split_out[0] if split_out else m) // tm
    ff = wg.shape[-1]
    tf = _pick(ff, FF_TILES)
    nf = ff // tf
    norm_out = next_gain is not None
    wbytes = jnp.dtype(wg.dtype).itemsize
    row_buffers = (3 if len(xs) == 2 else 2) + (3 if split_out else 2)
    est = (row_buffers * tm * d * 4 + 2 * (3 * d * tf * wbytes + int(norm_out) * tm * d * wbytes)
           + tm * d * (4 + wbytes) + 4 * tm * tf * 4)
    rows = lambda: pl.BlockSpec((tm, d), lambda i, f: (i, 0))
    gain_spec = pl.BlockSpec((None, 1, d), lambda i, f: (layer, 0, 0))
    in_specs = (_group_specs(tm, d, n_first) if len(xs) == 2 else [rows()]) + [
        gain_spec,
        pl.BlockSpec((None, d, tf), lambda i, f: (layer, 0, f)),
        pl.BlockSpec((None, d, tf), lambda i, f: (layer, 0, f)),
        pl.BlockSpec((None, tf, d), lambda i, f: (layer, f, 0)),
    ]
    args = [*xs, gain, wg, wu, wd]
    if split_out:
        out_specs = _group_specs(tm, d, n_first)
        out_shape = [jax.ShapeDtypeStruct((r, d), F32) for r in split_out]
    else:
        out_specs = [rows()]
        out_shape = [jax.ShapeDtypeStruct((m, d), F32)]
    if norm_out:
        in_specs.append(gain_spec)
        args.append(next_gain)
        out_specs.append(rows())
        out_shape.append(jax.ShapeDtypeStruct((m, d), wg.dtype))
    outs = pl.pallas_call(
        functools.partial(_ffn_kernel, nf=nf, n_in=len(xs), n_out=2 if split_out else 1, n_first=n_first,
                          norm_out=norm_out),
        grid=(m // tm, nf),
        in_specs=in_specs,
        out_specs=out_specs,
        out_shape=out_shape,
        scratch_shapes=[pltpu.VMEM((tm, d), wg.dtype), pltpu.VMEM((tm, d), F32)],
        compiler_params=_params(("arbitrary", "arbitrary"), est),
        name="swiglu_half_step",
    )(*args)
    return outs if len(outs) > 1 else outs[0]


SEG_RQ, SEG_RK, SEG_RV, SEG_RG, SEG_DQ, SEG_DK, SEG_DV, SEG_GATES, N_SEG = 0, 1, 2, 3, 4, 5, 6, 7, 11


def _rotate_half_pairs(a, cos2, sin2):
    return a * cos2 + pltpu.roll(a, HALF, 1) * sin2


def _component_rms_norm(a, gain2):
    lo = lax.broadcasted_iota(jnp.int32, a.shape, 1) < HALF
    sq = a * a
    s_all = jnp.sum(sq, axis=-1, keepdims=True)
    s_lo = jnp.sum(jnp.where(lo, sq, 0.0), axis=-1, keepdims=True)
    ms = jnp.where(lo, s_lo, s_all - s_lo) * (1.0 / HALF)
    return a * lax.rsqrt(ms + EPS) * gain2


def _keep_f32(fp_ref, fs_ref, prev_refs, n_first, value):
    i = pl.program_id(1)

    @pl.when(i < n_first)
    def _():
        if prev_refs:
            for l, prev in enumerate(prev_refs):
                fp_ref[l] = prev[...]
            fp_ref[len(prev_refs)] = value
        else:
            fp_ref[...] = value

    @pl.when(i >= n_first)
    def _():
        fs_ref[...] = value


def _segment_kernel(h_ref, w_ref, *refs, kind, scale, nheads, n_first, n_prev):
    *refs, wc_ref = refs

    @pl.when(pl.program_id(1) == 0)
    def _():
        wc_ref[...] = w_ref[...].astype(wc_ref.dtype)

    acc = jnp.dot(h_ref[...], wc_ref[...], preferred_element_type=F32)
    heads = [slice(h * HEAD, (h + 1) * HEAD) for h in range(nheads)]
    if kind == "rotary":
        cos_ref, sin_ref, o_ref = refs
        for sl in heads:
            r = _rotate_half_pairs(acc[:, sl], cos_ref[...], sin_ref[...])
            o_ref[:, sl] = (r if scale == 1.0 else r * scale).astype(o_ref.dtype)
    elif kind == "cast":
        (o_ref,) = refs
        o_ref[...] = acc.astype(o_ref.dtype)
    elif kind == "silu":
        (o_ref,) = refs
        o_ref[...] = (acc * jax.nn.sigmoid(acc)).astype(o_ref.dtype)
    elif kind == "sigmoid":
        (o_ref,) = refs
        o_ref[...] = jax.nn.sigmoid(acc).astype(o_ref.dtype)
    elif kind == "norm":
        gain_ref, o_ref = refs
        for sl in heads:
            o_ref[:, sl] = (_component_rms_norm(acc[:, sl], gain_ref[...]) * scale).astype(o_ref.dtype)
    elif kind == "norm_keep":
        gain_ref, *prev_refs, o_ref, fp_ref, fs_ref = refs
        normed = jnp.concatenate([_component_rms_norm(acc[:, sl], gain_ref[...]) for sl in heads], axis=1)
        _keep_f32(fp_ref, fs_ref, prev_refs, n_first, normed)
        o_ref[...] = normed.astype(o_ref.dtype)
    elif kind == "keep":
        *prev_refs, o_ref, fp_ref, fs_ref = refs
        _keep_f32(fp_ref, fs_ref, prev_refs, n_first, acc)
        o_ref[...] = acc.astype(o_ref.dtype)
    else:
        raise ValueError(kind)
    assert kind not in ("norm_keep", "keep") or len(prev_refs) == n_prev


def _segment(h, w_in, layer, seg0, nseg, tm, kind, *, scale=1.0, tables=(), gain=None, group_rows=None,
             prev_prompt=()):
    m, d = h.shape
    u = d // 2
    nheads = u // HEAD
    n_first = group_rows[0] // tm if group_rows else 0
    n_prev = len(prev_prompt)
    in_specs = [pl.BlockSpec((tm, d), lambda s, i: (i, 0)),
                pl.BlockSpec((None, d, u), lambda s, i: (layer, 0, seg0 + s))]
    args = [h, w_in]
    for tab in tables:
        in_specs.append(pl.BlockSpec((tm, HEAD), lambda s, i: (i, 0)))
        args.append(tab)
    if gain is not None:
        in_specs.append(pl.BlockSpec((None, 1, HEAD), lambda s, i: (layer, 0, 0)))
        args.append(gain)
    out_specs = [pl.BlockSpec((tm, u), lambda s, i: (i, s))]
    out_shape = [jax.ShapeDtypeStruct((m, nseg * u), h.dtype)]
    if group_rows:
        first, second = _group_specs(tm, u, n_first, row_axis=1)
        in_specs += [first] * n_prev
        args += list(prev_prompt)
        if n_prev:
            first = pl.BlockSpec((n_prev + 1, tm, u), lambda s, i: (0, jnp.minimum(i, n_first - 1), 0))
        out_specs += [first, second]
        out_shape += [jax.ShapeDtypeStruct(((n_prev + 1, group_rows[0], u) if n_prev else (group_rows[0], u)), F32),
                      jax.ShapeDtypeStruct((group_rows[1], u), F32)]
    hb = jnp.dtype(h.dtype).itemsize
    est = (2 * (tm * d * hb + d * u * 4 + tm * u * hb + (2 * n_prev + 2) * tm * u * 4 + 2 * tm * HEAD * 4)
           + d * u * hb + 4 * tm * u * 4)
    outs = pl.pallas_call(
        functools.partial(_segment_kernel, kind=kind, scale=scale, nheads=nheads, n_first=n_first, n_prev=n_prev),
        grid=(nseg, m // tm),
        in_specs=in_specs,
        out_specs=out_specs,
        out_shape=out_shape,
        scratch_shapes=[pltpu.VMEM((d, u), h.dtype)],
        compiler_params=_params(("arbitrary", "arbitrary"), est),
        name="input_projection_" + kind,
    )(*args)
    return outs if len(outs) > 1 else outs[0]


def _retention_kernel(q_ref, k_ref, v_ref, g_ref, s0_ref, d_ref, wq_ref, we_ref, dec_ref, o_ref, sout_ref, st_ref,
                      *, nheads, nblk):
    t = pl.program_id(1)

    @pl.when(t == 0)
    def _():
        st_ref[...] = s0_ref[...]

    lanes = [slice(h * HEAD, (h + 1) * HEAD) for h in range(nheads)]
    scores, inter = [], []
    for h, sl in enumerate(lanes):
        q, k, v = q_ref[:, sl], k_ref[:, sl], v_ref[:, sl]
        state = st_ref[h]
        scores.append(lax.dot_general(q, k, (((1,), (1,)), ((), ())), preferred_element_type=F32))
        inter.append(jnp.dot(q, state.astype(q.dtype), preferred_element_type=F32))
        kw = (k.astype(F32) * we_ref[:, sl]).astype(k.dtype)
        kv = lax.dot_general(kw, v, (((0,), (0,)), ((), ())), preferred_element_type=F32)
        st_ref[h] = state * dec_ref[h:h + 1, :] + kv
    for h, sl in enumerate(lanes):
        v = v_ref[:, sl]
        s = scores[h] * d_ref[h]
        o = jnp.dot(s.astype(v.dtype), v, preferred_element_type=F32) + wq_ref[:, sl] * inter[h]
        r = o * _rms_scale(o)
        o_ref[:, sl] = (r * g_ref[:, sl].astype(F32)).astype(o_ref.dtype)

    @pl.when(t == nblk - 1)
    def _():
        sout_ref[...] = st_ref[...]


def _retention_tables(t, nheads):
    log_g = jnp.log(1.0 - 2.0 ** (-5.0 - jnp.arange(nheads, dtype=F32)))
    idx = jnp.arange(t, dtype=F32)
    dist = jnp.abs(idx[:, None] - idx[None, :])
    ci = np.arange(t) // CHUNK
    visible = jnp.asarray(ci[None, :] <= ci[:, None])
    dmat = jnp.where(visible[None], jnp.exp(log_g[:, None, None] * dist[None]), 0.0)
    wq = jnp.exp(log_g[None, :] * (idx + 1.0)[:, None])
    we = jnp.exp(log_g[None, :] * (t - 1.0 - idx)[:, None])
    dec = jnp.exp(log_g * t)
    expand = lambda a: jnp.repeat(a, HEAD, axis=1)
    return dmat, expand(wq), expand(we), jnp.broadcast_to(dec[:, None], (nheads, HEAD))


def _retention(q, k, v, g, s0, t, nbatch, nblk, row_block0):
    nheads = s0.shape[1]
    u = nheads * HEAD
    dmat, wq, we, dec = _retention_tables(t, nheads)
    rows = pl.BlockSpec((t, u), lambda b, i: (row_block0 + b * nblk + i, 0))
    whole = lambda a: pl.BlockSpec(a.shape, lambda b, i: (0,) * a.ndim)
    state_spec = pl.BlockSpec((None, nheads, HEAD, HEAD), lambda b, i: (b, 0, 0, 0))
    pbytes = jnp.dtype(q.dtype).itemsize
    est = (2 * (5 * t * u * pbytes + 2 * nheads * HEAD * HEAD * 4 + nheads * t * t * 4 + 2 * t * u * 4)
           + nheads * HEAD * HEAD * 4 + 6 * t * max(t, HEAD) * 4)
    return pl.pallas_call(
        functools.partial(_retention_kernel, nheads=nheads, nblk=nblk),
        grid=(nbatch, nblk),
        in_specs=[rows, rows, rows, rows, state_spec, whole(dmat), whole(wq), whole(we), whole(dec)],
        out_specs=[
            pl.BlockSpec((t, u), lambda b, i: (b * nblk + i, 0)),
            state_spec,
        ],
        out_shape=[
            jax.ShapeDtypeStruct((nbatch * nblk * t, u), q.dtype),
            jax.ShapeDtypeStruct((nbatch, nheads, HEAD, HEAD), F32),
        ],
        scratch_shapes=[pltpu.VMEM((nheads, HEAD, HEAD), F32)],
        compiler_params=_params(("parallel", "arbitrary"), est),
        name="retention",
    )(q, k, v, g, s0, dmat, wq, we, dec)


def _bucket_thresholds():
    nb = N_BUCKETS // 2
    me = nb // 2
    out = []
    for k in range(1, nb - me):
        n = me
        while n ** (nb - me) * me ** k < me ** (nb - me) * MAX_DISTANCE ** k:
            n += 1
        out.append(n)
    return out


def _t5_bucket_np(rel):
    nb = N_BUCKETS // 2
    me = nb // 2
    n = np.abs(rel)
    large = np.full(rel.shape, me, np.int64)
    for thr in _bucket_thresholds():
        large += (n >= thr)
    large = np.minimum(large, nb - 1)
    return (np.where(rel > 0, nb, 0) + np.where(n < me, n, large)).astype(np.int32)


def _bias_kernel(rb_ref, idx_ref, mask_ref, o_ref):
    h = pl.program_id(0)
    idx = idx_ref[...]
    out = mask_ref[...]
    for b in range(N_BUCKETS):
        out = out + jnp.where(idx == b, rb_ref[b, h], 0.0)
    o_ref[...] = out


def _bias_table(rel_bias, qpos, kpos):
    nheads = rel_bias.shape[1]
    rel = kpos[None, :] - qpos[:, None]
    idx = jnp.asarray(_t5_bucket_np(rel))
    mask = jnp.asarray(np.where((kpos[None, :] // CHUNK) <= (qpos[:, None] // CHUNK), 0.0, MASK_VALUE)
                       .astype(np.float32))
    nq, nk = rel.shape
    return pl.pallas_call(
        _bias_kernel,
        grid=(nheads,),
        in_specs=[
            pl.BlockSpec(memory_space=pltpu.SMEM),
            pl.BlockSpec((nq, nk), lambda h: (0, 0)),
            pl.BlockSpec((nq, nk), lambda h: (0, 0)),
        ],
        out_specs=pl.BlockSpec((None, nq, nk), lambda h: (h, 0, 0)),
        out_shape=jax.ShapeDtypeStruct((nheads, nq, nk), F32),
        compiler_params=_params(("arbitrary",), 6 * nq * nk * 4),
        name="relative_bias_table",
    )(rel_bias, idx, mask)


def _stack_components(q):
    lo = lax.broadcasted_iota(jnp.int32, q.shape, 1) < HALF
    zero = jnp.zeros_like(q)
    return jnp.concatenate([jnp.where(lo, q, zero), jnp.where(lo, zero, q)], axis=0)


def _biased_scores(qs, k, bias):
    s = lax.dot_general(qs, k, (((1,), (1,)), ((), ())), preferred_element_type=F32)
    if bias.ndim == 2:
        t = bias.shape[0]
        return jnp.concatenate([s[:t] + bias, s[t:] + bias], axis=0), 0.0
    return s, bias


def _softmax_step(qs, k, v, bias, stats, g):
    m_ref, l_ref, acc_ref = stats
    s, c = _biased_scores(qs, k, bias)
    m_prev = m_ref[g]
    m_new = jnp.maximum(m_prev, jnp.max(s, axis=-1, keepdims=True) + c)
    alpha = jnp.exp(m_prev - m_new)
    p = jnp.exp(s - (m_new - c))
    l_ref[g] = alpha * l_ref[g] + jnp.sum(p, axis=-1, keepdims=True)
    acc_ref[g] = alpha * acc_ref[g] + jnp.dot(p.astype(v.dtype), v, preferred_element_type=F32)
    m_ref[g] = m_new


def _softmax_init(stats):
    m_ref, l_ref, acc_ref = stats
    m_ref[...] = jnp.full_like(m_ref, MASK_VALUE)
    l_ref[...] = jnp.zeros_like(l_ref)
    acc_ref[...] = jnp.zeros_like(acc_ref)


def _softmax_scratch(groups, t):
    return [pltpu.VMEM((groups, 2 * t, 1), F32), pltpu.VMEM((groups, 2 * t, 1), F32),
            pltpu.VMEM((groups, 2 * t, HEAD), F32)]


def _lambda_value(lam_ref, lam_init):
    a = lam_ref[...]
    e1 = jnp.exp(jnp.sum(a[0:1] * a[1:2], axis=-1, keepdims=True))
    e2 = jnp.exp(jnp.sum(a[2:3] * a[3:4], axis=-1, keepdims=True))
    return e1 - e2 + lam_init


def _diff_finish(t, lam, lam_init, subln, stats, g):
    _, l_ref, acc_ref = stats
    acc = acc_ref[g]
    l = l_ref[g]
    o = acc[:t] / l[:t] - lam * (acc[t:] / l[t:])
    return o * _rms_scale(o) * subln * (1.0 - lam_init)


def _shift_brackets(rb_ref, qg_ref, kg_ref, qs, own_ks, head0, m_ref):
    gain_bound = (SCORE_BOUND_MARGIN * HALF ** 0.5 * jnp.max(jnp.abs(qg_ref[...]), axis=-1, keepdims=True)
                  * jnp.max(jnp.abs(kg_ref[...]), axis=-1, keepdims=True))
    widest = jnp.zeros((1, 1), F32)
    for g, (q, own_k) in enumerate(zip(qs, own_ks)):
        head = head0 + g
        bias_max = rb_ref[0, head]
        for b in range(1, N_BUCKETS):
            bias_max = jnp.maximum(bias_max, rb_ref[b, head])
        upper = gain_bound + bias_max
        own_k = own_k.astype(F32)
        own = jnp.sum(q.astype(F32) * jnp.concatenate([own_k, own_k], axis=0), axis=-1, keepdims=True)
        lower = own + rb_ref[0, head]
        m_ref[g] = 0.5 * (upper + lower)
        widest = jnp.maximum(widest, jnp.max(upper - lower, axis=0, keepdims=True))
    return widest[0, 0]


def _attn_prompt_kernel(rb_ref, q_ref, k_ref, v_ref, bias_ref, lam_ref, sub_ref, qg_ref, kg_ref, o_ref, *stats,
                        t, groups, lam_init, far_bucket):
    hp = pl.program_id(1)
    qi = pl.program_id(2)
    m_ref, l_ref, acc_ref = stats
    lanes = [slice(g * HEAD, (g + 1) * HEAD) for g in range(groups)]
    qs = [_stack_components(q_ref[:, sl]) for sl in lanes]
    far_bias = [rb_ref[far_bucket, hp * groups + g] for g in range(groups)]
    _softmax_init(stats)

    def sweep(step):
        def run(j, bias_of):
            step(pl.ds(pl.multiple_of(j * t, t), t), [bias_of(g) for g in range(groups)])

        def far_step(j, carry):
            run(j, lambda g: far_bias[g])
            return carry

        lax.fori_loop(0, jnp.maximum(qi - 1, 0), far_step, 0)

        @pl.when(qi > 0)
        def _():
            run(qi - 1, lambda g: bias_ref[g, :, :t])

        run(qi, lambda g: bias_ref[g, :, t:])

    def max_step(rows, biases):
        scores = [_biased_scores(qs[g], k_ref[rows, lanes[g]], biases[g]) for g in range(groups)]
        for g, (s, c) in enumerate(scores):
            m_ref[g] = jnp.maximum(m_ref[g], jnp.max(s, axis=-1, keepdims=True) + c)

    def acc_step(rows, biases):
        scores = [_biased_scores(qs[g], k_ref[rows, lanes[g]], biases[g]) for g in range(groups)]
        probs = [jnp.exp(s - (m_ref[g] - c)) for g, (s, c) in enumerate(scores)]
        for g, p in enumerate(probs):
            l_ref[g] += jnp.sum(p, axis=-1, keepdims=True)
            v = v_ref[rows, lanes[g]]
            acc_ref[g] += jnp.dot(p.astype(v.dtype), v, preferred_element_type=F32)

    own_rows = pl.ds(pl.multiple_of(qi * t, t), t)
    widest = _shift_brackets(rb_ref, qg_ref, kg_ref, qs, [k_ref[own_rows, sl] for sl in lanes],
                             hp * groups, m_ref)

    @pl.when(widest > MAX_SHIFT_BRACKET)
    def _():
        m_ref[...] = jnp.full_like(m_ref, MASK_VALUE)
        sweep(max_step)

    sweep(acc_step)

    lam = _lambda_value(lam_ref, lam_init)
    for g, sl in enumerate(lanes):
        o_ref[:, sl] = _diff_finish(t, lam, lam_init, sub_ref[...], stats, g).astype(o_ref.dtype)


def _attn_prompt(q, k, v, rel_bias, lam_params, subln2, qn2, kn2, layer, nbatch, seq, nheads, lam_init):
    u = nheads * HEAD
    t = _pick(seq, ATTN_BLOCKS)
    groups = _pick(nheads, HEADS_PER_STEP)
    assert t % CHUNK == 0 and t + 1 >= _bucket_thresholds()[-1]
    nq = seq // t
    w = groups * HEAD
    r = np.arange(t)
    bias = _bias_table(rel_bias, r + t, np.arange(2 * t))
    pbytes = jnp.dtype(q.dtype).itemsize
    est = (2 * (2 * t * w * pbytes + 2 * seq * w * pbytes + groups * 2 * t * t * 4)
           + groups * (2 * t * (HEAD + 2 * 128) * 4 + 6 * 2 * t * t * 4))
    return pl.pallas_call(
        functools.partial(_attn_prompt_kernel, t=t, groups=groups, lam_init=lam_init,
                          far_bucket=N_BUCKETS // 2 - 1),
        grid=(nbatch, nheads // groups, nq),
        in_specs=[
            pl.BlockSpec(memory_space=pltpu.SMEM),
            pl.BlockSpec((t, w), lambda b, h, i: (b * nq + i, h)),
            pl.BlockSpec((seq, w), lambda b, h, i: (b, h)),
            pl.BlockSpec((seq, w), lambda b, h, i: (b, h)),
            pl.BlockSpec((groups, t, 2 * t), lambda b, h, i: (h, 0, 0)),
            pl.BlockSpec((None, 4, HALF), lambda b, h, i: (layer, 0, 0)),
            pl.BlockSpec((None, 1, HEAD), lambda b, h, i: (layer, 0, 0)),
            pl.BlockSpec((None, 1, HEAD), lambda b, h, i: (layer, 0, 0)),
            pl.BlockSpec((None, 1, HEAD), lambda b, h, i: (layer, 0, 0)),
        ],
        out_specs=pl.BlockSpec((t, w), lambda b, h, i: (b * nq + i, h)),
        out_shape=jax.ShapeDtypeStruct((nbatch * seq, u), q.dtype),
        scratch_shapes=_softmax_scratch(groups, t),
        compiler_params=_params(("parallel", "parallel", "arbitrary"), est),
        name="diff_attention_prompt",
    )(rel_bias, q, k, v, bias, lam_params, subln2, qn2, kn2)


def _attn_sample_kernel(rb_ref, q_ref, kn_ref, vn_ref, kc_ref, vc_ref, bc_ref, bn_ref, lam_ref, sub_ref,
                        qg_ref, kg_ref, o_ref, *stats, t, tk, ncache, nheads, lam_init):
    *stats, wide_ref = stats
    m_ref, l_ref, acc_ref = stats
    j = pl.program_id(1)
    lanes = [slice(h * HEAD, (h + 1) * HEAD) for h in range(nheads)]
    heads = range(nheads)
    qs = [_stack_components(q_ref[:, sl]) for sl in lanes]

    @pl.when(j == 0)
    def _():
        _softmax_init(stats)
        widest = _shift_brackets(rb_ref, qg_ref, kg_ref, qs, [kn_ref[:, sl] for sl in lanes], 0, m_ref)
        wide_ref[0] = (widest > MAX_SHIFT_BRACKET).astype(jnp.int32)

        @pl.when(widest > MAX_SHIFT_BRACKET)
        def _():
            m_ref[...] = jnp.full_like(m_ref, MASK_VALUE)

    def fixed_shift_steps(ks, vs, biases):
        scores = [_biased_scores(qs[h], ks[h], biases[h])[0] for h in heads]
        probs = [jnp.exp(scores[h] - m_ref[h]) for h in heads]
        for h in heads:
            l_ref[h] += jnp.sum(probs[h], axis=-1, keepdims=True)
            acc_ref[h] += jnp.dot(probs[h].astype(vs[h].dtype), vs[h], preferred_element_type=F32)

    def online_steps(ks, vs, biases):
        for h in heads:
            _softmax_step(qs[h], ks[h], vs[h], biases[h], stats, h)

    def both(ks, vs, biases):
        @pl.when(wide_ref[0] == 0)
        def _():
            fixed_shift_steps(ks(), vs(), biases())

        @pl.when(wide_ref[0] != 0)
        def _():
            online_steps(ks(), vs(), biases())

    head_rows = lambda ref: [ref[pl.ds(h, tk, stride=nheads), :].astype(qs[0].dtype) for h in heads]
    both(lambda: head_rows(kc_ref), lambda: head_rows(vc_ref), lambda: [bc_ref[h] for h in heads])

    @pl.when(j == ncache - 1)
    def _():
        both(lambda: [kn_ref[:, sl] for sl in lanes], lambda: [vn_ref[:, sl] for sl in lanes],
             lambda: [bn_ref[h] for h in heads])
        lam = _lambda_value(lam_ref, lam_init)
        for h, sl in enumerate(lanes):
            o_ref[:, sl] = _diff_finish(t, lam, lam_init, sub_ref[...], stats, h).astype(o_ref.dtype)


def _attn_sample(q, k, v, cache_k, cache_v, rel_bias, lam_params, subln2, qn2, kn2, layer, nbatch, t, past,
                 row0, nheads, lam_init):
    u = nheads * HEAD
    assert row0 % t == 0
    rb0 = row0 // t
    tk = _pick(past, CACHE_TILES)
    qpos = past + np.arange(t)
    ncache = past // tk
    bias_c = _bias_table(rel_bias, qpos, np.arange(past))
    bias_c = bias_c.reshape(nheads, t, ncache, tk).transpose(2, 0, 1, 3)
    bias_n = _bias_table(rel_bias, qpos, qpos)
    pbytes = jnp.dtype(q.dtype).itemsize
    est = (2 * (4 * t * u * pbytes + 2 * tk * nheads * HEAD * 4 + nheads * t * (tk + t) * 4)
           + nheads * (2 * t * (HEAD + 2 * 128) * 4 + 6 * 2 * t * tk * 4))
    rows = pl.BlockSpec((t, u), lambda b, j: (rb0 + b, 0))
    cache = pl.BlockSpec((None, tk * nheads, HEAD), lambda b, j: (layer, b * ncache + j, 0))
    return pl.pallas_call(
        functools.partial(_attn_sample_kernel, t=t, tk=tk, ncache=ncache, nheads=nheads, lam_init=lam_init),
        grid=(nbatch, ncache),
        in_specs=[
            pl.BlockSpec(memory_space=pltpu.SMEM),
            rows, rows, rows, cache, cache,
            pl.BlockSpec((None, nheads, t, tk), lambda b, j: (j, 0, 0, 0)),
            pl.BlockSpec((nheads, t, t), lambda b, j: (0, 0, 0)),
            pl.BlockSpec((None, 4, HALF), lambda b, j: (layer, 0, 0)),
            pl.BlockSpec((None, 1, HEAD), lambda b, j: (layer, 0, 0)),
            pl.BlockSpec((None, 1, HEAD), lambda b, j: (layer, 0, 0)),
            pl.BlockSpec((None, 1, HEAD), lambda b, j: (layer, 0, 0)),
        ],
        out_specs=pl.BlockSpec((t, u), lambda b, j: (b, 0)),
        out_shape=jax.ShapeDtypeStruct((nbatch * t, u), q.dtype),
        scratch_shapes=_softmax_scratch(nheads, t) + [pltpu.SMEM((1,), jnp.int32)],
        compiler_params=_params(("parallel", "arbitrary"), est),
        name="diff_attention_sample",
    )(rel_bias, q, k, v, cache_k, cache_v, bias_c, bias_n, lam_params, subln2, qn2, kn2)


def _merge_kernel(ap_ref, as_ref, bp_ref, bs_ref, ga0_ref, ga1_ref, gb0_ref, gb1_ref, wa_ref, wb_ref, o_ref,
                  *, u, n_first):
    a = _group_tile((ap_ref, as_ref), n_first)
    b = _group_tile((bp_ref, bs_ref), n_first)
    ya = jnp.dot(a, wa_ref[...], preferred_element_type=F32)
    yb = jnp.dot(b, wb_ref[...], preferred_element_type=F32)
    for c, (ga, gb) in enumerate(((ga0_ref, gb0_ref), (ga1_ref, gb1_ref))):
        sl = slice(c * u, (c + 1) * u)
        o_ref[:, sl] = (ga[...].astype(F32) * ya[:, sl] + gb[...].astype(F32) * yb[:, sl]).astype(o_ref.dtype)


def _merge(ret_outs, dif_outs, gates, w_ret_up, w_dif_up, layer, tm):
    m = gates.shape[0]
    u = ret_outs[0].shape[1]
    d = 2 * u
    n_first = ret_outs[0].shape[0] // tm
    gate = lambda c: pl.BlockSpec((tm, u), lambda i: (i, c))
    rows = _group_specs(tm, u, n_first)
    wspec = pl.BlockSpec((None, u, d), lambda i: (layer, 0, 0))
    pbytes = jnp.dtype(gates.dtype).itemsize
    est = 2 * (8 * tm * u * pbytes + 2 * u * d * pbytes + tm * d * pbytes) + 3 * tm * d * 4
    return pl.pallas_call(
        functools.partial(_merge_kernel, u=u, n_first=n_first),
        grid=(m // tm,),
        in_specs=rows + rows + [gate(0), gate(1), gate(2), gate(3), wspec, wspec],
        out_specs=pl.BlockSpec((tm, d), lambda i: (i, 0)),
        out_shape=jax.ShapeDtypeStruct((m, d), gates.dtype),
        compiler_params=_params(("arbitrary",), est),
        name="gated_merge",
    )(*ret_outs, *dif_outs, gates, gates, gates, gates, w_ret_up, w_dif_up)


def _out_proj_kernel(x_ref, a_ref, w_ref, o_ref):
    o_ref[...] = x_ref[...] + jnp.dot(a_ref[...], w_ref[...], preferred_element_type=F32)


def _out_proj(x, merged, w_out, layer, tm):
    m, d = x.shape
    wbytes = jnp.dtype(w_out.dtype).itemsize
    est = 2 * (2 * tm * d * 4 + tm * d * wbytes + d * d * wbytes) + tm * d * 4
    return pl.pallas_call(
        _out_proj_kernel,
        grid=(m // tm,),
        in_specs=[
            pl.BlockSpec((tm, d), lambda i: (i, 0)),
            pl.BlockSpec((tm, d), lambda i: (i, 0)),
            pl.BlockSpec((None, d, d), lambda i: (layer, 0, 0)),
        ],
        out_specs=pl.BlockSpec((tm, d), lambda i: (i, 0)),
        out_shape=jax.ShapeDtypeStruct((m, d), F32),
        compiler_params=_params(("parallel",), est),
        name="output_projection",
    )(x, merged, w_out)


def _rotary_tables(pos):
    inv = ROPE_BASE ** (-jnp.arange(HALF, dtype=F32) / HALF)
    ang = pos.astype(F32)[:, None] * inv[None, :]
    cos, sin = jnp.cos(ang), jnp.sin(ang)
    return jnp.concatenate([cos, cos], axis=-1), jnp.concatenate([-sin, sin], axis=-1)


def kernel(x_prompt, x_sample, cache_diff_k, cache_diff_v, state_ret, ffn1_norm, ffn1_gate, ffn1_up, ffn1_down, mix_norm, w_in, q_norm, k_norm, lambda_q1, lambda_k1, lambda_q2, lambda_k2, subln, w_ret_up, w_dif_up, w_out, ffn2_norm, ffn2_gate, ffn2_up, ffn2_down, rel_bias):
    nb, seq, d = x_prompt.shape
    db, dseq, _ = x_sample.shape
    depth, _, past, nh_d, _ = cache_diff_k.shape
    nh_r = state_ret.shape[2]
    u = d // 2
    assert nh_r * HEAD == u and nh_d * HEAD == u and dseq == CHUNK and seq % CHUNK == 0
    assert w_in.shape[-1] == N_SEG * u
    mp, ms = nb * seq, db * dseq

    cast = lambda w: w.astype(MXU_DTYPE)
    wg1, wu1, wd1 = cast(ffn1_gate), cast(ffn1_up), cast(ffn1_down)
    wg2, wu2, wd2 = cast(ffn2_gate), cast(ffn2_up), cast(ffn2_down)
    w_ret_c, w_dif_c, w_out_c = cast(w_ret_up), cast(w_dif_up), cast(w_out)
    row3 = lambda g: g.reshape(depth, 1, g.shape[-1])
    n1, nmix, n2 = row3(ffn1_norm), row3(mix_norm), row3(ffn2_norm)
    qn2 = row3(jnp.concatenate([q_norm, q_norm], axis=-1))
    kn2 = row3(jnp.concatenate([k_norm, k_norm], axis=-1))
    subln2 = row3(subln)
    lam_params = jnp.stack([lambda_q1, lambda_k1, lambda_q2, lambda_k2], axis=1)
    cache_k = cache_diff_k.reshape(depth, db * past * nh_d, HEAD)
    cache_v = cache_diff_v.reshape(depth, db * past * nh_d, HEAD)

    pos = jnp.concatenate([jnp.tile(jnp.arange(seq, dtype=jnp.int32), nb),
                           past + jnp.tile(jnp.arange(dseq, dtype=jnp.int32), db)])
    rot = _rotary_tables(pos)

    tm = _pick(math.gcd(mp, ms), ROW_TILES)
    tm_proj = _pick(mp + ms, PROJ_ROW_TILES)
    groups = (mp, ms)
    zero_state = jnp.zeros((nb, nh_r, HEAD, HEAD), F32)
    t_ret = _pick(seq, RET_BLOCKS)

    x = (x_prompt.reshape(mp, d), x_sample.reshape(ms, d))
    kps, kss, vps, vss, states_p, states_s = [], [], [], [], [], []
    for l in range(depth):
        lam_init = 0.8 - 0.6 * math.exp(-0.3 * l)
        x, h = _ffn(x, n1, wg1, wu1, wd1, l, tm, next_gain=nmix)

        last = l == depth - 1
        seg = functools.partial(_segment, h, w_in, l)
        rq = seg(SEG_RQ, 1, tm_proj, "rotary", tables=rot)
        rk = seg(SEG_RK, 1, tm_proj, "rotary", tables=rot, scale=HEAD ** -0.5)
        rv = seg(SEG_RV, 1, tm_proj, "cast")
        rg = seg(SEG_RG, 1, tm_proj, "silu")
        dq = seg(SEG_DQ, 1, tm_proj, "norm", gain=qn2, scale=HALF ** -0.5)
        dk, k_p, k_s = seg(SEG_DK, 1, tm, "norm_keep", gain=kn2, group_rows=groups,
                           prev_prompt=kps if last else ())
        dv, v_p, v_s = seg(SEG_DV, 1, tm, "keep", group_rows=groups, prev_prompt=vps if last else ())
        gates = seg(SEG_GATES, N_SEG - SEG_GATES, tm_proj, "sigmoid")

        ret_p, st_p = _retention(rq, rk, rv, rg, zero_state, t_ret, nb, seq // t_ret, 0)
        ret_s, st_s = _retention(rq, rk, rv, rg, state_ret[l].astype(F32), dseq, db, 1, mp // dseq)

        dif_p = _attn_prompt(dq, dk, dv, rel_bias, lam_params, subln2, qn2, kn2, l, nb, seq, nh_d, lam_init)
        dif_s = _attn_sample(dq, dk, dv, cache_k, cache_v, rel_bias, lam_params, subln2, qn2, kn2, l, db, dseq,
                             past, mp, nh_d, lam_init)

        merged = _merge((ret_p, ret_s), (dif_p, dif_s), gates, w_ret_c, w_dif_c, l, tm)
        x = _out_proj(x, merged, w_out_c, l, tm)
        x = _ffn(x, n2, wg2, wu2, wd2, l, tm, split_out=groups if l == depth - 1 else None)

        for acc, val in ((kps, k_p), (kss, k_s), (vps, v_p), (vss, v_s), (states_p, st_p), (states_s, st_s)):
            acc.append(val)

    y_p, y_s = x
    kv_p = lambda parts: parts[-1].reshape(depth, nb, seq, nh_d, HEAD)
    kv_s = lambda parts: jnp.stack(parts).reshape(depth, db, dseq, nh_d, HEAD)
    return (y_p.reshape(nb, seq, d), y_s.reshape(db, dseq, d),
            kv_p(kps).astype(cache_diff_k.dtype), kv_p(vps).astype(cache_diff_v.dtype),
            jnp.stack(states_p).astype(state_ret.dtype),
            kv_s(kss).astype(cache_diff_k.dtype), kv_s(vss).astype(cache_diff_v.dtype),
            jnp.stack(states_s).astype(state_ret.dtype))
```

```python
import functools
import math

import numpy as np
import jax
import jax.numpy as jnp
from jax import lax
from jax.experimental import pallas as pl
from jax.experimental.pallas import tpu as pltpu

F32 = jnp.float32
MXU_DTYPE = jnp.bfloat16

CHUNK = 64
HEAD = 128
HALF = HEAD // 2
ROPE_BASE = 10000.0
N_BUCKETS = 32
MAX_DISTANCE = 128
EPS = 1e-6
MASK_VALUE = -1e30
SCORE_BOUND_MARGIN = 1.03
MAX_SHIFT_BRACKET = 100.0

VMEM_LIMIT_CAP = 60 * 1024 * 1024
MIB = 1024 * 1024

ROW_TILES = (512, 256, 128, 64)
PROJ_ROW_TILES = (1536, 1024, 768, 512, 256, 128, 64)
FF_TILES = (512, 256, 128)
ATTN_BLOCKS = (512, 256, 128)
HEADS_PER_STEP = (4, 2, 1)
RET_BLOCKS = (256, 128, 64)
CACHE_TILES = (2048, 1024, 512, 256, 128, 64)


def _pick(n, prefs):
    for p in prefs:
        if n % p == 0:
            return p
    raise ValueError(f"no tile in {prefs} divides {n}")


def _params(semantics, est_bytes):
    limit = int(min(max(est_bytes + 8 * MIB, 32 * MIB), VMEM_LIMIT_CAP))
    return pltpu.CompilerParams(dimension_semantics=semantics, vmem_limit_bytes=limit)


def _rms_scale(x):
    return lax.rsqrt(jnp.mean(x * x, axis=-1, keepdims=True) + EPS)


def _group_specs(tm, width, n_first, row_axis=0):
    first = pl.BlockSpec((tm, width), lambda *g: (jnp.minimum(g[row_axis], n_first - 1), 0))
    second = pl.BlockSpec((tm, width), lambda *g: (jnp.maximum(g[row_axis] - n_first, 0), 0),
                          pipeline_mode=pl.Buffered(1))
    return [first, second]


def _group_tile(refs, n_first, row_axis=0):
    if len(refs) == 1:
        return refs[0][...]
    return jnp.where(pl.program_id(row_axis) < n_first, refs[0][...], refs[1][...])


def _store_group_tile(refs, n_first, value, row_axis=0):
    if len(refs) == 1:
        refs[0][...] = value
        return
    i = pl.program_id(row_axis)

    @pl.when(i < n_first)
    def _():
        refs[0][...] = value

    @pl.when(i >= n_first)
    def _():
        refs[1][...] = value


def _ffn_kernel(*refs, nf, n_in, n_out, n_first, norm_out):
    x_refs = refs[:n_in]
    g_ref, wg_ref, wu_ref, wd_ref = refs[n_in:n_in + 4]
    pos = n_in + 4
    g2_ref = refs[pos] if norm_out else None
    pos += int(norm_out)
    o_refs = refs[pos:pos + n_out]
    pos += n_out
    hn_ref = refs[pos] if norm_out else None
    pos += int(norm_out)
    h_ref, acc_ref = refs[pos:]
    f = pl.program_id(1)

    def chunk(h):
        gate = jnp.dot(h, wg_ref[...], preferred_element_type=F32)
        up = jnp.dot(h, wu_ref[...], preferred_element_type=F32)
        act = (gate * jax.nn.sigmoid(gate) * up).astype(wd_ref.dtype)
        return jnp.dot(act, wd_ref[...], preferred_element_type=F32)

    def normed_input():
        x = _group_tile(x_refs, n_first)
        return (x * _rms_scale(x) * g_ref[...]).astype(h_ref.dtype)

    def finish(acc):
        y = _group_tile(x_refs, n_first) + 0.5 * acc
        _store_group_tile(o_refs, n_first, y)
        if norm_out:
            hn_ref[...] = (y * _rms_scale(y) * g2_ref[...]).astype(hn_ref.dtype)

    if nf == 1:
        finish(chunk(normed_input()))
        return

    @pl.when(f == 0)
    def _():
        h = normed_input()
        h_ref[...] = h
        acc_ref[...] = chunk(h)

    @pl.when(jnp.logical_and(f > 0, f < nf - 1))
    def _():
        acc_ref[...] += chunk(h_ref[...])

    @pl.when(f == nf - 1)
    def _():
        finish(acc_ref[...] + chunk(h_ref[...]))


def _ffn(xs, gain, wg, wu, wd, layer, tm, split_out=None, next_gain=None):
    xs = tuple(xs) if isinstance(xs, (tuple, list)) else (xs,)
    m = sum(x.shape[0] for x in xs)
    d = xs[0].shape[1]
    n_first = (xs[0].shape[0] if len(xs) == 2 else split_out[0] if split_out else m) // tm
    ff = wg.shape[-1]
    tf = _pick(ff, FF_TILES)
    nf = ff // tf
    norm_out = next_gain is not None
    wbytes = jnp.dtype(wg.dtype).itemsize
    row_buffers = (3 if len(xs) == 2 else 2) + (3 if split_out else 2)
    est = (row_buffers * tm * d * 4 + 2 * (3 * d * tf * wbytes + int(norm_out) * tm * d * wbytes)
           + tm * d * (4 + wbytes) + 4 * tm * tf * 4)
    rows = lambda: pl.BlockSpec((tm, d), lambda i, f: (i, 0))
    gain_spec = pl.BlockSpec((None, 1, d), lambda i, f: (layer, 0, 0))
    in_specs = (_group_specs(tm, d, n_first) if len(xs) == 2 else [rows()]) + [
        gain_spec,
        pl.BlockSpec((None, d, tf), lambda i, f: (layer, 0, f)),
        pl.BlockSpec((None, d, tf), lambda i, f: (layer, 0, f)),
        pl.BlockSpec((None, tf, d), lambda i, f: (layer, f, 0)),
    ]
    args = [*xs, gain, wg, wu, wd]
    if split_out:
        out_specs = _group_specs(tm, d, n_first)
        out_shape = [jax.ShapeDtypeStruct((r, d), F32) for r in split_out]
    else:
        out_specs = [rows()]
        out_shape = [jax.ShapeDtypeStruct((m, d), F32)]
    if norm_out:
        in_specs.append(gain_spec)
        args.append(next_gain)
        out_specs.append(rows())
        out_shape.append(jax.ShapeDtypeStruct((m, d), wg.dtype))
    outs = pl.pallas_call(
        functools.partial(_ffn_kernel, nf=nf, n_in=len(xs), n_out=2 if split_out else 1, n_first=n_first,
                          norm_out=norm_out),
        grid=(m // tm, nf),
        in_specs=in_specs,
        out_specs=out_specs,
        out_shape=out_shape,
        scratch_shapes=[pltpu.VMEM((tm, d), wg.dtype), pltpu.VMEM((tm, d), F32)],
        compiler_params=_params(("arbitrary", "arbitrary"), est),
        name="swiglu_half_step",
    )(*args)
    return outs if len(outs) > 1 else outs[0]


SEG_RQ, SEG_RK, SEG_RV, SEG_RG, SEG_DQ, SEG_DK, SEG_DV, SEG_GATES, N_SEG = 0, 1, 2, 3, 4, 5, 6, 7, 11


def _rotate_half_pairs(a, cos2, sin2):
    return a * cos2 + pltpu.roll(a, HALF, 1) * sin2


def _component_rms_norm(a, gain2):
    lo = lax.broadcasted_iota(jnp.int32, a.shape, 1) < HALF
    sq = a * a
    s_all = jnp.sum(sq, axis=-1, keepdims=True)
    s_lo = jnp.sum(jnp.where(lo, sq, 0.0), axis=-1, keepdims=True)
    ms = jnp.where(lo, s_lo, s_all - s_lo) * (1.0 / HALF)
    return a * lax.rsqrt(ms + EPS) * gain2


def _keep_f32(fp_ref, fs_ref, prev_refs, n_first, value):
    i = pl.program_id(1)

    @pl.when(i < n_first)
    def _():
        if prev_refs:
            for l, prev in enumerate(prev_refs):
                fp_ref[l] = prev[...]
            fp_ref[len(prev_refs)] = value
        else:
            fp_ref[...] = value

    @pl.when(i >= n_first)
    def _():
        fs_ref[...] = value


def _segment_kernel(h_ref, w_ref, *refs, kind, scale, nheads, n_first, n_prev):
    *refs, wc_ref = refs

    @pl.when(pl.program_id(1) == 0)
    def _():
        wc_ref[...] = w_ref[...].astype(wc_ref.dtype)

    acc = jnp.dot(h_ref[...], wc_ref[...], preferred_element_type=F32)
    heads = [slice(h * HEAD, (h + 1) * HEAD) for h in range(nheads)]
    if kind == "rotary":
        cos_ref, sin_ref, o_ref = refs
        for sl in heads:
            r = _rotate_half_pairs(acc[:, sl], cos_ref[...], sin_ref[...])
            o_ref[:, sl] = (r if scale == 1.0 else r * scale).astype(o_ref.dtype)
    elif kind == "cast":
        (o_ref,) = refs
        o_ref[...] = acc.astype(o_ref.dtype)
    elif kind == "silu":
        (o_ref,) = refs
        o_ref[...] = (acc * jax.nn.sigmoid(acc)).astype(o_ref.dtype)
    elif kind == "sigmoid":
        (o_ref,) = refs
        o_ref[...] = jax.nn.sigmoid(acc).astype(o_ref.dtype)
    elif kind == "norm":
        gain_ref, o_ref = refs
        for sl in heads:
            o_ref[:, sl] = (_component_rms_norm(acc[:, sl], gain_ref[...]) * scale).astype(o_ref.dtype)
    elif kind == "norm_keep":
        gain_ref, *prev_refs, o_ref, fp_ref, fs_ref = refs
        normed = jnp.concatenate([_component_rms_norm(acc[:, sl], gain_ref[...]) for sl in heads], axis=1)
        _keep_f32(fp_ref, fs_ref, prev_refs, n_first, normed)
        o_ref[...] = normed.astype(o_ref.dtype)
    elif kind == "keep":
        *prev_refs, o_ref, fp_ref, fs_ref = refs
        _keep_f32(fp_ref, fs_ref, prev_refs, n_first, acc)
        o_ref[...] = acc.astype(o_ref.dtype)
    else:
        raise ValueError(kind)
    assert kind not in ("norm_keep", "keep") or len(prev_refs) == n_prev


def _segment(h, w_in, layer, seg0, nseg, tm, kind, *, scale=1.0, tables=(), gain=None, group_rows=None,
             prev_prompt=()):
    m, d = h.shape
    u = d // 2
    nheads = u // HEAD
    n_first = group_rows[0] // tm if group_rows else 0
    n_prev = len(prev_prompt)
    in_specs = [pl.BlockSpec((tm, d), lambda s, i: (i, 0)),
                pl.BlockSpec((None, d, u), lambda s, i: (layer, 0, seg0 + s))]
    args = [h, w_in]
    for tab in tables:
        in_specs.append(pl.BlockSpec((tm, HEAD), lambda s, i: (i, 0)))
        args.append(tab)
    if gain is not None:
        in_specs.append(pl.BlockSpec((None, 1, HEAD), lambda s, i: (layer, 0, 0)))
        args.append(gain)
    out_specs = [pl.BlockSpec((tm, u), lambda s, i: (i, s))]
    out_shape = [jax.ShapeDtypeStruct((m, nseg * u), h.dtype)]
    if group_rows:
        first, second = _group_specs(tm, u, n_first, row_axis=1)
        in_specs += [first] * n_prev
        args += list(prev_prompt)
        if n_prev:
            first = pl.BlockSpec((n_prev + 1, tm, u), lambda s, i: (0, jnp.minimum(i, n_first - 1), 0))
        out_specs += [first, second]
        out_shape += [jax.ShapeDtypeStruct(((n_prev + 1, group_rows[0], u) if n_prev else (group_rows[0], u)), F32),
                      jax.ShapeDtypeStruct((group_rows[1], u), F32)]
    hb = jnp.dtype(h.dtype).itemsize
    est = (2 * (tm * d * hb + d * u * 4 + tm * u * hb + (2 * n_prev + 2) * tm * u * 4 + 2 * tm * HEAD * 4)
           + d * u * hb + 4 * tm * u * 4)
    outs = pl.pallas_call(
        functools.partial(_segment_kernel, kind=kind, scale=scale, nheads=nheads, n_first=n_first, n_prev=n_prev),
        grid=(nseg, m // tm),
        in_specs=in_specs,
        out_specs=out_specs,
        out_shape=out_shape,
        scratch_shapes=[pltpu.VMEM((d, u), h.dtype)],
        compiler_params=_params(("arbitrary", "arbitrary"), est),
        name="input_projection_" + kind,
    )(*args)
    return outs if len(outs) > 1 else outs[0]


def _retention_kernel(q_ref, k_ref, v_ref, g_ref, s0_ref, d_ref, wq_ref, we_ref, dec_ref, o_ref, sout_ref, st_ref,
                      *, nheads, nblk):
    t = pl.program_id(1)

    @pl.when(t == 0)
    def _():
        st_ref[...] = s0_ref[...]

    lanes = [slice(h * HEAD, (h + 1) * HEAD) for h in range(nheads)]
    scores, inter = [], []
    for h, sl in enumerate(lanes):
        q, k, v = q_ref[:, sl], k_ref[:, sl], v_ref[:, sl]
        state = st_ref[h]
        scores.append(lax.dot_general(q, k, (((1,), (1,)), ((), ())), preferred_element_type=F32))
        inter.append(jnp.dot(q, state.astype(q.dtype), preferred_element_type=F32))
        kw = (k.astype(F32) * we_ref[:, sl]).astype(k.dtype)
        kv = lax.dot_general(kw, v, (((0,), (0,)), ((), ())), preferred_element_type=F32)
        st_ref[h] = state * dec_ref[h:h + 1, :] + kv
    for h, sl in enumerate(lanes):
        v = v_ref[:, sl]
        s = scores[h] * d_ref[h]
        o = jnp.dot(s.astype(v.dtype), v, preferred_element_type=F32) + wq_ref[:, sl] * inter[h]
        r = o * _rms_scale(o)
        o_ref[:, sl] = (r * g_ref[:, sl].astype(F32)).astype(o_ref.dtype)

    @pl.when(t == nblk - 1)
    def _():
        sout_ref[...] = st_ref[...]


def _retention_tables(t, nheads):
    log_g = jnp.log(1.0 - 2.0 ** (-5.0 - jnp.arange(nheads, dtype=F32)))
    idx = jnp.arange(t, dtype=F32)
    dist = jnp.abs(idx[:, None] - idx[None, :])
    ci = np.arange(t) // CHUNK
    visible = jnp.asarray(ci[None, :] <= ci[:, None])
    dmat = jnp.where(visible[None], jnp.exp(log_g[:, None, None] * dist[None]), 0.0)
    wq = jnp.exp(log_g[None, :] * (idx + 1.0)[:, None])
    we = jnp.exp(log_g[None, :] * (t - 1.0 - idx)[:, None])
    dec = jnp.exp(log_g * t)
    expand = lambda a: jnp.repeat(a, HEAD, axis=1)
    return dmat, expand(wq), expand(we), jnp.broadcast_to(dec[:, None], (nheads, HEAD))


def _retention(q, k, v, g, s0, t, nbatch, nblk, row_block0):
    nheads = s0.shape[1]
    u = nheads * HEAD
    dmat, wq, we, dec = _retention_tables(t, nheads)
    rows = pl.BlockSpec((t, u), lambda b, i: (row_block0 + b * nblk + i, 0))
    whole = lambda a: pl.BlockSpec(a.shape, lambda b, i: (0,) * a.ndim)
    state_spec = pl.BlockSpec((None, nheads, HEAD, HEAD), lambda b, i: (b, 0, 0, 0))
    pbytes = jnp.dtype(q.dtype).itemsize
    est = (2 * (5 * t * u * pbytes + 2 * nheads * HEAD * HEAD * 4 + nheads * t * t * 4 + 2 * t * u * 4)
           + nheads * HEAD * HEAD * 4 + 6 * t * max(t, HEAD) * 4)
    return pl.pallas_call(
        functools.partial(_retention_kernel, nheads=nheads, nblk=nblk),
        grid=(nbatch, nblk),
        in_specs=[rows, rows, rows, rows, state_spec, whole(dmat), whole(wq), whole(we), whole(dec)],
        out_specs=[
            pl.BlockSpec((t, u), lambda b, i: (b * nblk + i, 0)),
            state_spec,
        ],
        out_shape=[
            jax.ShapeDtypeStruct((nbatch * nblk * t, u), q.dtype),
            jax.ShapeDtypeStruct((nbatch, nheads, HEAD, HEAD), F32),
        ],
        scratch_shapes=[pltpu.VMEM((nheads, HEAD, HEAD), F32)],
        compiler_params=_params(("parallel", "arbitrary"), est),
        name="retention",
    )(q, k, v, g, s0, dmat, wq, we, dec)


def _bucket_thresholds():
    nb = N_BUCKETS // 2
    me = nb // 2
    out = []
    for k in range(1, nb - me):
        n = me
        while n ** (nb - me) * me ** k < me ** (nb - me) * MAX_DISTANCE ** k:
            n += 1
        out.append(n)
    return out


def _t5_bucket_np(rel):
    nb = N_BUCKETS // 2
    me = nb // 2
    n = np.abs(rel)
    large = np.full(rel.shape, me, np.int64)
    for thr in _bucket_thresholds():
        large += (n >= thr)
    large = np.minimum(large, nb - 1)
    return (np.where(rel > 0, nb, 0) + np.where(n < me, n, large)).astype(np.int32)


def _bias_kernel(rb_ref, idx_ref, mask_ref, o_ref):
    h = pl.program_id(0)
    idx = idx_ref[...]
    out = mask_ref[...]
    for b in range(N_BUCKETS):
        out = out + jnp.where(idx == b, rb_ref[b, h], 0.0)
    o_ref[...] = out


def _bias_table(rel_bias, qpos, kpos):
    nheads = rel_bias.shape[1]
    rel = kpos[None, :] - qpos[:, None]
    idx = jnp.asarray(_t5_bucket_np(rel))
    mask = jnp.asarray(np.where((kpos[None, :] // CHUNK) <= (qpos[:, None] // CHUNK), 0.0, MASK_VALUE)
                       .astype(np.float32))
    nq, nk = rel.shape
    return pl.pallas_call(
        _bias_kernel,
        grid=(nheads,),
        in_specs=[
            pl.BlockSpec(memory_space=pltpu.SMEM),
            pl.BlockSpec((nq, nk), lambda h: (0, 0)),
            pl.BlockSpec((nq, nk), lambda h: (0, 0)),
        ],
        out_specs=pl.BlockSpec((None, nq, nk), lambda h: (h, 0, 0)),
        out_shape=jax.ShapeDtypeStruct((nheads, nq, nk), F32),
        compiler_params=_params(("arbitrary",), 6 * nq * nk * 4),
        name="relative_bias_table",
    )(rel_bias, idx, mask)


def _stack_components(q):
    lo = lax.broadcasted_iota(jnp.int32, q.shape, 1) < HALF
    zero = jnp.zeros_like(q)
    return jnp.concatenate([jnp.where(lo, q, zero), jnp.where(lo, zero, q)], axis=0)


def _biased_scores(qs, k, bias):
    s = lax.dot_general(qs, k, (((1,), (1,)), ((), ())), preferred_element_type=F32)
    if bias.ndim == 2:
        t = bias.shape[0]
        return jnp.concatenate([s[:t] + bias, s[t:] + bias], axis=0), 0.0
    return s, bias


def _softmax_step(qs, k, v, bias, stats, g):
    m_ref, l_ref, acc_ref = stats
    s, c = _biased_scores(qs, k, bias)
    m_prev = m_ref[g]
    m_new = jnp.maximum(m_prev, jnp.max(s, axis=-1, keepdims=True) + c)
    alpha = jnp.exp(m_prev - m_new)
    p = jnp.exp(s - (m_new - c))
    l_ref[g] = alpha * l_ref[g] + jnp.sum(p, axis=-1, keepdims=True)
    acc_ref[g] = alpha * acc_ref[g] + jnp.dot(p.astype(v.dtype), v, preferred_element_type=F32)
    m_ref[g] = m_new


def _softmax_init(stats):
    m_ref, l_ref, acc_ref = stats
    m_ref[...] = jnp.full_like(m_ref, MASK_VALUE)
    l_ref[...] = jnp.zeros_like(l_ref)
    acc_ref[...] = jnp.zeros_like(acc_ref)


def _softmax_scratch(groups, t):
    return [pltpu.VMEM((groups, 2 * t, 1), F32), pltpu.VMEM((groups, 2 * t, 1), F32),
            pltpu.VMEM((groups, 2 * t, HEAD), F32)]


def _lambda_value(lam_ref, lam_init):
    a = lam_ref[...]
    e1 = jnp.exp(jnp.sum(a[0:1] * a[1:2], axis=-1, keepdims=True))
    e2 = jnp.exp(jnp.sum(a[2:3] * a[3:4], axis=-1, keepdims=True))
    return e1 - e2 + lam_init


def _diff_finish(t, lam, lam_init, subln, stats, g):
    _, l_ref, acc_ref = stats
    acc = acc_ref[g]
    l = l_ref[g]
    o = acc[:t] / l[:t] - lam * (acc[t:] / l[t:])
    return o * _rms_scale(o) * subln * (1.0 - lam_init)


def _shift_brackets(rb_ref, qg_ref, kg_ref, qs, own_ks, head0, m_ref):
    gain_bound = (SCORE_BOUND_MARGIN * HALF ** 0.5 * jnp.max(jnp.abs(qg_ref[...]), axis=-1, keepdims=True)
                  * jnp.max(jnp.abs(kg_ref[...]), axis=-1, keepdims=True))
    widest = jnp.zeros((1, 1), F32)
    for g, (q, own_k) in enumerate(zip(qs, own_ks)):
        head = head0 + g
        bias_max = rb_ref[0, head]
        for b in range(1, N_BUCKETS):
            bias_max = jnp.maximum(bias_max, rb_ref[b, head])
        upper = gain_bound + bias_max
        own_k = own_k.astype(F32)
        own = jnp.sum(q.astype(F32) * jnp.concatenate([own_k, own_k], axis=0), axis=-1, keepdims=True)
        lower = own + rb_ref[0, head]
        m_ref[g] = 0.5 * (upper + lower)
        widest = jnp.maximum(widest, jnp.max(upper - lower, axis=0, keepdims=True))
    return widest[0, 0]


def _attn_prompt_kernel(rb_ref, q_ref, k_ref, v_ref, bias_ref, lam_ref, sub_ref, qg_ref, kg_ref, o_ref, *stats,
                        t, groups, lam_init, far_bucket):
    hp = pl.program_id(1)
    qi = pl.program_id(2)
    m_ref, l_ref, acc_ref = stats
    lanes = [slice(g * HEAD, (g + 1) * HEAD) for g in range(groups)]
    qs = [_stack_components(q_ref[:, sl]) for sl in lanes]
    far_bias = [rb_ref[far_bucket, hp * groups + g] for g in range(groups)]
    _softmax_init(stats)

    def sweep(step):
        def run(j, bias_of):
            step(pl.ds(pl.multiple_of(j * t, t), t), [bias_of(g) for g in range(groups)])

        def far_step(j, carry):
            run(j, lambda g: far_bias[g])
            return carry

        lax.fori_loop(0, jnp.maximum(qi - 1, 0), far_step, 0)

        @pl.when(qi > 0)
        def _():
            run(qi - 1, lambda g: bias_ref[g, :, :t])

        run(qi, lambda g: bias_ref[g, :, t:])

    def max_step(rows, biases):
        scores = [_biased_scores(qs[g], k_ref[rows, lanes[g]], biases[g]) for g in range(groups)]
        for g, (s, c) in enumerate(scores):
            m_ref[g] = jnp.maximum(m_ref[g], jnp.max(s, axis=-1, keepdims=True) + c)

    def acc_step(rows, biases):
        scores = [_biased_scores(qs[g], k_ref[rows, lanes[g]], biases[g]) for g in range(groups)]
        probs = [jnp.exp(s - (m_ref[g] - c)) for g, (s, c) in enumerate(scores)]
        for g, p in enumerate(probs):
            l_ref[g] += jnp.sum(p, axis=-1, keepdims=True)
            v = v_ref[rows, lanes[g]]
            acc_ref[g] += jnp.dot(p.astype(v.dtype), v, preferred_element_type=F32)

    own_rows = pl.ds(pl.multiple_of(qi * t, t), t)
    widest = _shift_brackets(rb_ref, qg_ref, kg_ref, qs, [k_ref[own_rows, sl] for sl in lanes],
                             hp * groups, m_ref)

    @pl.when(widest > MAX_SHIFT_BRACKET)
    def _():
        m_ref[...] = jnp.full_like(m_ref, MASK_VALUE)
        sweep(max_step)

    sweep(acc_step)

    lam = _lambda_value(lam_ref, lam_init)
    for g, sl in enumerate(lanes):
        o_ref[:, sl] = _diff_finish(t, lam, lam_init, sub_ref[...], stats, g).astype(o_ref.dtype)


def _attn_prompt(q, k, v, rel_bias, lam_params, subln2, qn2, kn2, layer, nbatch, seq, nheads, lam_init):
    u = nheads * HEAD
    t = _pick(seq, ATTN_BLOCKS)
    groups = _pick(nheads, HEADS_PER_STEP)
    assert t % CHUNK == 0 and t + 1 >= _bucket_thresholds()[-1]
    nq = seq // t
    w = groups * HEAD
    r = np.arange(t)
    bias = _bias_table(rel_bias, r + t, np.arange(2 * t))
    pbytes = jnp.dtype(q.dtype).itemsize
    est = (2 * (2 * t * w * pbytes + 2 * seq * w * pbytes + groups * 2 * t * t * 4)
           + groups * (2 * t * (HEAD + 2 * 128) * 4 + 6 * 2 * t * t * 4))
    return pl.pallas_call(
        functools.partial(_attn_prompt_kernel, t=t, groups=groups, lam_init=lam_init,
                          far_bucket=N_BUCKETS // 2 - 1),
        grid=(nbatch, nheads // groups, nq),
        in_specs=[
            pl.BlockSpec(memory_space=pltpu.SMEM),
            pl.BlockSpec((t, w), lambda b, h, i: (b * nq + i, h)),
            pl.BlockSpec((seq, w), lambda b, h, i: (b, h)),
            pl.BlockSpec((seq, w), lambda b, h, i: (b, h)),
            pl.BlockSpec((groups, t, 2 * t), lambda b, h, i: (h, 0, 0)),
            pl.BlockSpec((None, 4, HALF), lambda b, h, i: (layer, 0, 0)),
            pl.BlockSpec((None, 1, HEAD), lambda b, h, i: (layer, 0, 0)),
            pl.BlockSpec((None, 1, HEAD), lambda b, h, i: (layer, 0, 0)),
            pl.BlockSpec((None, 1, HEAD), lambda b, h, i: (layer, 0, 0)),
        ],
        out_specs=pl.BlockSpec((t, w), lambda b, h, i: (b * nq + i, h)),
        out_shape=jax.ShapeDtypeStruct((nbatch * seq, u), q.dtype),
        scratch_shapes=_softmax_scratch(groups, t),
        compiler_params=_params(("parallel", "parallel", "arbitrary"), est),
        name="diff_attention_prompt",
    )(rel_bias, q, k, v, bias, lam_params, subln2, qn2, kn2)


def _attn_sample_kernel(rb_ref, q_ref, kn_ref, vn_ref, kc_ref, vc_ref, bc_ref, bn_ref, lam_ref, sub_ref,
                        qg_ref, kg_ref, o_ref, *stats, t, tk, ncache, nheads, lam_init):
    *stats, wide_ref = stats
    m_ref, l_ref, acc_ref = stats
    j = pl.program_id(1)
    lanes = [slice(h * HEAD, (h + 1) * HEAD) for h in range(nheads)]
    heads = range(nheads)
    qs = [_stack_components(q_ref[:, sl]) for sl in lanes]

    @pl.when(j == 0)
    def _():
        _softmax_init(stats)
        widest = _shift_brackets(rb_ref, qg_ref, kg_ref, qs, [kn_ref[:, sl] for sl in lanes], 0, m_ref)
        wide_ref[0] = (widest > MAX_SHIFT_BRACKET).astype(jnp.int32)

        @pl.when(widest > MAX_SHIFT_BRACKET)
        def _():
            m_ref[...] = jnp.full_like(m_ref, MASK_VALUE)

    def fixed_shift_steps(ks, vs, biases):
        scores = [_biased_scores(qs[h], ks[h], biases[h])[0] for h in heads]
        probs = [jnp.exp(scores[h] - m_ref[h]) for h in heads]
        for h in heads:
            l_ref[h] += jnp.sum(probs[h], axis=-1, keepdims=True)
            acc_ref[h] += jnp.dot(probs[h].astype(vs[h].dtype), vs[h], preferred_element_type=F32)

    def online_steps(ks, vs, biases):
        for h in heads:
            _softmax_step(qs[h], ks[h], vs[h], biases[h], stats, h)

    def both(ks, vs, biases):
        @pl.when(wide_ref[0] == 0)
        def _():
            fixed_shift_steps(ks(), vs(), biases())

        @pl.when(wide_ref[0] != 0)
        def _():
            online_steps(ks(), vs(), biases())

    head_rows = lambda ref: [ref[pl.ds(h, tk, stride=nheads), :].astype(qs[0].dtype) for h in heads]
    both(lambda: head_rows(kc_ref), lambda: head_rows(vc_ref), lambda: [bc_ref[h] for h in heads])

    @pl.when(j == ncache - 1)
    def _():
        both(lambda: [kn_ref[:, sl] for sl in lanes], lambda: [vn_ref[:, sl] for sl in lanes],
             lambda: [bn_ref[h] for h in heads])
        lam = _lambda_value(lam_ref, lam_init)
        for h, sl in enumerate(lanes):
            o_ref[:, sl] = _diff_finish(t, lam, lam_init, sub_ref[...], stats, h).astype(o_ref.dtype)


def _attn_sample(q, k, v, cache_k, cache_v, rel_bias, lam_params, subln2, qn2, kn2, layer, nbatch, t, past,
                 row0, nheads, lam_init):
    u = nheads * HEAD
    assert row0 % t == 0
    rb0 = row0 // t
    tk = _pick(past, CACHE_TILES)
    qpos = past + np.arange(t)
    ncache = past // tk
    bias_c = _bias_table(rel_bias, qpos, np.arange(past))
    bias_c = bias_c.reshape(nheads, t, ncache, tk).transpose(2, 0, 1, 3)
    bias_n = _bias_table(rel_bias, qpos, qpos)
    pbytes = jnp.dtype(q.dtype).itemsize
    est = (2 * (4 * t * u * pbytes + 2 * tk * nheads * HEAD * 4 + nheads * t * (tk + t) * 4)
           + nheads * (2 * t * (HEAD + 2 * 128) * 4 + 6 * 2 * t * tk * 4))
    rows = pl.BlockSpec((t, u), lambda b, j: (rb0 + b, 0))
    cache = pl.BlockSpec((None, tk * nheads, HEAD), lambda b, j: (layer, b * ncache + j, 0))
    return pl.pallas_call(
        functools.partial(_attn_sample_kernel, t=t, tk=tk, ncache=ncache, nheads=nheads, lam_init=lam_init),
        grid=(nbatch, ncache),
        in_specs=[
            pl.BlockSpec(memory_space=pltpu.SMEM),
            rows, rows, rows, cache, cache,
            pl.BlockSpec((None, nheads, t, tk), lambda b, j: (j, 0, 0, 0)),
            pl.BlockSpec((nheads, t, t), lambda b, j: (0, 0, 0)),
            pl.BlockSpec((None, 4, HALF), lambda b, j: (layer, 0, 0)),
            pl.BlockSpec((None, 1, HEAD), lambda b, j: (layer, 0, 0)),
            pl.BlockSpec((None, 1, HEAD), lambda b, j: (layer, 0, 0)),
            pl.BlockSpec((None, 1, HEAD), lambda b, j: (layer, 0, 0)),
        ],
        out_specs=pl.BlockSpec((t, u), lambda b, j: (b, 0)),
        out_shape=jax.ShapeDtypeStruct((nbatch * t, u), q.dtype),
        scratch_shapes=_softmax_scratch(nheads, t) + [pltpu.SMEM((1,), jnp.int32)],
        compiler_params=_params(("parallel", "arbitrary"), est),
        name="diff_attention_sample",
    )(rel_bias, q, k, v, cache_k, cache_v, bias_c, bias_n, lam_params, subln2, qn2, kn2)


def _merge_kernel(ap_ref, as_ref, bp_ref, bs_ref, ga0_ref, ga1_ref, gb0_ref, gb1_ref, wa_ref, wb_ref, o_ref,
                  *, u, n_first):
    a = _group_tile((ap_ref, as_ref), n_first)
    b = _group_tile((bp_ref, bs_ref), n_first)
    ya = jnp.dot(a, wa_ref[...], preferred_element_type=F32)
    yb = jnp.dot(b, wb_ref[...], preferred_element_type=F32)
    for c, (ga, gb) in enumerate(((ga0_ref, gb0_ref), (ga1_ref, gb1_ref))):
        sl = slice(c * u, (c + 1) * u)
        o_ref[:, sl] = (ga[...].astype(F32) * ya[:, sl] + gb[...].astype(F32) * yb[:, sl]).astype(o_ref.dtype)


def _merge(ret_outs, dif_outs, gates, w_ret_up, w_dif_up, layer, tm):
    m = gates.shape[0]
    u = ret_outs[0].shape[1]
    d = 2 * u
    n_first = ret_outs[0].shape[0] // tm
    gate = lambda c: pl.BlockSpec((tm, u), lambda i: (i, c))
    rows = _group_specs(tm, u, n_first)
    wspec = pl.BlockSpec((None, u, d), lambda i: (layer, 0, 0))
    pbytes = jnp.dtype(gates.dtype).itemsize
    est = 2 * (8 * tm * u * pbytes + 2 * u * d * pbytes + tm * d * pbytes) + 3 * tm * d * 4
    return pl.pallas_call(
        functools.partial(_merge_kernel, u=u, n_first=n_first),
        grid=(m // tm,),
        in_specs=rows + rows + [gate(0), gate(1), gate(2), gate(3), wspec, wspec],
        out_specs=pl.BlockSpec((tm, d), lambda i: (i, 0)),
        out_shape=jax.ShapeDtypeStruct((m, d), gates.dtype),
        compiler_params=_params(("arbitrary",), est),
        name="gated_merge",
    )(*ret_outs, *dif_outs, gates, gates, gates, gates, w_ret_up, w_dif_up)


def _out_proj_kernel(x_ref, a_ref, w_ref, o_ref):
    o_ref[...] = x_ref[...] + jnp.dot(a_ref[...], w_ref[...], preferred_element_type=F32)


def _out_proj(x, merged, w_out, layer, tm):
    m, d = x.shape
    wbytes = jnp.dtype(w_out.dtype).itemsize
    est = 2 * (2 * tm * d * 4 + tm * d * wbytes + d * d * wbytes) + tm * d * 4
    return pl.pallas_call(
        _out_proj_kernel,
        grid=(m // tm,),
        in_specs=[
            pl.BlockSpec((tm, d), lambda i: (i, 0)),
            pl.BlockSpec((tm, d), lambda i: (i, 0)),
            pl.BlockSpec((None, d, d), lambda i: (layer, 0, 0)),
        ],
        out_specs=pl.BlockSpec((tm, d), lambda i: (i, 0)),
        out_shape=jax.ShapeDtypeStruct((m, d), F32),
        compiler_params=_params(("parallel",), est),
        name="output_projection",
    )(x, merged, w_out)


def _rotary_tables(pos):
    inv = ROPE_BASE ** (-jnp.arange(HALF, dtype=F32) / HALF)
    ang = pos.astype(F32)[:, None] * inv[None, :]
    cos, sin = jnp.cos(ang), jnp.sin(ang)
    return jnp.concatenate([cos, cos], axis=-1), jnp.concatenate([-sin, sin], axis=-1)


def kernel(x_prompt, x_sample, cache_diff_k, cache_diff_v, state_ret, ffn1_norm, ffn1_gate, ffn1_up, ffn1_down, mix_norm, w_in, q_norm, k_norm, lambda_q1, lambda_k1, lambda_q2, lambda_k2, subln, w_ret_up, w_dif_up, w_out, ffn2_norm, ffn2_gate, ffn2_up, ffn2_down, rel_bias):
    nb, seq, d = x_prompt.shape
    db, dseq, _ = x_sample.shape
    depth, _, past, nh_d, _ = cache_diff_k.shape
    nh_r = state_ret.shape[2]
    u = d // 2
    assert nh_r * HEAD == u and nh_d * HEAD == u and dseq == CHUNK and seq % CHUNK == 0
    assert w_in.shape[-1] == N_SEG * u
    mp, ms = nb * seq, db * dseq

    cast = lambda w: w.astype(MXU_DTYPE)
    wg1, wu1, wd1 = cast(ffn1_gate), cast(ffn1_up), cast(ffn1_down)
    wg2, wu2, wd2 = cast(ffn2_gate), cast(ffn2_up), cast(ffn2_down)
    w_ret_c, w_dif_c, w_out_c = cast(w_ret_up), cast(w_dif_up), cast(w_out)
    row3 = lambda g: g.reshape(depth, 1, g.shape[-1])
    n1, nmix, n2 = row3(ffn1_norm), row3(mix_norm), row3(ffn2_norm)
    qn2 = row3(jnp.concatenate([q_norm, q_norm], axis=-1))
    kn2 = row3(jnp.concatenate([k_norm, k_norm], axis=-1))
    subln2 = row3(subln)
    lam_params = jnp.stack([lambda_q1, lambda_k1, lambda_q2, lambda_k2], axis=1)
    cache_k = cache_diff_k.reshape(depth, db * past * nh_d, HEAD)
    cache_v = cache_diff_v.reshape(depth, db * past * nh_d, HEAD)

    pos = jnp.concatenate([jnp.tile(jnp.arange(seq, dtype=jnp.int32), nb),
                           past + jnp.tile(jnp.arange(dseq, dtype=jnp.int32), db)])
    rot = _rotary_tables(pos)

    tm = _pick(math.gcd(mp, ms), ROW_TILES)
    tm_proj = _pick(mp + ms, PROJ_ROW_TILES)
    groups = (mp, ms)
    zero_state = jnp.zeros((nb, nh_r, HEAD, HEAD), F32)
    t_ret = _pick(seq, RET_BLOCKS)

    x = (x_prompt.reshape(mp, d), x_sample.reshape(ms, d))
    kps, kss, vps, vss, states_p, states_s = [], [], [], [], [], []
    for l in range(depth):
        lam_init = 0.8 - 0.6 * math.exp(-0.3 * l)
        x, h = _ffn(x, n1, wg1, wu1, wd1, l, tm, next_gain=nmix)

        last = l == depth - 1
        seg = functools.partial(_segment, h, w_in, l)
        rq = seg(SEG_RQ, 1, tm_proj, "rotary", tables=rot)
        rk = seg(SEG_RK, 1, tm_proj, "rotary", tables=rot, scale=HEAD ** -0.5)
        rv = seg(SEG_RV, 1, tm_proj, "cast")
        rg = seg(SEG_RG, 1, tm_proj, "silu")
        dq = seg(SEG_DQ, 1, tm_proj, "norm", gain=qn2, scale=HALF ** -0.5)
        dk, k_p, k_s = seg(SEG_DK, 1, tm, "norm_keep", gain=kn2, group_rows=groups,
                           prev_prompt=kps if last else ())
        dv, v_p, v_s = seg(SEG_DV, 1, tm, "keep", group_rows=groups, prev_prompt=vps if last else ())
        gates = seg(SEG_GATES, N_SEG - SEG_GATES, tm_proj, "sigmoid")

        ret_p, st_p = _retention(rq, rk, rv, rg, zero_state, t_ret, nb, seq // t_ret, 0)
        ret_s, st_s = _retention(rq, rk, rv, rg, state_ret[l].astype(F32), dseq, db, 1, mp // dseq)

        dif_p = _attn_prompt(dq, dk, dv, rel_bias, lam_params, subln2, qn2, kn2, l, nb, seq, nh_d, lam_init)
        dif_s = _attn_sample(dq, dk, dv, cache_k, cache_v, rel_bias, lam_params, subln2, qn2, kn2, l, db, dseq,
                             past, mp, nh_d, lam_init)

        merged = _merge((ret_p, ret_s), (dif_p, dif_s), gates, w_ret_c, w_dif_c, l, tm)
        x = _out_proj(x, merged, w_out_c, l, tm)
        x = _ffn(x, n2, wg2, wu2, wd2, l, tm, split_out=groups if l == depth - 1 else None)

        for acc, val in ((kps, k_p), (kss, k_s), (vps, v_p), (vss, v_s), (states_p, st_p), (states_s, st_s)):
            acc.append(val)

    y_p, y_s = x
    kv_p = lambda parts: parts[-1].reshape(depth, nb, seq, nh_d, HEAD)
    kv_s = lambda parts: jnp.stack(parts).reshape(depth, db, dseq, nh_d, HEAD)
    return (y_p.reshape(nb, seq, d), y_s.reshape(db, dseq, d),
            kv_p(kps).astype(cache_diff_k.dtype), kv_p(vps).astype(cache_diff_v.dtype),
            jnp.stack(states_p).astype(state_ret.dtype),
            kv_s(kss).astype(cache_diff_k.dtype), kv_s(vss).astype(cache_diff_v.dtype),
            jnp.stack(states_s).astype(state_ret.dtype))
```

```python
import functools
import math

import numpy as np
import jax
import jax.numpy as jnp
from jax import lax
from jax.experimental import pallas as pl
from jax.experimental.pallas import tpu as pltpu

F32 = jnp.float32
MXU_DTYPE = jnp.bfloat16

CHUNK = 64
HEAD = 128
HALF = HEAD // 2
ROPE_BASE = 10000.0
N_BUCKETS = 32
MAX_DISTANCE = 128
EPS = 1e-6
MASK_VALUE = -1e30
SCORE_BOUND_MARGIN = 1.03
MAX_SHIFT_BRACKET = 100.0

VMEM_LIMIT_CAP = 60 * 1024 * 1024
MIB = 1024 * 1024

ROW_TILES = (512, 256, 128, 64)
PROJ_ROW_TILES = (1536, 1024, 768, 512, 256, 128, 64)
FF_TILES = (512, 256, 128)
ATTN_BLOCKS = (512, 256, 128)
HEADS_PER_STEP = (4, 2, 1)
QUERY_SUBBLOCKS = 2
RET_BLOCKS = (256, 128, 64)
CACHE_TILES = (2048, 1024, 512, 256, 128, 64)


def _pick(n, prefs):
    for p in prefs:
        if n % p == 0:
            return p
    raise ValueError(f"no tile in {prefs} divides {n}")


def _params(semantics, est_bytes):
    limit = int(min(max(est_bytes + 8 * MIB, 32 * MIB), VMEM_LIMIT_CAP))
    return pltpu.CompilerParams(dimension_semantics=semantics, vmem_limit_bytes=limit)


def _rms_scale(x):
    return lax.rsqrt(jnp.mean(x * x, axis=-1, keepdims=True) + EPS)


def _group_specs(tm, width, n_first, row_axis=0):
    first = pl.BlockSpec((tm, width), lambda *g: (jnp.minimum(g[row_axis], n_first - 1), 0))
    second = pl.BlockSpec((tm, width), lambda *g: (jnp.maximum(g[row_axis] - n_first, 0), 0),
                          pipeline_mode=pl.Buffered(1))
    return [first, second]


def _group_tile(refs, n_first, row_axis=0):
    if len(refs) == 1:
        return refs[0][...]
    return jnp.where(pl.program_id(row_axis) < n_first, refs[0][...], refs[1][...])


def _store_group_tile(refs, n_first, value, row_axis=0):
    if len(refs) == 1:
        refs[0][...] = value
        return
    i = pl.program_id(row_axis)

    @pl.when(i < n_first)
    def _():
        refs[0][...] = value

    @pl.when(i >= n_first)
    def _():
        refs[1][...] = value


def _ffn_kernel(*refs, nf, n_in, n_out, n_first, norm_out):
    x_refs = refs[:n_in]
    g_ref, wg_ref, wu_ref, wd_ref = refs[n_in:n_in + 4]
    pos = n_in + 4
    g2_ref = refs[pos] if norm_out else None
    pos += int(norm_out)
    o_refs = refs[pos:pos + n_out]
    pos += n_out
    hn_ref = refs[pos] if norm_out else None
    pos += int(norm_out)
    h_ref, acc_ref = refs[pos:]
    f = pl.program_id(1)

    def chunk(h):
        gate = jnp.dot(h, wg_ref[...], preferred_element_type=F32)
        up = jnp.dot(h, wu_ref[...], preferred_element_type=F32)
        act = (gate * jax.nn.sigmoid(gate) * up).astype(wd_ref.dtype)
        return jnp.dot(act, wd_ref[...], preferred_element_type=F32)

    def normed_input():
        x = _group_tile(x_refs, n_first)
        return (x * _rms_scale(x) * g_ref[...]).astype(h_ref.dtype)

    def finish(acc):
        y = _group_tile(x_refs, n_first) + 0.5 * acc
        _store_group_tile(o_refs, n_first, y)
        if norm_out:
            hn_ref[...] = (y * _rms_scale(y) * g2_ref[...]).astype(hn_ref.dtype)

    if nf == 1:
        finish(chunk(normed_input()))
        return

    @pl.when(f == 0)
    def _():
        h = normed_input()
        h_ref[...] = h
        acc_ref[...] = chunk(h)

    @pl.when(jnp.logical_and(f > 0, f < nf - 1))
    def _():
        acc_ref[...] += chunk(h_ref[...])

    @pl.when(f == nf - 1)
    def _():
        finish(acc_ref[...] + chunk(h_ref[...]))


def _ffn(xs, gain, wg, wu, wd, layer, tm, split_out=None, next_gain=None):
    xs = tuple(xs) if isinstance(xs, (tuple, list)) else (xs,)
    m = sum(x.shape[0] for x in xs)
    d = xs[0].shape[1]
    n_first = (xs[0].shape[0] if len(xs) == 2 else split_out[0] if split_out else m) // tm
    ff = wg.shape[-1]
    tf = _pick(ff, FF_TILES)
    nf = ff // tf
    norm_out = next_gain is not None
    wbytes = jnp.dtype(wg.dtype).itemsize
    row_buffers = (3 if len(xs) == 2 else 2) + (3 if split_out else 2)
    est = (row_buffers * tm * d * 4 + 2 * (3 * d * tf * wbytes + int(norm_out) * tm * d * wbytes)
           + tm * d * (4 + wbytes) + 4 * tm * tf * 4)
    rows = lambda: pl.BlockSpec((tm, d), lambda i, f: (i, 0))
    gain_spec = pl.BlockSpec((None, 1, d), lambda i, f: (layer, 0, 0))
    in_specs = (_group_specs(tm, d, n_first) if len(xs) == 2 else [rows()]) + [
        gain_spec,
        pl.BlockSpec((None, d, tf), lambda i, f: (layer, 0, f)),
        pl.BlockSpec((None, d, tf), lambda i, f: (layer, 0, f)),
        pl.BlockSpec((None, tf, d), lambda i, f: (layer, f, 0)),
    ]
    args = [*xs, gain, wg, wu, wd]
    if split_out:
        out_specs = _group_specs(tm, d, n_first)
        out_shape = [jax.ShapeDtypeStruct((r, d), F32) for r in split_out]
    else:
        out_specs = [rows()]
        out_shape = [jax.ShapeDtypeStruct((m, d), F32)]
    if norm_out:
        in_specs.append(gain_spec)
        args.append(next_gain)
        out_specs.append(rows())
        out_shape.append(jax.ShapeDtypeStruct((m, d), wg.dtype))
    outs = pl.pallas_call(
        functools.partial(_ffn_kernel, nf=nf, n_in=len(xs), n_out=2 if split_out else 1, n_first=n_first,
                          norm_out=norm_out),
        grid=(m // tm, nf),
        in_specs=in_specs,
        out_specs=out_specs,
        out_shape=out_shape,
        scratch_shapes=[pltpu.VMEM((tm, d), wg.dtype), pltpu.VMEM((tm, d), F32)],
        compiler_params=_params(("arbitrary", "arbitrary"), est),
        name="swiglu_half_step",
    )(*args)
    return outs if len(outs) > 1 else outs[0]


SEG_RQ, SEG_RK, SEG_RV, SEG_RG, SEG_DQ, SEG_DK, SEG_DV, SEG_GATES, N_SEG = 0, 1, 2, 3, 4, 5, 6, 7, 11


def _rotate_half_pairs(a, cos2, sin2):
    return a * cos2 + pltpu.roll(a, HALF, 1) * sin2


def _component_rms_norm(a, gain2):
    lo = lax.broadcasted_iota(jnp.int32, a.shape, 1) < HALF
    sq = a * a
    s_all = jnp.sum(sq, axis=-1, keepdims=True)
    s_lo = jnp.sum(jnp.where(lo, sq, 0.0), axis=-1, keepdims=True)
    ms = jnp.where(lo, s_lo, s_all - s_lo) * (1.0 / HALF)
    return a * lax.rsqrt(ms + EPS) * gain2


def _keep_f32(fp_ref, fs_ref, prev_refs, n_first, value):
    i = pl.program_id(1)

    @pl.when(i < n_first)
    def _():
        if prev_refs:
            for l, prev in enumerate(prev_refs):
                fp_ref[l] = prev[...]
            fp_ref[len(prev_refs)] = value
        else:
            fp_ref[...] = value

    @pl.when(i >= n_first)
    def _():
        fs_ref[...] = value


def _segment_kernel(h_ref, w_ref, *refs, kind, scale, nheads, n_first, n_prev):
    *refs, wc_ref = refs

    @pl.when(pl.program_id(1) == 0)
    def _():
        wc_ref[...] = w_ref[...].astype(wc_ref.dtype)

    acc = jnp.dot(h_ref[...], wc_ref[...], preferred_element_type=F32)
    heads = [slice(h * HEAD, (h + 1) * HEAD) for h in range(nheads)]
    if kind == "rotary":
        cos_ref, sin_ref, o_ref = refs
        for sl in heads:
            r = _rotate_half_pairs(acc[:, sl], cos_ref[...], sin_ref[...])
            o_ref[:, sl] = (r if scale == 1.0 else r * scale).astype(o_ref.dtype)
    elif kind == "cast":
        (o_ref,) = refs
        o_ref[...] = acc.astype(o_ref.dtype)
    elif kind == "silu":
        (o_ref,) = refs
        o_ref[...] = (acc * jax.nn.sigmoid(acc)).astype(o_ref.dtype)
    elif kind == "sigmoid":
        (o_ref,) = refs
        o_ref[...] = jax.nn.sigmoid(acc).astype(o_ref.dtype)
    elif kind == "norm":
        gain_ref, o_ref = refs
        for sl in heads:
            o_ref[:, sl] = (_component_rms_norm(acc[:, sl], gain_ref[...]) * scale).astype(o_ref.dtype)
    elif kind == "norm_keep":
        gain_ref, *prev_refs, o_ref, fp_ref, fs_ref = refs
        normed = jnp.concatenate([_component_rms_norm(acc[:, sl], gain_ref[...]) for sl in heads], axis=1)
        _keep_f32(fp_ref, fs_ref, prev_refs, n_first, normed)
        o_ref[...] = normed.astype(o_ref.dtype)
    elif kind == "keep":
        *prev_refs, o_ref, fp_ref, fs_ref = refs
        _keep_f32(fp_ref, fs_ref, prev_refs, n_first, acc)
        o_ref[...] = acc.astype(o_ref.dtype)
    else:
        raise ValueError(kind)
    assert kind not in ("norm_keep", "keep") or len(prev_refs) == n_prev


def _segment(h, w_in, layer, seg0, nseg, tm, kind, *, scale=1.0, tables=(), gain=None, group_rows=None,
             prev_prompt=()):
    m, d = h.shape
    u = d // 2
    nheads = u // HEAD
    n_first = group_rows[0] // tm if group_rows else 0
    n_prev = len(prev_prompt)
    in_specs = [pl.BlockSpec((tm, d), lambda s, i: (i, 0)),
                pl.BlockSpec((None, d, u), lambda s, i: (layer, 0, seg0 + s))]
    args = [h, w_in]
    for tab in tables:
        in_specs.append(pl.BlockSpec((tm, HEAD), lambda s, i: (i, 0)))
        args.append(tab)
    if gain is not None:
        in_specs.append(pl.BlockSpec((None, 1, HEAD), lambda s, i: (layer, 0, 0)))
        args.append(gain)
    out_specs = [pl.BlockSpec((tm, u), lambda s, i: (i, s))]
    out_shape = [jax.ShapeDtypeStruct((m, nseg * u), h.dtype)]
    if group_rows:
        first, second = _group_specs(tm, u, n_first, row_axis=1)
        in_specs += [first] * n_prev
        args += list(prev_prompt)
        if n_prev:
            first = pl.BlockSpec((n_prev + 1, tm, u), lambda s, i: (0, jnp.minimum(i, n_first - 1), 0))
        out_specs += [first, second]
        out_shape += [jax.ShapeDtypeStruct(((n_prev + 1, group_rows[0], u) if n_prev else (group_rows[0], u)), F32),
                      jax.ShapeDtypeStruct((group_rows[1], u), F32)]
    hb = jnp.dtype(h.dtype).itemsize
    est = (2 * (tm * d * hb + d * u * 4 + tm * u * hb + (2 * n_prev + 2) * tm * u * 4 + 2 * tm * HEAD * 4)
           + d * u * hb + 4 * tm * u * 4)
    outs = pl.pallas_call(
        functools.partial(_segment_kernel, kind=kind, scale=scale, nheads=nheads, n_first=n_first, n_prev=n_prev),
        grid=(nseg, m // tm),
        in_specs=in_specs,
        out_specs=out_specs,
        out_shape=out_shape,
        scratch_shapes=[pltpu.VMEM((d, u), h.dtype)],
        compiler_params=_params(("arbitrary", "arbitrary"), est),
        name="input_projection_" + kind,
    )(*args)
    return outs if len(outs) > 1 else outs[0]


def _retention_kernel(q_ref, k_ref, v_ref, g_ref, s0_ref, d_ref, wq_ref, we_ref, dec_ref, o_ref, sout_ref, st_ref,
                      *, nheads, nblk):
    t = pl.program_id(1)

    @pl.when(t == 0)
    def _():
        st_ref[...] = s0_ref[...]

    lanes = [slice(h * HEAD, (h + 1) * HEAD) for h in range(nheads)]
    scores, inter = [], []
    for h, sl in enumerate(lanes):
        q, k, v = q_ref[:, sl], k_ref[:, sl], v_ref[:, sl]
        state = st_ref[h]
        scores.append(lax.dot_general(q, k, (((1,), (1,)), ((), ())), preferred_element_type=F32))
        inter.append(jnp.dot(q, state.astype(q.dtype), preferred_element_type=F32))
        kw = (k.astype(F32) * we_ref[:, sl]).astype(k.dtype)
        kv = lax.dot_general(kw, v, (((0,), (0,)), ((), ())), preferred_element_type=F32)
        st_ref[h] = state * dec_ref[h:h + 1, :] + kv
    for h, sl in enumerate(lanes):
        v = v_ref[:, sl]
        s = scores[h] * d_ref[h]
        o = jnp.dot(s.astype(v.dtype), v, preferred_element_type=F32) + wq_ref[:, sl] * inter[h]
        r = o * _rms_scale(o)
        o_ref[:, sl] = (r * g_ref[:, sl].astype(F32)).astype(o_ref.dtype)

    @pl.when(t == nblk - 1)
    def _():
        sout_ref[...] = st_ref[...]


def _retention_tables(t, nheads):
    log_g = jnp.log(1.0 - 2.0 ** (-5.0 - jnp.arange(nheads, dtype=F32)))
    idx = jnp.arange(t, dtype=F32)
    dist = jnp.abs(idx[:, None] - idx[None, :])
    ci = np.arange(t) // CHUNK
    visible = jnp.asarray(ci[None, :] <= ci[:, None])
    dmat = jnp.where(visible[None], jnp.exp(log_g[:, None, None] * dist[None]), 0.0)
    wq = jnp.exp(log_g[None, :] * (idx + 1.0)[:, None])
    we = jnp.exp(log_g[None, :] * (t - 1.0 - idx)[:, None])
    dec = jnp.exp(log_g * t)
    expand = lambda a: jnp.repeat(a, HEAD, axis=1)
    return dmat, expand(wq), expand(we), jnp.broadcast_to(dec[:, None], (nheads, HEAD))


def _retention(q, k, v, g, s0, t, nbatch, nblk, row_block0):
    nheads = s0.shape[1]
    u = nheads * HEAD
    dmat, wq, we, dec = _retention_tables(t, nheads)
    rows = pl.BlockSpec((t, u), lambda b, i: (row_block0 + b * nblk + i, 0))
    whole = lambda a: pl.BlockSpec(a.shape, lambda b, i: (0,) * a.ndim)
    state_spec = pl.BlockSpec((None, nheads, HEAD, HEAD), lambda b, i: (b, 0, 0, 0))
    pbytes = jnp.dtype(q.dtype).itemsize
    est = (2 * (5 * t * u * pbytes + 2 * nheads * HEAD * HEAD * 4 + nheads * t * t * 4 + 2 * t * u * 4)
           + nheads * HEAD * HEAD * 4 + 6 * t * max(t, HEAD) * 4)
    return pl.pallas_call(
        functools.partial(_retention_kernel, nheads=nheads, nblk=nblk),
        grid=(nbatch, nblk),
        in_specs=[rows, rows, rows, rows, state_spec, whole(dmat), whole(wq), whole(we), whole(dec)],
        out_specs=[
            pl.BlockSpec((t, u), lambda b, i: (b * nblk + i, 0)),
            state_spec,
        ],
        out_shape=[
            jax.ShapeDtypeStruct((nbatch * nblk * t, u), q.dtype),
            jax.ShapeDtypeStruct((nbatch, nheads, HEAD, HEAD), F32),
        ],
        scratch_shapes=[pltpu.VMEM((nheads, HEAD, HEAD), F32)],
        compiler_params=_params(("parallel", "arbitrary"), est),
        name="retention",
    )(q, k, v, g, s0, dmat, wq, we, dec)


def _bucket_thresholds():
    nb = N_BUCKETS // 2
    me = nb // 2
    out = []
    for k in range(1, nb - me):
        n = me
        while n ** (nb - me) * me ** k < me ** (nb - me) * MAX_DISTANCE ** k:
            n += 1
        out.append(n)
    return out


def _t5_bucket_np(rel):
    nb = N_BUCKETS // 2
    me = nb // 2
    n = np.abs(rel)
    large = np.full(rel.shape, me, np.int64)
    for thr in _bucket_thresholds():
        large += (n >= thr)
    large = np.minimum(large, nb - 1)
    return (np.where(rel > 0, nb, 0) + np.where(n < me, n, large)).astype(np.int32)


def _bias_kernel(rb_ref, idx_ref, mask_ref, o_ref):
    h = pl.program_id(0)
    idx = idx_ref[...]
    out = mask_ref[...]
    for b in range(N_BUCKETS):
        out = out + jnp.where(idx == b, rb_ref[b, h], 0.0)
    o_ref[...] = out


def _bias_table(rel_bias, qpos, kpos):
    nheads = rel_bias.shape[1]
    rel = kpos[None, :] - qpos[:, None]
    idx = jnp.asarray(_t5_bucket_np(rel))
    mask = jnp.asarray(np.where((kpos[None, :] // CHUNK) <= (qpos[:, None] // CHUNK), 0.0, MASK_VALUE)
                       .astype(np.float32))
    nq, nk = rel.shape
    return pl.pallas_call(
        _bias_kernel,
        grid=(nheads,),
        in_specs=[
            pl.BlockSpec(memory_space=pltpu.SMEM),
            pl.BlockSpec((nq, nk), lambda h: (0, 0)),
            pl.BlockSpec((nq, nk), lambda h: (0, 0)),
        ],
        out_specs=pl.BlockSpec((None, nq, nk), lambda h: (h, 0, 0)),
        out_shape=jax.ShapeDtypeStruct((nheads, nq, nk), F32),
        compiler_params=_params(("arbitrary",), 6 * nq * nk * 4),
        name="relative_bias_table",
    )(rel_bias, idx, mask)


def _stack_components(q, nsub=1):
    lo = lax.broadcasted_iota(jnp.int32, q.shape, 1) < HALF
    zero = jnp.zeros_like(q)
    comps = (jnp.where(lo, q, zero), jnp.where(lo, zero, q))
    rows = q.shape[0] // nsub
    return jnp.concatenate([c[s * rows:(s + 1) * rows] for s in range(nsub) for c in comps], axis=0)


def _add_table(s, bias, nsub=1):
    rows = bias.shape[0] // nsub
    parts = []
    for i in range(nsub):
        b = bias[i * rows:(i + 1) * rows]
        parts += [s[(2 * i + c) * rows:(2 * i + c + 1) * rows] + b for c in range(2)]
    return jnp.concatenate(parts, axis=0)


def _biased_scores(qs, k, bias, nsub=1):
    s = lax.dot_general(qs, k, (((1,), (1,)), ((), ())), preferred_element_type=F32)
    if bias.ndim == 2:
        return _add_table(s, bias, nsub), 0.0
    return s, bias


def _softmax_step(qs, k, v, bias, stats, g):
    m_ref, l_ref, acc_ref = stats
    s, c = _biased_scores(qs, k, bias)
    m_prev = m_ref[g]
    m_new = jnp.maximum(m_prev, jnp.max(s, axis=-1, keepdims=True) + c)
    alpha = jnp.exp(m_prev - m_new)
    p = jnp.exp(s - (m_new - c))
    l_ref[g] = alpha * l_ref[g] + jnp.sum(p, axis=-1, keepdims=True)
    acc_ref[g] = alpha * acc_ref[g] + jnp.dot(p.astype(v.dtype), v, preferred_element_type=F32)
    m_ref[g] = m_new


def _softmax_init(stats):
    m_ref, l_ref, acc_ref = stats
    m_ref[...] = jnp.full_like(m_ref, MASK_VALUE)
    l_ref[...] = jnp.zeros_like(l_ref)
    acc_ref[...] = jnp.zeros_like(acc_ref)


def _softmax_scratch(groups, t):
    return [pltpu.VMEM((groups, 2 * t, 1), F32), pltpu.VMEM((groups, 2 * t, 1), F32),
            pltpu.VMEM((groups, 2 * t, HEAD), F32)]


def _lambda_value(lam_ref, lam_init):
    a = lam_ref[...]
    e1 = jnp.exp(jnp.sum(a[0:1] * a[1:2], axis=-1, keepdims=True))
    e2 = jnp.exp(jnp.sum(a[2:3] * a[3:4], axis=-1, keepdims=True))
    return e1 - e2 + lam_init


def _diff_finish(t, lam, lam_init, subln, stats, g, nsub=1):
    _, l_ref, acc_ref = stats
    acc = acc_ref[g]
    l = l_ref[g]
    rows = t // nsub
    part = lambda a, i: a[i * rows:(i + 1) * rows]
    o = jnp.concatenate([part(acc, 2 * i) / part(l, 2 * i) - lam * (part(acc, 2 * i + 1) / part(l, 2 * i + 1))
                         for i in range(nsub)], axis=0)
    return o * _rms_scale(o) * subln * (1.0 - lam_init)


def _shift_brackets(rb_ref, qg_ref, kg_ref, qs, own_ks, head0, m_ref, nsub=1):
    gain_bound = (SCORE_BOUND_MARGIN * HALF ** 0.5 * jnp.max(jnp.abs(qg_ref[...]), axis=-1, keepdims=True)
                  * jnp.max(jnp.abs(kg_ref[...]), axis=-1, keepdims=True))
    widest = jnp.zeros((1, 1), F32)
    for g, (q, own_k) in enumerate(zip(qs, own_ks)):
        head = head0 + g
        bias_max = rb_ref[0, head]
        for b in range(1, N_BUCKETS):
            bias_max = jnp.maximum(bias_max, rb_ref[b, head])
        upper = gain_bound + bias_max
        own_k = own_k.astype(F32)
        rows = own_k.shape[0] // nsub
        own_stacked = jnp.concatenate([own_k[i * rows:(i + 1) * rows] for i in range(nsub) for _ in range(2)], axis=0)
        own = jnp.sum(q.astype(F32) * own_stacked, axis=-1, keepdims=True)
        lower = own + rb_ref[0, head]
        m_ref[g] = 0.5 * (upper + lower)
        widest = jnp.maximum(widest, jnp.max(upper - lower, axis=0, keepdims=True))
    return widest[0, 0]


def _attn_prompt_kernel(rb_ref, q_ref, k_ref, v_ref, bias_ref, lam_ref, sub_ref, qg_ref, kg_ref, o_ref, *stats,
                        t, groups, nsub, lam_init, far_bucket):
    hp = pl.program_id(1)
    qi = pl.program_id(2)
    m_ref, l_ref, acc_ref = stats
    lanes = [slice(g * HEAD, (g + 1) * HEAD) for g in range(groups)]
    qs = [_stack_components(q_ref[:, sl], nsub) for sl in lanes]
    far_bias = [rb_ref[far_bucket, hp * groups + g] for g in range(groups)]
    _softmax_init(stats)

    def sweep(step, diag_step):
        def run(j, bias_of):
            step(pl.ds(pl.multiple_of(j * t, t), t), [bias_of(g) for g in range(groups)])

        def far_step(j, carry):
            run(j, lambda g: far_bias[g])
            return carry

        lax.fori_loop(0, jnp.maximum(qi - 1, 0), far_step, 0)

        @pl.when(qi > 0)
        def _():
            run(qi - 1, lambda g: bias_ref[g, :, :t])

        diag_step(pl.multiple_of(qi * t, t))

    def max_step(rows, biases):
        scores = [_biased_scores(qs[g], k_ref[rows, lanes[g]], biases[g], nsub) for g in range(groups)]
        for g, (s, c) in enumerate(scores):
            m_ref[g] = jnp.maximum(m_ref[g], jnp.max(s, axis=-1, keepdims=True) + c)

    def acc_step(rows, biases):
        scores = [_biased_scores(qs[g], k_ref[rows, lanes[g]], biases[g], nsub) for g in range(groups)]
        probs = [jnp.exp(s - (m_ref[g] - c)) for g, (s, c) in enumerate(scores)]
        for g, p in enumerate(probs):
            l_ref[g] += jnp.sum(p, axis=-1, keepdims=True)
            v = v_ref[rows, lanes[g]]
            acc_ref[g] += jnp.dot(p.astype(v.dtype), v, preferred_element_type=F32)

    def diag_max_step(start):
        max_step(pl.ds(start, t), [bias_ref[g, :, t:] for g in range(groups)])

    def diag_acc_step(start):
        rows = t // nsub
        parts = [(i, g) for i in range(nsub) for g in range(groups)]
        stacked = lambda i: slice(2 * i * rows, 2 * (i + 1) * rows)
        keys = lambda i: pl.ds(start, (i + 1) * rows)
        scores = []
        for i, g in parts:
            table = bias_ref[g, i * rows:(i + 1) * rows, t:t + (i + 1) * rows]
            scores.append(_biased_scores(qs[g][stacked(i)], k_ref[keys(i), lanes[g]], table)[0])
        probs = [jnp.exp(s - m_ref[g, stacked(i)]) for (i, g), s in zip(parts, scores)]
        for (i, g), p in zip(parts, probs):
            l_ref[g, stacked(i)] += jnp.sum(p, axis=-1, keepdims=True)
            v = v_ref[keys(i), lanes[g]]
            acc_ref[g, stacked(i)] += jnp.dot(p.astype(v.dtype), v, preferred_element_type=F32)

    own_rows = pl.ds(pl.multiple_of(qi * t, t), t)
    widest = _shift_brackets(rb_ref, qg_ref, kg_ref, qs, [k_ref[own_rows, sl] for sl in lanes],
                             hp * groups, m_ref, nsub)

    @pl.when(widest > MAX_SHIFT_BRACKET)
    def _():
        m_ref[...] = jnp.full_like(m_ref, MASK_VALUE)
        sweep(max_step, diag_max_step)

    sweep(acc_step, diag_acc_step)

    lam = _lambda_value(lam_ref, lam_init)
    for g, sl in enumerate(lanes):
        o_ref[:, sl] = _diff_finish(t, lam, lam_init, sub_ref[...], stats, g, nsub).astype(o_ref.dtype)


def _attn_prompt(q, k, v, rel_bias, lam_params, subln2, qn2, kn2, layer, nbatch, seq, nheads, lam_init):
    u = nheads * HEAD
    t = _pick(seq, ATTN_BLOCKS)
    groups = _pick(nheads, HEADS_PER_STEP)
    assert t % CHUNK == 0 and t + 1 >= _bucket_thresholds()[-1]
    nsub = QUERY_SUBBLOCKS if (t // QUERY_SUBBLOCKS) % max(CHUNK, HEAD) == 0 else 1
    nq = seq // t
    w = groups * HEAD
    r = np.arange(t)
    bias = _bias_table(rel_bias, r + t, np.arange(2 * t))
    pbytes = jnp.dtype(q.dtype).itemsize
    est = (2 * (2 * t * w * pbytes + 2 * seq * w * pbytes + groups * 2 * t * t * 4)
           + groups * (2 * t * (HEAD + 2 * 128) * 4 + 6 * 2 * t * t * 4))
    return pl.pallas_call(
        functools.partial(_attn_prompt_kernel, t=t, groups=groups, nsub=nsub, lam_init=lam_init,
                          far_bucket=N_BUCKETS // 2 - 1),
        grid=(nbatch, nheads // groups, nq),
        in_specs=[
            pl.BlockSpec(memory_space=pltpu.SMEM),
            pl.BlockSpec((t, w), lambda b, h, i: (b * nq + i, h)),
            pl.BlockSpec((seq, w), lambda b, h, i: (b, h)),
            pl.BlockSpec((seq, w), lambda b, h, i: (b, h)),
            pl.BlockSpec((groups, t, 2 * t), lambda b, h, i: (h, 0, 0)),
            pl.BlockSpec((None, 4, HALF), lambda b, h, i: (layer, 0, 0)),
            pl.BlockSpec((None, 1, HEAD), lambda b, h, i: (layer, 0, 0)),
            pl.BlockSpec((None, 1, HEAD), lambda b, h, i: (layer, 0, 0)),
            pl.BlockSpec((None, 1, HEAD), lambda b, h, i: (layer, 0, 0)),
        ],
        out_specs=pl.BlockSpec((t, w), lambda b, h, i: (b * nq + i, h)),
        out_shape=jax.ShapeDtypeStruct((nbatch * seq, u), q.dtype),
        scratch_shapes=_softmax_scratch(groups, t),
        compiler_params=_params(("parallel", "parallel", "arbitrary"), est),
        name="diff_attention_prompt",
    )(rel_bias, q, k, v, bias, lam_params, subln2, qn2, kn2)


def _attn_sample_kernel(rb_ref, q_ref, kn_ref, vn_ref, kc_ref, vc_ref, bc_ref, bn_ref, lam_ref, sub_ref,
                        qg_ref, kg_ref, o_ref, *stats, t, tk, ncache, nheads, lam_init):
    *stats, wide_ref = stats
    m_ref, l_ref, acc_ref = stats
    j = pl.program_id(1)
    lanes = [slice(h * HEAD, (h + 1) * HEAD) for h in range(nheads)]
    heads = range(nheads)
    qs = [_stack_components(q_ref[:, sl]) for sl in lanes]

    @pl.when(j == 0)
    def _():
        _softmax_init(stats)
        widest = _shift_brackets(rb_ref, qg_ref, kg_ref, qs, [kn_ref[:, sl] for sl in lanes], 0, m_ref)
        wide_ref[0] = (widest > MAX_SHIFT_BRACKET).astype(jnp.int32)

        @pl.when(widest > MAX_SHIFT_BRACKET)
        def _():
            m_ref[...] = jnp.full_like(m_ref, MASK_VALUE)

    def fixed_shift_steps(ks, vs, biases):
        scores = [_biased_scores(qs[h], ks[h], biases[h])[0] for h in heads]
        probs = [jnp.exp(scores[h] - m_ref[h]) for h in heads]
        for h in heads:
            l_ref[h] += jnp.sum(probs[h], axis=-1, keepdims=True)
            acc_ref[h] += jnp.dot(probs[h].astype(vs[h].dtype), vs[h], preferred_element_type=F32)

    def online_steps(ks, vs, biases):
        for h in heads:
            _softmax_step(qs[h], ks[h], vs[h], biases[h], stats, h)

    def both(ks, vs, biases):
        @pl.when(wide_ref[0] == 0)
        def _():
            fixed_shift_steps(ks(), vs(), biases())

        @pl.when(wide_ref[0] != 0)
        def _():
            online_steps(ks(), vs(), biases())

    head_rows = lambda ref: [ref[pl.ds(h, tk, stride=nheads), :].astype(qs[0].dtype) for h in heads]
    both(lambda: head_rows(kc_ref), lambda: head_rows(vc_ref), lambda: [bc_ref[h] for h in heads])

    @pl.when(j == ncache - 1)
    def _():
        both(lambda: [kn_ref[:, sl] for sl in lanes], lambda: [vn_ref[:, sl] for sl in lanes],
             lambda: [bn_ref[h] for h in heads])
        lam = _lambda_value(lam_ref, lam_init)
        for h, sl in enumerate(lanes):
            o_ref[:, sl] = _diff_finish(t, lam, lam_init, sub_ref[...], stats, h).astype(o_ref.dtype)


def _attn_sample(q, k, v, cache_k, cache_v, rel_bias, lam_params, subln2, qn2, kn2, layer, nbatch, t, past,
                 row0, nheads, lam_init):
    u = nheads * HEAD
    assert row0 % t == 0
    rb0 = row0 // t
    tk = _pick(past, CACHE_TILES)
    qpos = past + np.arange(t)
    ncache = past // tk
    bias_c = _bias_table(rel_bias, qpos, np.arange(past))
    bias_c = bias_c.reshape(nheads, t, ncache, tk).transpose(2, 0, 1, 3)
    bias_n = _bias_table(rel_bias, qpos, qpos)
    pbytes = jnp.dtype(q.dtype).itemsize
    est = (2 * (4 * t * u * pbytes + 2 * tk * nheads * HEAD * 4 + nheads * t * (tk + t) * 4)
           + nheads * (2 * t * (HEAD + 2 * 128) * 4 + 6 * 2 * t * tk * 4))
    rows = pl.BlockSpec((t, u), lambda b, j: (rb0 + b, 0))
    cache = pl.BlockSpec((None, tk * nheads, HEAD), lambda b, j: (layer, b * ncache + j, 0))
    return pl.pallas_call(
        functools.partial(_attn_sample_kernel, t=t, tk=tk, ncache=ncache, nheads=nheads, lam_init=lam_init),
        grid=(nbatch, ncache),
        in_specs=[
            pl.BlockSpec(memory_space=pltpu.SMEM),
            rows, rows, rows, cache, cache,
            pl.BlockSpec((None, nheads, t, tk), lambda b, j: (j, 0, 0, 0)),
            pl.BlockSpec((nheads, t, t), lambda b, j: (0, 0, 0)),
            pl.BlockSpec((None, 4, HALF), lambda b, j: (layer, 0, 0)),
            pl.BlockSpec((None, 1, HEAD), lambda b, j: (layer, 0, 0)),
            pl.BlockSpec((None, 1, HEAD), lambda b, j: (layer, 0, 0)),
            pl.BlockSpec((None, 1, HEAD), lambda b, j: (layer, 0, 0)),
        ],
        out_specs=pl.BlockSpec((t, u), lambda b, j: (b, 0)),
        out_shape=jax.ShapeDtypeStruct((nbatch * t, u), q.dtype),
        scratch_shapes=_softmax_scratch(nheads, t) + [pltpu.SMEM((1,), jnp.int32)],
        compiler_params=_params(("parallel", "arbitrary"), est),
        name="diff_attention_sample",
    )(rel_bias, q, k, v, cache_k, cache_v, bias_c, bias_n, lam_params, subln2, qn2, kn2)


def _merge_kernel(ap_ref, as_ref, bp_ref, bs_ref, ga0_ref, ga1_ref, gb0_ref, gb1_ref, wa_ref, wb_ref, o_ref,
                  *, u, n_first):
    a = _group_tile((ap_ref, as_ref), n_first)
    b = _group_tile((bp_ref, bs_ref), n_first)
    ya = jnp.dot(a, wa_ref[...], preferred_element_type=F32)
    yb = jnp.dot(b, wb_ref[...], preferred_element_type=F32)
    for c, (ga, gb) in enumerate(((ga0_ref, gb0_ref), (ga1_ref, gb1_ref))):
        sl = slice(c * u, (c + 1) * u)
        o_ref[:, sl] = (ga[...].astype(F32) * ya[:, sl] + gb[...].astype(F32) * yb[:, sl]).astype(o_ref.dtype)


def _merge(ret_outs, dif_outs, gates, w_ret_up, w_dif_up, layer, tm):
    m = gates.shape[0]
    u = ret_outs[0].shape[1]
    d = 2 * u
    n_first = ret_outs[0].shape[0] // tm
    gate = lambda c: pl.BlockSpec((tm, u), lambda i: (i, c))
    rows = _group_specs(tm, u, n_first)
    wspec = pl.BlockSpec((None, u, d), lambda i: (layer, 0, 0))
    pbytes = jnp.dtype(gates.dtype).itemsize
    est = 2 * (8 * tm * u * pbytes + 2 * u * d * pbytes + tm * d * pbytes) + 3 * tm * d * 4
    return pl.pallas_call(
        functools.partial(_merge_kernel, u=u, n_first=n_first),
        grid=(m // tm,),
        in_specs=rows + rows + [gate(0), gate(1), gate(2), gate(3), wspec, wspec],
        out_specs=pl.BlockSpec((tm, d), lambda i: (i, 0)),
        out_shape=jax.ShapeDtypeStruct((m, d), gates.dtype),
        compiler_params=_params(("arbitrary",), est),
        name="gated_merge",
    )(*ret_outs, *dif_outs, gates, gates, gates, gates, w_ret_up, w_dif_up)


def _out_proj_kernel(x_ref, a_ref, w_ref, o_ref):
    o_ref[...] = x_ref[...] + jnp.dot(a_ref[...], w_ref[...], preferred_element_type=F32)


def _out_proj(x, merged, w_out, layer, tm):
    m, d = x.shape
    wbytes = jnp.dtype(w_out.dtype).itemsize
    est = 2 * (2 * tm * d * 4 + tm * d * wbytes + d * d * wbytes) + tm * d * 4
    return pl.pallas_call(
        _out_proj_kernel,
        grid=(m // tm,),
        in_specs=[
            pl.BlockSpec((tm, d), lambda i: (i, 0)),
            pl.BlockSpec((tm, d), lambda i: (i, 0)),
            pl.BlockSpec((None, d, d), lambda i: (layer, 0, 0)),
        ],
        out_specs=pl.BlockSpec((tm, d), lambda i: (i, 0)),
        out_shape=jax.ShapeDtypeStruct((m, d), F32),
        compiler_params=_params(("parallel",), est),
        name="output_projection",
    )(x, merged, w_out)


def _rotary_tables(pos):
    inv = ROPE_BASE ** (-jnp.arange(HALF, dtype=F32) / HALF)
    ang = pos.astype(F32)[:, None] * inv[None, :]
    cos, sin = jnp.cos(ang), jnp.sin(ang)
    return jnp.concatenate([cos, cos], axis=-1), jnp.concatenate([-sin, sin], axis=-1)


def kernel(x_prompt, x_sample, cache_diff_k, cache_diff_v, state_ret, ffn1_norm, ffn1_gate, ffn1_up, ffn1_down, mix_norm, w_in, q_norm, k_norm, lambda_q1, lambda_k1, lambda_q2, lambda_k2, subln, w_ret_up, w_dif_up, w_out, ffn2_norm, ffn2_gate, ffn2_up, ffn2_down, rel_bias):
    nb, seq, d = x_prompt.shape
    db, dseq, _ = x_sample.shape
    depth, _, past, nh_d, _ = cache_diff_k.shape
    nh_r = state_ret.shape[2]
    u = d // 2
    assert nh_r * HEAD == u and nh_d * HEAD == u and dseq == CHUNK and seq % CHUNK == 0
    assert w_in.shape[-1] == N_SEG * u
    mp, ms = nb * seq, db * dseq

    cast = lambda w: w.astype(MXU_DTYPE)
    wg1, wu1, wd1 = cast(ffn1_gate), cast(ffn1_up), cast(ffn1_down)
    wg2, wu2, wd2 = cast(ffn2_gate), cast(ffn2_up), cast(ffn2_down)
    w_ret_c, w_dif_c, w_out_c = cast(w_ret_up), cast(w_dif_up), cast(w_out)
    row3 = lambda g: g.reshape(depth, 1, g.shape[-1])
    n1, nmix, n2 = row3(ffn1_norm), row3(mix_norm), row3(ffn2_norm)
    qn2 = row3(jnp.concatenate([q_norm, q_norm], axis=-1))
    kn2 = row3(jnp.concatenate([k_norm, k_norm], axis=-1))
    subln2 = row3(subln)
    lam_params = jnp.stack([lambda_q1, lambda_k1, lambda_q2, lambda_k2], axis=1)
    cache_k = cache_diff_k.reshape(depth, db * past * nh_d, HEAD)
    cache_v = cache_diff_v.reshape(depth, db * past * nh_d, HEAD)

    pos = jnp.concatenate([jnp.tile(jnp.arange(seq, dtype=jnp.int32), nb),
                           past + jnp.tile(jnp.arange(dseq, dtype=jnp.int32), db)])
    rot = _rotary_tables(pos)

    tm = _pick(math.gcd(mp, ms), ROW_TILES)
    tm_proj = _pick(mp + ms, PROJ_ROW_TILES)
    groups = (mp, ms)
    zero_state = jnp.zeros((nb, nh_r, HEAD, HEAD), F32)
    t_ret = _pick(seq, RET_BLOCKS)

    x = (x_prompt.reshape(mp, d), x_sample.reshape(ms, d))
    kps, kss, vps, vss, states_p, states_s = [], [], [], [], [], []
    for l in range(depth):
        lam_init = 0.8 - 0.6 * math.exp(-0.3 * l)
        x, h = _ffn(x, n1, wg1, wu1, wd1, l, tm, next_gain=nmix)

        last = l == depth - 1
        seg = functools.partial(_segment, h, w_in, l)
        rq = seg(SEG_RQ, 1, tm_proj, "rotary", tables=rot)
        rk = seg(SEG_RK, 1, tm_proj, "rotary", tables=rot, scale=HEAD ** -0.5)
        rv = seg(SEG_RV, 1, tm_proj, "cast")
        rg = seg(SEG_RG, 1, tm_proj, "silu")
        dq = seg(SEG_DQ, 1, tm_proj, "norm", gain=qn2, scale=HALF ** -0.5)
        dk, k_p, k_s = seg(SEG_DK, 1, tm, "norm_keep", gain=kn2, group_rows=groups,
                           prev_prompt=kps if last else ())
        dv, v_p, v_s = seg(SEG_DV, 1, tm, "keep", group_rows=groups, prev_prompt=vps if last else ())
        gates = seg(SEG_GATES, N_SEG - SEG_GATES, tm_proj, "sigmoid")

        ret_p, st_p = _retention(rq, rk, rv, rg, zero_state, t_ret, nb, seq // t_ret, 0)
        ret_s, st_s = _retention(rq, rk, rv, rg, state_ret[l].astype(F32), dseq, db, 1, mp // dseq)

        dif_p = _attn_prompt(dq, dk, dv, rel_bias, lam_params, subln2, qn2, kn2, l, nb, seq, nh_d, lam_init)
        dif_s = _attn_sample(dq, dk, dv, cache_k, cache_v, rel_bias, lam_params, subln2, qn2, kn2, l, db, dseq,
                             past, mp, nh_d, lam_init)

        merged = _merge((ret_p, ret_s), (dif_p, dif_s), gates, w_ret_c, w_dif_c, l, tm)
        x = _out_proj(x, merged, w_out_c, l, tm)
        x = _ffn(x, n2, wg2, wu2, wd2, l, tm, split_out=groups if l == depth - 1 else None)

        for acc, val in ((kps, k_p), (kss, k_s), (vps, v_p), (vss, v_s), (states_p, st_p), (states_s, st_s)):
            acc.append(val)

    y_p, y_s = x
    kv_p = lambda parts: parts[-1].reshape(depth, nb, seq, nh_d, HEAD)
    kv_s = lambda parts: jnp.stack(parts).reshape(depth, db, dseq, nh_d, HEAD)
    return (y_p.reshape(nb, seq, d), y_s.reshape(db, dseq, d),
            kv_p(kps).astype(cache_diff_k.dtype), kv_p(vps).astype(cache_diff_v.dtype),
            jnp.stack(states_p).astype(state_ret.dtype),
            kv_s(kss).astype(cache_diff_k.dtype), kv_s(vss).astype(cache_diff_v.dtype),
            jnp.stack(states_s).astype(state_ret.dtype))
```

```python
import functools
import math

import numpy as np
import jax
import jax.numpy as jnp
from jax import lax
from jax.experimental import pallas as pl
from jax.experimental.pallas import tpu as pltpu

F32 = jnp.float32
MXU_DTYPE = jnp.bfloat16

CHUNK = 64
HEAD = 128
HALF = HEAD // 2
ROPE_BASE = 10000.0
N_BUCKETS = 32
MAX_DISTANCE = 128
EPS = 1e-6
MASK_VALUE = -1e30
SCORE_BOUND_MARGIN = 1.03
MAX_SHIFT_BRACKET = 100.0

VMEM_LIMIT_CAP = 60 * 1024 * 1024
MIB = 1024 * 1024

ROW_TILES = (512, 256, 128, 64)
PROJ_ROW_TILES = (1536, 1024, 768, 512, 256, 128, 64)
FF_TILES = (512, 256, 128)
ATTN_BLOCKS = (512, 256, 128)
HEADS_PER_STEP = (4, 2, 1)
QUERY_SUBBLOCKS = 4
RET_BLOCKS = (256, 128, 64)
CACHE_TILES = (2048, 1024, 512, 256, 128, 64)


def _pick(n, prefs):
    for p in prefs:
        if n % p == 0:
            return p
    raise ValueError(f"no tile in {prefs} divides {n}")


def _params(semantics, est_bytes):
    limit = int(min(max(est_bytes + 8 * MIB, 32 * MIB), VMEM_LIMIT_CAP))
    return pltpu.CompilerParams(dimension_semantics=semantics, vmem_limit_bytes=limit)


def _rms_scale(x):
    return lax.rsqrt(jnp.mean(x * x, axis=-1, keepdims=True) + EPS)


def _group_specs(tm, width, n_first, row_axis=0):
    first = pl.BlockSpec((tm, width), lambda *g: (jnp.minimum(g[row_axis], n_first - 1), 0))
    second = pl.BlockSpec((tm, width), lambda *g: (jnp.maximum(g[row_axis] - n_first, 0), 0),
                          pipeline_mode=pl.Buffered(1))
    return [first, second]


def _group_tile(refs, n_first, row_axis=0):
    if len(refs) == 1:
        return refs[0][...]
    return jnp.where(pl.program_id(row_axis) < n_first, refs[0][...], refs[1][...])


def _store_group_tile(refs, n_first, value, row_axis=0):
    if len(refs) == 1:
        refs[0][...] = value
        return
    i = pl.program_id(row_axis)

    @pl.when(i < n_first)
    def _():
        refs[0][...] = value

    @pl.when(i >= n_first)
    def _():
        refs[1][...] = value


def _ffn_kernel(*refs, nf, n_in, n_out, n_first, norm_out):
    x_refs = refs[:n_in]
    g_ref, wg_ref, wu_ref, wd_ref = refs[n_in:n_in + 4]
    pos = n_in + 4
    g2_ref = refs[pos] if norm_out else None
    pos += int(norm_out)
    o_refs = refs[pos:pos + n_out]
    pos += n_out
    hn_ref = refs[pos] if norm_out else None
    pos += int(norm_out)
    h_ref, acc_ref = refs[pos:]
    f = pl.program_id(1)

    def chunk(h):
        gate = jnp.dot(h, wg_ref[...], preferred_element_type=F32)
        up = jnp.dot(h, wu_ref[...], preferred_element_type=F32)
        act = (gate * jax.nn.sigmoid(gate) * up).astype(wd_ref.dtype)
        return jnp.dot(act, wd_ref[...], preferred_element_type=F32)

    def normed_input():
        x = _group_tile(x_refs, n_first)
        return (x * _rms_scale(x) * g_ref[...]).astype(h_ref.dtype)

    def finish(acc):
        y = _group_tile(x_refs, n_first) + 0.5 * acc
        _store_group_tile(o_refs, n_first, y)
        if norm_out:
            hn_ref[...] = (y * _rms_scale(y) * g2_ref[...]).astype(hn_ref.dtype)

    if nf == 1:
        finish(chunk(normed_input()))
        return

    @pl.when(f == 0)
    def _():
        h = normed_input()
        h_ref[...] = h
        acc_ref[...] = chunk(h)

    @pl.when(jnp.logical_and(f > 0, f < nf - 1))
    def _():
        acc_ref[...] += chunk(h_ref[...])

    @pl.when(f == nf - 1)
    def _():
        finish(acc_ref[...] + chunk(h_ref[...]))


def _ffn(xs, gain, wg, wu, wd, layer, tm, split_out=None, next_gain=None):
    xs = tuple(xs) if isinstance(xs, (tuple, list)) else (xs,)
    m = sum(x.shape[0] for x in xs)
    d = xs[0].shape[1]
    n_first = (xs[0].shape[0] if len(xs) == 2 else split_out[0] if split_out else m) // tm
    ff = wg.shape[-1]
    tf = _pick(ff, FF_TILES)
    nf = ff // tf
    norm_out = next_gain is not None
    wbytes = jnp.dtype(wg.dtype).itemsize
    row_buffers = (3 if len(xs) == 2 else 2) + (3 if split_out else 2)
    est = (row_buffers * tm * d * 4 + 2 * (3 * d * tf * wbytes + int(norm_out) * tm * d * wbytes)
           + tm * d * (4 + wbytes) + 4 * tm * tf * 4)
    rows = lambda: pl.BlockSpec((tm, d), lambda i, f: (i, 0))
    gain_spec = pl.BlockSpec((None, 1, d), lambda i, f: (layer, 0, 0))
    in_specs = (_group_specs(tm, d, n_first) if len(xs) == 2 else [rows()]) + [
        gain_spec,
        pl.BlockSpec((None, d, tf), lambda i, f: (layer, 0, f)),
        pl.BlockSpec((None, d, tf), lambda i, f: (layer, 0, f)),
        pl.BlockSpec((None, tf, d), lambda i, f: (layer, f, 0)),
    ]
    args = [*xs, gain, wg, wu, wd]
    if split_out:
        out_specs = _group_specs(tm, d, n_first)
        out_shape = [jax.ShapeDtypeStruct((r, d), F32) for r in split_out]
    else:
        out_specs = [rows()]
        out_shape = [jax.ShapeDtypeStruct((m, d), F32)]
    if norm_out:
        in_specs.append(gain_spec)
        args.append(next_gain)
        out_specs.append(rows())
        out_shape.append(jax.ShapeDtypeStruct((m, d), wg.dtype))
    outs = pl.pallas_call(
        functools.partial(_ffn_kernel, nf=nf, n_in=len(xs), n_out=2 if split_out else 1, n_first=n_first,
                          norm_out=norm_out),
        grid=(m // tm, nf),
        in_specs=in_specs,
        out_specs=out_specs,
        out_shape=out_shape,
        scratch_shapes=[pltpu.VMEM((tm, d), wg.dtype), pltpu.VMEM((tm, d), F32)],
        compiler_params=_params(("arbitrary", "arbitrary"), est),
        name="swiglu_half_step",
    )(*args)
    return outs if len(outs) > 1 else outs[0]


SEG_RQ, SEG_RK, SEG_RV, SEG_RG, SEG_DQ, SEG_DK, SEG_DV, SEG_GATES, N_SEG = 0, 1, 2, 3, 4, 5, 6, 7, 11


def _rotate_half_pairs(a, cos2, sin2):
    return a * cos2 + pltpu.roll(a, HALF, 1) * sin2


def _component_rms_norm(a, gain2):
    lo = lax.broadcasted_iota(jnp.int32, a.shape, 1) < HALF
    sq = a * a
    s_all = jnp.sum(sq, axis=-1, keepdims=True)
    s_lo = jnp.sum(jnp.where(lo, sq, 0.0), axis=-1, keepdims=True)
    ms = jnp.where(lo, s_lo, s_all - s_lo) * (1.0 / HALF)
    return a * lax.rsqrt(ms + EPS) * gain2


def _keep_f32(fp_ref, fs_ref, prev_refs, n_first, value):
    i = pl.program_id(1)

    @pl.when(i < n_first)
    def _():
        if prev_refs:
            for l, prev in enumerate(prev_refs):
                fp_ref[l] = prev[...]
            fp_ref[len(prev_refs)] = value
        else:
            fp_ref[...] = value

    @pl.when(i >= n_first)
    def _():
        fs_ref[...] = value


def _segment_kernel(h_ref, w_ref, *refs, kind, scale, nheads, n_first, n_prev):
    *refs, wc_ref = refs

    @pl.when(pl.program_id(1) == 0)
    def _():
        wc_ref[...] = w_ref[...].astype(wc_ref.dtype)

    acc = jnp.dot(h_ref[...], wc_ref[...], preferred_element_type=F32)
    heads = [slice(h * HEAD, (h + 1) * HEAD) for h in range(nheads)]
    if kind == "rotary":
        cos_ref, sin_ref, o_ref = refs
        for sl in heads:
            r = _rotate_half_pairs(acc[:, sl], cos_ref[...], sin_ref[...])
            o_ref[:, sl] = (r if scale == 1.0 else r * scale).astype(o_ref.dtype)
    elif kind == "cast":
        (o_ref,) = refs
        o_ref[...] = acc.astype(o_ref.dtype)
    elif kind == "silu":
        (o_ref,) = refs
        o_ref[...] = (acc * jax.nn.sigmoid(acc)).astype(o_ref.dtype)
    elif kind == "sigmoid":
        (o_ref,) = refs
        o_ref[...] = jax.nn.sigmoid(acc).astype(o_ref.dtype)
    elif kind == "norm":
        gain_ref, o_ref = refs
        for sl in heads:
            o_ref[:, sl] = (_component_rms_norm(acc[:, sl], gain_ref[...]) * scale).astype(o_ref.dtype)
    elif kind == "norm_keep":
        gain_ref, *prev_refs, o_ref, fp_ref, fs_ref = refs
        normed = jnp.concatenate([_component_rms_norm(acc[:, sl], gain_ref[...]) for sl in heads], axis=1)
        _keep_f32(fp_ref, fs_ref, prev_refs, n_first, normed)
        o_ref[...] = normed.astype(o_ref.dtype)
    elif kind == "keep":
        *prev_refs, o_ref, fp_ref, fs_ref = refs
        _keep_f32(fp_ref, fs_ref, prev_refs, n_first, acc)
        o_ref[...] = acc.astype(o_ref.dtype)
    else:
        raise ValueError(kind)
    assert kind not in ("norm_keep", "keep") or len(prev_refs) == n_prev


def _segment(h, w_in, layer, seg0, nseg, tm, kind, *, scale=1.0, tables=(), gain=None, group_rows=None,
             prev_prompt=()):
    m, d = h.shape
    u = d // 2
    nheads = u // HEAD
    n_first = group_rows[0] // tm if group_rows else 0
    n_prev = len(prev_prompt)
    in_specs = [pl.BlockSpec((tm, d), lambda s, i: (i, 0)),
                pl.BlockSpec((None, d, u), lambda s, i: (layer, 0, seg0 + s))]
    args = [h, w_in]
    for tab in tables:
        in_specs.append(pl.BlockSpec((tm, HEAD), lambda s, i: (i, 0)))
        args.append(tab)
    if gain is not None:
        in_specs.append(pl.BlockSpec((None, 1, HEAD), lambda s, i: (layer, 0, 0)))
        args.append(gain)
    out_specs = [pl.BlockSpec((tm, u), lambda s, i: (i, s))]
    out_shape = [jax.ShapeDtypeStruct((m, nseg * u), h.dtype)]
    if group_rows:
        first, second = _group_specs(tm, u, n_first, row_axis=1)
        in_specs += [first] * n_prev
        args += list(prev_prompt)
        if n_prev:
            first = pl.BlockSpec((n_prev + 1, tm, u), lambda s, i: (0, jnp.minimum(i, n_first - 1), 0))
        out_specs += [first, second]
        out_shape += [jax.ShapeDtypeStruct(((n_prev + 1, group_rows[0], u) if n_prev else (group_rows[0], u)), F32),
                      jax.ShapeDtypeStruct((group_rows[1], u), F32)]
    hb = jnp.dtype(h.dtype).itemsize
    est = (2 * (tm * d * hb + d * u * 4 + tm * u * hb + (2 * n_prev + 2) * tm * u * 4 + 2 * tm * HEAD * 4)
           + d * u * hb + 4 * tm * u * 4)
    outs = pl.pallas_call(
        functools.partial(_segment_kernel, kind=kind, scale=scale, nheads=nheads, n_first=n_first, n_prev=n_prev),
        grid=(nseg, m // tm),
        in_specs=in_specs,
        out_specs=out_specs,
        out_shape=out_shape,
        scratch_shapes=[pltpu.VMEM((d, u), h.dtype)],
        compiler_params=_params(("arbitrary", "arbitrary"), est),
        name="input_projection_" + kind,
    )(*args)
    return outs if len(outs) > 1 else outs[0]


def _retention_kernel(q_ref, k_ref, v_ref, g_ref, s0_ref, d_ref, wq_ref, we_ref, dec_ref, o_ref, sout_ref, st_ref,
                      *, nheads, nblk):
    t = pl.program_id(1)

    @pl.when(t == 0)
    def _():
        st_ref[...] = s0_ref[...]

    lanes = [slice(h * HEAD, (h + 1) * HEAD) for h in range(nheads)]
    scores, inter = [], []
    for h, sl in enumerate(lanes):
        q, k, v = q_ref[:, sl], k_ref[:, sl], v_ref[:, sl]
        state = st_ref[h]
        scores.append(lax.dot_general(q, k, (((1,), (1,)), ((), ())), preferred_element_type=F32))
        inter.append(jnp.dot(q, state.astype(q.dtype), preferred_element_type=F32))
        kw = (k.astype(F32) * we_ref[:, sl]).astype(k.dtype)
        kv = lax.dot_general(kw, v, (((0,), (0,)), ((), ())), preferred_element_type=F32)
        st_ref[h] = state * dec_ref[h:h + 1, :] + kv
    for h, sl in enumerate(lanes):
        v = v_ref[:, sl]
        s = scores[h] * d_ref[h]
        o = jnp.dot(s.astype(v.dtype), v, preferred_element_type=F32) + wq_ref[:, sl] * inter[h]
        r = o * _rms_scale(o)
        o_ref[:, sl] = (r * g_ref[:, sl].astype(F32)).astype(o_ref.dtype)

    @pl.when(t == nblk - 1)
    def _():
        sout_ref[...] = st_ref[...]


def _retention_tables(t, nheads):
    log_g = jnp.log(1.0 - 2.0 ** (-5.0 - jnp.arange(nheads, dtype=F32)))
    idx = jnp.arange(t, dtype=F32)
    dist = jnp.abs(idx[:, None] - idx[None, :])
    ci = np.arange(t) // CHUNK
    visible = jnp.asarray(ci[None, :] <= ci[:, None])
    dmat = jnp.where(visible[None], jnp.exp(log_g[:, None, None] * dist[None]), 0.0)
    wq = jnp.exp(log_g[None, :] * (idx + 1.0)[:, None])
    we = jnp.exp(log_g[None, :] * (t - 1.0 - idx)[:, None])
    dec = jnp.exp(log_g * t)
    expand = lambda a: jnp.repeat(a, HEAD, axis=1)
    return dmat, expand(wq), expand(we), jnp.broadcast_to(dec[:, None], (nheads, HEAD))


def _retention(q, k, v, g, s0, t, nbatch, nblk, row_block0):
    nheads = s0.shape[1]
    u = nheads * HEAD
    dmat, wq, we, dec = _retention_tables(t, nheads)
    rows = pl.BlockSpec((t, u), lambda b, i: (row_block0 + b * nblk + i, 0))
    whole = lambda a: pl.BlockSpec(a.shape, lambda b, i: (0,) * a.ndim)
    state_spec = pl.BlockSpec((None, nheads, HEAD, HEAD), lambda b, i: (b, 0, 0, 0))
    pbytes = jnp.dtype(q.dtype).itemsize
    est = (2 * (5 * t * u * pbytes + 2 * nheads * HEAD * HEAD * 4 + nheads * t * t * 4 + 2 * t * u * 4)
           + nheads * HEAD * HEAD * 4 + 6 * t * max(t, HEAD) * 4)
    return pl.pallas_call(
        functools.partial(_retention_kernel, nheads=nheads, nblk=nblk),
        grid=(nbatch, nblk),
        in_specs=[rows, rows, rows, rows, state_spec, whole(dmat), whole(wq), whole(we), whole(dec)],
        out_specs=[
            pl.BlockSpec((t, u), lambda b, i: (b * nblk + i, 0)),
            state_spec,
        ],
        out_shape=[
            jax.ShapeDtypeStruct((nbatch * nblk * t, u), q.dtype),
            jax.ShapeDtypeStruct((nbatch, nheads, HEAD, HEAD), F32),
        ],
        scratch_shapes=[pltpu.VMEM((nheads, HEAD, HEAD), F32)],
        compiler_params=_params(("parallel", "arbitrary"), est),
        name="retention",
    )(q, k, v, g, s0, dmat, wq, we, dec)


def _bucket_thresholds():
    nb = N_BUCKETS // 2
    me = nb // 2
    out = []
    for k in range(1, nb - me):
        n = me
        while n ** (nb - me) * me ** k < me ** (nb - me) * MAX_DISTANCE ** k:
            n += 1
        out.append(n)
    return out


def _t5_bucket_np(rel):
    nb = N_BUCKETS // 2
    me = nb // 2
    n = np.abs(rel)
    large = np.full(rel.shape, me, np.int64)
    for thr in _bucket_thresholds():
        large += (n >= thr)
    large = np.minimum(large, nb - 1)
    return (np.where(rel > 0, nb, 0) + np.where(n < me, n, large)).astype(np.int32)


def _bias_kernel(rb_ref, idx_ref, mask_ref, o_ref):
    h = pl.program_id(0)
    idx = idx_ref[...]
    out = mask_ref[...]
    for b in range(N_BUCKETS):
        out = out + jnp.where(idx == b, rb_ref[b, h], 0.0)
    o_ref[...] = out


def _bias_table(rel_bias, qpos, kpos):
    nheads = rel_bias.shape[1]
    rel = kpos[None, :] - qpos[:, None]
    idx = jnp.asarray(_t5_bucket_np(rel))
    mask = jnp.asarray(np.where((kpos[None, :] // CHUNK) <= (qpos[:, None] // CHUNK), 0.0, MASK_VALUE)
                       .astype(np.float32))
    nq, nk = rel.shape
    return pl.pallas_call(
        _bias_kernel,
        grid=(nheads,),
        in_specs=[
            pl.BlockSpec(memory_space=pltpu.SMEM),
            pl.BlockSpec((nq, nk), lambda h: (0, 0)),
            pl.BlockSpec((nq, nk), lambda h: (0, 0)),
        ],
        out_specs=pl.BlockSpec((None, nq, nk), lambda h: (h, 0, 0)),
        out_shape=jax.ShapeDtypeStruct((nheads, nq, nk), F32),
        compiler_params=_params(("arbitrary",), 6 * nq * nk * 4),
        name="relative_bias_table",
    )(rel_bias, idx, mask)


def _stack_components(q, nsub=1):
    lo = lax.broadcasted_iota(jnp.int32, q.shape, 1) < HALF
    zero = jnp.zeros_like(q)
    comps = (jnp.where(lo, q, zero), jnp.where(lo, zero, q))
    rows = q.shape[0] // nsub
    return jnp.concatenate([c[s * rows:(s + 1) * rows] for s in range(nsub) for c in comps], axis=0)


def _add_table(s, bias, nsub=1):
    rows = bias.shape[0] // nsub
    parts = []
    for i in range(nsub):
        b = bias[i * rows:(i + 1) * rows]
        parts += [s[(2 * i + c) * rows:(2 * i + c + 1) * rows] + b for c in range(2)]
    return jnp.concatenate(parts, axis=0)


def _biased_scores(qs, k, bias, nsub=1):
    s = lax.dot_general(qs, k, (((1,), (1,)), ((), ())), preferred_element_type=F32)
    if bias.ndim == 2:
        return _add_table(s, bias, nsub), 0.0
    return s, bias


def _softmax_step(qs, k, v, bias, stats, g):
    m_ref, l_ref, acc_ref = stats
    s, c = _biased_scores(qs, k, bias)
    m_prev = m_ref[g]
    m_new = jnp.maximum(m_prev, jnp.max(s, axis=-1, keepdims=True) + c)
    alpha = jnp.exp(m_prev - m_new)
    p = jnp.exp(s - (m_new - c))
    l_ref[g] = alpha * l_ref[g] + jnp.sum(p, axis=-1, keepdims=True)
    acc_ref[g] = alpha * acc_ref[g] + jnp.dot(p.astype(v.dtype), v, preferred_element_type=F32)
    m_ref[g] = m_new


def _softmax_init(stats):
    m_ref, l_ref, acc_ref = stats
    m_ref[...] = jnp.full_like(m_ref, MASK_VALUE)
    l_ref[...] = jnp.zeros_like(l_ref)
    acc_ref[...] = jnp.zeros_like(acc_ref)


def _softmax_scratch(groups, t):
    return [pltpu.VMEM((groups, 2 * t, 1), F32), pltpu.VMEM((groups, 2 * t, 1), F32),
            pltpu.VMEM((groups, 2 * t, HEAD), F32)]


def _lambda_value(lam_ref, lam_init):
    a = lam_ref[...]
    e1 = jnp.exp(jnp.sum(a[0:1] * a[1:2], axis=-1, keepdims=True))
    e2 = jnp.exp(jnp.sum(a[2:3] * a[3:4], axis=-1, keepdims=True))
    return e1 - e2 + lam_init


def _diff_finish(t, lam, lam_init, subln, stats, g, nsub=1):
    _, l_ref, acc_ref = stats
    acc = acc_ref[g]
    l = l_ref[g]
    rows = t // nsub
    part = lambda a, i: a[i * rows:(i + 1) * rows]
    o = jnp.concatenate([part(acc, 2 * i) / part(l, 2 * i) - lam * (part(acc, 2 * i + 1) / part(l, 2 * i + 1))
                         for i in range(nsub)], axis=0)
    return o * _rms_scale(o) * subln * (1.0 - lam_init)


def _shift_brackets(rb_ref, qg_ref, kg_ref, qs, own_ks, head0, m_ref, nsub=1):
    gain_bound = (SCORE_BOUND_MARGIN * HALF ** 0.5 * jnp.max(jnp.abs(qg_ref[...]), axis=-1, keepdims=True)
                  * jnp.max(jnp.abs(kg_ref[...]), axis=-1, keepdims=True))
    widest = jnp.zeros((1, 1), F32)
    for g, (q, own_k) in enumerate(zip(qs, own_ks)):
        head = head0 + g
        bias_max = rb_ref[0, head]
        for b in range(1, N_BUCKETS):
            bias_max = jnp.maximum(bias_max, rb_ref[b, head])
        upper = gain_bound + bias_max
        own_k = own_k.astype(F32)
        rows = own_k.shape[0] // nsub
        own_stacked = jnp.concatenate([own_k[i * rows:(i + 1) * rows] for i in range(nsub) for _ in range(2)], axis=0)
        own = jnp.sum(q.astype(F32) * own_stacked, axis=-1, keepdims=True)
        lower = own + rb_ref[0, head]
        m_ref[g] = 0.5 * (upper + lower)
        widest = jnp.maximum(widest, jnp.max(upper - lower, axis=0, keepdims=True))
    return widest[0, 0]


def _attn_prompt_kernel(rb_ref, q_ref, k_ref, v_ref, bias_ref, lam_ref, sub_ref, qg_ref, kg_ref, o_ref, *stats,
                        t, groups, nsub, lam_init, far_bucket):
    hp = pl.program_id(1)
    qi = pl.program_id(2)
    m_ref, l_ref, acc_ref = stats
    lanes = [slice(g * HEAD, (g + 1) * HEAD) for g in range(groups)]
    qs = [_stack_components(q_ref[:, sl], nsub) for sl in lanes]
    far_bias = [rb_ref[far_bucket, hp * groups + g] for g in range(groups)]
    _softmax_init(stats)

    def sweep(step, diag_step):
        def run(j, bias_of):
            step(pl.ds(pl.multiple_of(j * t, t), t), [bias_of(g) for g in range(groups)])

        def far_step(j, carry):
            run(j, lambda g: far_bias[g])
            return carry

        lax.fori_loop(0, jnp.maximum(qi - 1, 0), far_step, 0)

        @pl.when(qi > 0)
        def _():
            run(qi - 1, lambda g: bias_ref[g, :, :t])

        diag_step(pl.multiple_of(qi * t, t))

    def max_step(rows, biases):
        scores = [_biased_scores(qs[g], k_ref[rows, lanes[g]], biases[g], nsub) for g in range(groups)]
        for g, (s, c) in enumerate(scores):
            m_ref[g] = jnp.maximum(m_ref[g], jnp.max(s, axis=-1, keepdims=True) + c)

    def acc_step(rows, biases):
        scores = [_biased_scores(qs[g], k_ref[rows, lanes[g]], biases[g], nsub) for g in range(groups)]
        probs = [jnp.exp(s - (m_ref[g] - c)) for g, (s, c) in enumerate(scores)]
        for g, p in enumerate(probs):
            l_ref[g] += jnp.sum(p, axis=-1, keepdims=True)
            v = v_ref[rows, lanes[g]]
            acc_ref[g] += jnp.dot(p.astype(v.dtype), v, preferred_element_type=F32)

    def diag_max_step(start):
        max_step(pl.ds(start, t), [bias_ref[g, :, t:] for g in range(groups)])

    def diag_acc_step(start):
        rows = t // nsub
        parts = [(i, g) for i in range(nsub) for g in range(groups)]
        stacked = lambda i: slice(2 * i * rows, 2 * (i + 1) * rows)
        keys = lambda i: pl.ds(start, (i + 1) * rows)
        scores = []
        for i, g in parts:
            table = bias_ref[g, i * rows:(i + 1) * rows, t:t + (i + 1) * rows]
            scores.append(_biased_scores(qs[g][stacked(i)], k_ref[keys(i), lanes[g]], table)[0])
        probs = [jnp.exp(s - m_ref[g, stacked(i)]) for (i, g), s in zip(parts, scores)]
        for (i, g), p in zip(parts, probs):
            l_ref[g, stacked(i)] += jnp.sum(p, axis=-1, keepdims=True)
            v = v_ref[keys(i), lanes[g]]
            acc_ref[g, stacked(i)] += jnp.dot(p.astype(v.dtype), v, preferred_element_type=F32)

    own_rows = pl.ds(pl.multiple_of(qi * t, t), t)
    widest = _shift_brackets(rb_ref, qg_ref, kg_ref, qs, [k_ref[own_rows, sl] for sl in lanes],
                             hp * groups, m_ref, nsub)

    @pl.when(widest > MAX_SHIFT_BRACKET)
    def _():
        m_ref[...] = jnp.full_like(m_ref, MASK_VALUE)
        sweep(max_step, diag_max_step)

    sweep(acc_step, diag_acc_step)

    lam = _lambda_value(lam_ref, lam_init)
    for g, sl in enumerate(lanes):
        o_ref[:, sl] = _diff_finish(t, lam, lam_init, sub_ref[...], stats, g, nsub).astype(o_ref.dtype)


def _attn_prompt(q, k, v, rel_bias, lam_params, subln2, qn2, kn2, layer, nbatch, seq, nheads, lam_init):
    u = nheads * HEAD
    t = _pick(seq, ATTN_BLOCKS)
    groups = _pick(nheads, HEADS_PER_STEP)
    assert t % CHUNK == 0 and t + 1 >= _bucket_thresholds()[-1]
    nsub = QUERY_SUBBLOCKS if (t // QUERY_SUBBLOCKS) % max(CHUNK, HEAD) == 0 else 1
    nq = seq // t
    w = groups * HEAD
    r = np.arange(t)
    bias = _bias_table(rel_bias, r + t, np.arange(2 * t))
    pbytes = jnp.dtype(q.dtype).itemsize
    est = (2 * (2 * t * w * pbytes + 2 * seq * w * pbytes + groups * 2 * t * t * 4)
           + groups * (2 * t * (HEAD + 2 * 128) * 4 + 6 * 2 * t * t * 4))
    return pl.pallas_call(
        functools.partial(_attn_prompt_kernel, t=t, groups=groups, nsub=nsub, lam_init=lam_init,
                          far_bucket=N_BUCKETS // 2 - 1),
        grid=(nbatch, nheads // groups, nq),
        in_specs=[
            pl.BlockSpec(memory_space=pltpu.SMEM),
            pl.BlockSpec((t, w), lambda b, h, i: (b * nq + i, h)),
            pl.BlockSpec((seq, w), lambda b, h, i: (b, h)),
            pl.BlockSpec((seq, w), lambda b, h, i: (b, h)),
            pl.BlockSpec((groups, t, 2 * t), lambda b, h, i: (h, 0, 0)),
            pl.BlockSpec((None, 4, HALF), lambda b, h, i: (layer, 0, 0)),
            pl.BlockSpec((None, 1, HEAD), lambda b, h, i: (layer, 0, 0)),
            pl.BlockSpec((None, 1, HEAD), lambda b, h, i: (layer, 0, 0)),
            pl.BlockSpec((None, 1, HEAD), lambda b, h, i: (layer, 0, 0)),
        ],
        out_specs=pl.BlockSpec((t, w), lambda b, h, i: (b * nq + i, h)),
        out_shape=jax.ShapeDtypeStruct((nbatch * seq, u), q.dtype),
        scratch_shapes=_softmax_scratch(groups, t),
        compiler_params=_params(("parallel", "parallel", "arbitrary"), est),
        name="diff_attention_prompt",
    )(rel_bias, q, k, v, bias, lam_params, subln2, qn2, kn2)


def _attn_sample_kernel(rb_ref, q_ref, kn_ref, vn_ref, kc_ref, vc_ref, bc_ref, bn_ref, lam_ref, sub_ref,
                        qg_ref, kg_ref, o_ref, *stats, t, tk, ncache, nheads, lam_init):
    *stats, wide_ref = stats
    m_ref, l_ref, acc_ref = stats
    j = pl.program_id(1)
    lanes = [slice(h * HEAD, (h + 1) * HEAD) for h in range(nheads)]
    heads = range(nheads)
    qs = [_stack_components(q_ref[:, sl]) for sl in lanes]

    @pl.when(j == 0)
    def _():
        _softmax_init(stats)
        widest = _shift_brackets(rb_ref, qg_ref, kg_ref, qs, [kn_ref[:, sl] for sl in lanes], 0, m_ref)
        wide_ref[0] = (widest > MAX_SHIFT_BRACKET).astype(jnp.int32)

        @pl.when(widest > MAX_SHIFT_BRACKET)
        def _():
            m_ref[...] = jnp.full_like(m_ref, MASK_VALUE)

    def fixed_shift_steps(ks, vs, biases):
        scores = [_biased_scores(qs[h], ks[h], biases[h])[0] for h in heads]
        probs = [jnp.exp(scores[h] - m_ref[h]) for h in heads]
        for h in heads:
            l_ref[h] += jnp.sum(probs[h], axis=-1, keepdims=True)
            acc_ref[h] += jnp.dot(probs[h].astype(vs[h].dtype), vs[h], preferred_element_type=F32)

    def online_steps(ks, vs, biases):
        for h in heads:
            _softmax_step(qs[h], ks[h], vs[h], biases[h], stats, h)

    def both(ks, vs, biases):
        @pl.when(wide_ref[0] == 0)
        def _():
            fixed_shift_steps(ks(), vs(), biases())

        @pl.when(wide_ref[0] != 0)
        def _():
            online_steps(ks(), vs(), biases())

    head_rows = lambda ref: [ref[pl.ds(h, tk, stride=nheads), :].astype(qs[0].dtype) for h in heads]
    both(lambda: head_rows(kc_ref), lambda: head_rows(vc_ref), lambda: [bc_ref[h] for h in heads])

    @pl.when(j == ncache - 1)
    def _():
        both(lambda: [kn_ref[:, sl] for sl in lanes], lambda: [vn_ref[:, sl] for sl in lanes],
             lambda: [bn_ref[h] for h in heads])
        lam = _lambda_value(lam_ref, lam_init)
        for h, sl in enumerate(lanes):
            o_ref[:, sl] = _diff_finish(t, lam, lam_init, sub_ref[...], stats, h).astype(o_ref.dtype)


def _attn_sample(q, k, v, cache_k, cache_v, rel_bias, lam_params, subln2, qn2, kn2, layer, nbatch, t, past,
                 row0, nheads, lam_init):
    u = nheads * HEAD
    assert row0 % t == 0
    rb0 = row0 // t
    tk = _pick(past, CACHE_TILES)
    qpos = past + np.arange(t)
    ncache = past // tk
    bias_c = _bias_table(rel_bias, qpos, np.arange(past))
    bias_c = bias_c.reshape(nheads, t, ncache, tk).transpose(2, 0, 1, 3)
    bias_n = _bias_table(rel_bias, qpos, qpos)
    pbytes = jnp.dtype(q.dtype).itemsize
    est = (2 * (4 * t * u * pbytes + 2 * tk * nheads * HEAD * 4 + nheads * t * (tk + t) * 4)
           + nheads * (2 * t * (HEAD + 2 * 128) * 4 + 6 * 2 * t * tk * 4))
    rows = pl.BlockSpec((t, u), lambda b, j: (rb0 + b, 0))
    cache = pl.BlockSpec((None, tk * nheads, HEAD), lambda b, j: (layer, b * ncache + j, 0))
    return pl.pallas_call(
        functools.partial(_attn_sample_kernel, t=t, tk=tk, ncache=ncache, nheads=nheads, lam_init=lam_init),
        grid=(nbatch, ncache),
        in_specs=[
            pl.BlockSpec(memory_space=pltpu.SMEM),
            rows, rows, rows, cache, cache,
            pl.BlockSpec((None, nheads, t, tk), lambda b, j: (j, 0, 0, 0)),
            pl.BlockSpec((nheads, t, t), lambda b, j: (0, 0, 0)),
            pl.BlockSpec((None, 4, HALF), lambda b, j: (layer, 0, 0)),
            pl.BlockSpec((None, 1, HEAD), lambda b, j: (layer, 0, 0)),
            pl.BlockSpec((None, 1, HEAD), lambda b, j: (layer, 0, 0)),
            pl.BlockSpec((None, 1, HEAD), lambda b, j: (layer, 0, 0)),
        ],
        out_specs=pl.BlockSpec((t, u), lambda b, j: (b, 0)),
        out_shape=jax.ShapeDtypeStruct((nbatch * t, u), q.dtype),
        scratch_shapes=_softmax_scratch(nheads, t) + [pltpu.SMEM((1,), jnp.int32)],
        compiler_params=_params(("parallel", "arbitrary"), est),
        name="diff_attention_sample",
    )(rel_bias, q, k, v, cache_k, cache_v, bias_c, bias_n, lam_params, subln2, qn2, kn2)


def _merge_kernel(ap_ref, as_ref, bp_ref, bs_ref, ga0_ref, ga1_ref, gb0_ref, gb1_ref, wa_ref, wb_ref, o_ref,
                  *, u, n_first):
    a = _group_tile((ap_ref, as_ref), n_first)
    b = _group_tile((bp_ref, bs_ref), n_first)
    ya = jnp.dot(a, wa_ref[...], preferred_element_type=F32)
    yb = jnp.dot(b, wb_ref[...], preferred_element_type=F32)
    for c, (ga, gb) in enumerate(((ga0_ref, gb0_ref), (ga1_ref, gb1_ref))):
        sl = slice(c * u, (c + 1) * u)
        o_ref[:, sl] = (ga[...].astype(F32) * ya[:, sl] + gb[...].astype(F32) * yb[:, sl]).astype(o_ref.dtype)


def _merge(ret_outs, dif_outs, gates, w_ret_up, w_dif_up, layer, tm):
    m = gates.shape[0]
    u = ret_outs[0].shape[1]
    d = 2 * u
    n_first = ret_outs[0].shape[0] // tm
    gate = lambda c: pl.BlockSpec((tm, u), lambda i: (i, c))
    rows = _group_specs(tm, u, n_first)
    wspec = pl.BlockSpec((None, u, d), lambda i: (layer, 0, 0))
    pbytes = jnp.dtype(gates.dtype).itemsize
    est = 2 * (8 * tm * u * pbytes + 2 * u * d * pbytes + tm * d * pbytes) + 3 * tm * d * 4
    return pl.pallas_call(
        functools.partial(_merge_kernel, u=u, n_first=n_first),
        grid=(m // tm,),
        in_specs=rows + rows + [gate(0), gate(1), gate(2), gate(3), wspec, wspec],
        out_specs=pl.BlockSpec((tm, d), lambda i: (i, 0)),
        out_shape=jax.ShapeDtypeStruct((m, d), gates.dtype),
        compiler_params=_params(("arbitrary",), est),
        name="gated_merge",
    )(*ret_outs, *dif_outs, gates, gates, gates, gates, w_ret_up, w_dif_up)


def _out_proj_kernel(x_ref, a_ref, w_ref, o_ref):
    o_ref[...] = x_ref[...] + jnp.dot(a_ref[...], w_ref[...], preferred_element_type=F32)


def _out_proj(x, merged, w_out, layer, tm):
    m, d = x.shape
    wbytes = jnp.dtype(w_out.dtype).itemsize
    est = 2 * (2 * tm * d * 4 + tm * d * wbytes + d * d * wbytes) + tm * d * 4
    return pl.pallas_call(
        _out_proj_kernel,
        grid=(m // tm,),
        in_specs=[
            pl.BlockSpec((tm, d), lambda i: (i, 0)),
            pl.BlockSpec((tm, d), lambda i: (i, 0)),
            pl.BlockSpec((None, d, d), lambda i: (layer, 0, 0)),
        ],
        out_specs=pl.BlockSpec((tm, d), lambda i: (i, 0)),
        out_shape=jax.ShapeDtypeStruct((m, d), F32),
        compiler_params=_params(("parallel",), est),
        name="output_projection",
    )(x, merged, w_out)


def _rotary_tables(pos):
    inv = ROPE_BASE ** (-jnp.arange(HALF, dtype=F32) / HALF)
    ang = pos.astype(F32)[:, None] * inv[None, :]
    cos, sin = jnp.cos(ang), jnp.sin(ang)
    return jnp.concatenate([cos, cos], axis=-1), jnp.concatenate([-sin, sin], axis=-1)


def kernel(x_prompt, x_sample, cache_diff_k, cache_diff_v, state_ret, ffn1_norm, ffn1_gate, ffn1_up, ffn1_down, mix_norm, w_in, q_norm, k_norm, lambda_q1, lambda_k1, lambda_q2, lambda_k2, subln, w_ret_up, w_dif_up, w_out, ffn2_norm, ffn2_gate, ffn2_up, ffn2_down, rel_bias):
    nb, seq, d = x_prompt.shape
    db, dseq, _ = x_sample.shape
    depth, _, past, nh_d, _ = cache_diff_k.shape
    nh_r = state_ret.shape[2]
    u = d // 2
    assert nh_r * HEAD == u and nh_d * HEAD == u and dseq == CHUNK and seq % CHUNK == 0
    assert w_in.shape[-1] == N_SEG * u
    mp, ms = nb * seq, db * dseq

    cast = lambda w: w.astype(MXU_DTYPE)
    wg1, wu1, wd1 = cast(ffn1_gate), cast(ffn1_up), cast(ffn1_down)
    wg2, wu2, wd2 = cast(ffn2_gate), cast(ffn2_up), cast(ffn2_down)
    w_ret_c, w_dif_c, w_out_c = cast(w_ret_up), cast(w_dif_up), cast(w_out)
    row3 = lambda g: g.reshape(depth, 1, g.shape[-1])
    n1, nmix, n2 = row3(ffn1_norm), row3(mix_norm), row3(ffn2_norm)
    qn2 = row3(jnp.concatenate([q_norm, q_norm], axis=-1))
    kn2 = row3(jnp.concatenate([k_norm, k_norm], axis=-1))
    subln2 = row3(subln)
    lam_params = jnp.stack([lambda_q1, lambda_k1, lambda_q2, lambda_k2], axis=1)
    cache_k = cache_diff_k.reshape(depth, db * past * nh_d, HEAD)
    cache_v = cache_diff_v.reshape(depth, db * past * nh_d, HEAD)

    pos = jnp.concatenate([jnp.tile(jnp.arange(seq, dtype=jnp.int32), nb),
                           past + jnp.tile(jnp.arange(dseq, dtype=jnp.int32), db)])
    rot = _rotary_tables(pos)

    tm = _pick(math.gcd(mp, ms), ROW_TILES)
    tm_proj = _pick(mp + ms, PROJ_ROW_TILES)
    groups = (mp, ms)
    zero_state = jnp.zeros((nb, nh_r, HEAD, HEAD), F32)
    t_ret = _pick(seq, RET_BLOCKS)

    x = (x_prompt.reshape(mp, d), x_sample.reshape(ms, d))
    kps, kss, vps, vss, states_p, states_s = [], [], [], [], [], []
    for l in range(depth):
        lam_init = 0.8 - 0.6 * math.exp(-0.3 * l)
        x, h = _ffn(x, n1, wg1, wu1, wd1, l, tm, next_gain=nmix)

        last = l == depth - 1
        seg = functools.partial(_segment, h, w_in, l)
        rq = seg(SEG_RQ, 1, tm_proj, "rotary", tables=rot)
        rk = seg(SEG_RK, 1, tm_proj, "rotary", tables=rot, scale=HEAD ** -0.5)
        rv = seg(SEG_RV, 1, tm_proj, "cast")
        rg = seg(SEG_RG, 1, tm_proj, "silu")
        dq = seg(SEG_DQ, 1, tm_proj, "norm", gain=qn2, scale=HALF ** -0.5)
        dk, k_p, k_s = seg(SEG_DK, 1, tm, "norm_keep", gain=kn2, group_rows=groups,
                           prev_prompt=kps if last else ())
        dv, v_p, v_s = seg(SEG_DV, 1, tm, "keep", group_rows=groups, prev_prompt=vps if last else ())
        gates = seg(SEG_GATES, N_SEG - SEG_GATES, tm_proj, "sigmoid")

        ret_p, st_p = _retention(rq, rk, rv, rg, zero_state, t_ret, nb, seq // t_ret, 0)
        ret_s, st_s = _retention(rq, rk, rv, rg, state_ret[l].astype(F32), dseq, db, 1, mp // dseq)

        dif_p = _attn_prompt(dq, dk, dv, rel_bias, lam_params, subln2, qn2, kn2, l, nb, seq, nh_d, lam_init)
        dif_s = _attn_sample(dq, dk, dv, cache_k, cache_v, rel_bias, lam_params, subln2, qn2, kn2, l, db, dseq,
                             past, mp, nh_d, lam_init)

        merged = _merge((ret_p, ret_s), (dif_p, dif_s), gates, w_ret_c, w_dif_c, l, tm)
        x = _out_proj(x, merged, w_out_c, l, tm)
        x = _ffn(x, n2, wg2, wu2, wd2, l, tm, split_out=groups if l == depth - 1 else None)

        for acc, val in ((kps, k_p), (kss, k_s), (vps, v_p), (vss, v_s), (states_p, st_p), (states_s, st_s)):
            acc.append(val)

    y_p, y_s = x
    kv_p = lambda parts: parts[-1].reshape(depth, nb, seq, nh_d, HEAD)
    kv_s = lambda parts: jnp.stack(parts).reshape(depth, db, dseq, nh_d, HEAD)
    return (y_p.reshape(nb, seq, d), y_s.reshape(db, dseq, d),
            kv_p(kps).astype(cache_diff_k.dtype), kv_p(vps).astype(cache_diff_v.dtype),
            jnp.stack(states_p).astype(state_ret.dtype),
            kv_s(kss).astype(cache_diff_k.dtype), kv_s(vss).astype(cache_diff_v.dtype),
            jnp.stack(states_s).astype(state_ret.dtype))
```
